```python
import math
import jax, jax.numpy as jnp
from jax import lax
import numpy as np

D_MODEL = 2048
BATCH = 4
SEQ = 4096
DEPTH = 2
DEC_BATCH = 16
DEC_SEQ = 32
PAST_LEN = 1024

CHUNK = 64
HEAD_DIM = 128
D_TOK = 3 * D_MODEL // 4
D_MEM = D_MODEL // 4
SB_HEADS = D_TOK // HEAD_DIM
MEM_HEADS = D_MEM // HEAD_DIM
N_MEM = 256
POOL_WINDOWS = (2, 4, 8, 16)
POOL_GROUP = D_TOK // len(POOL_WINDOWS)
POOL_STATE = max(POOL_WINDOWS) - 1
D_FF = 4 * D_MODEL
N_POOL_LAYERS = (DEPTH + 1) // 2
N_SB_LAYERS = DEPTH // 2
Q_BLOCK = 128
EPS = 1e-6

kernel_name = 'hybrid_pool_stickbreak_stream_step'


def rmsnorm(x, g):
    xf = x.astype(jnp.float32)
    y = xf * lax.rsqrt(jnp.mean(xf * xf, axis=-1, keepdims=True) + EPS) * g.astype(jnp.float32)
    return y.astype(x.dtype)


def pool_mix(u, prev, past_len, w_grp, scale):
    B, T, C = u.shape
    P = POOL_STATE
    ext = jnp.concatenate([prev.astype(u.dtype), u], axis=1).astype(jnp.float32)
    cs = jnp.concatenate([jnp.zeros((B, 1, C), jnp.float32), jnp.cumsum(ext, axis=1)], axis=1)
    pos = (past_len + jnp.arange(T)).astype(jnp.float32)
    cur = ext[:, P:]
    outs = []
    for g, w in enumerate(POOL_WINDOWS):
        sl = slice(g * POOL_GROUP, (g + 1) * POOL_GROUP)
        win = cs[:, P + 1:P + 1 + T, sl] - cs[:, P + 1 - w:P + 1 - w + T, sl]
        cnt = jnp.minimum(pos + 1.0, float(w))
        d = (win / cnt[None, :, None] - cur[..., sl]).astype(u.dtype)
        outs.append(jnp.einsum('btc,cd->btd', d, w_grp[g]))
    out = jnp.concatenate(outs, axis=-1) * scale
    new_prev = ext[:, -P:].astype(u.dtype)
    return out, new_prev


def sb_attention(q, k, v, past_len):
    B, T, H, Dh = q.shape
    S = k.shape[1]
    kpos = jnp.arange(S)
    qpos = past_len + jnp.arange(T)
    inv = 1.0 / math.sqrt(Dh)

    def block(qb, qp):
        z = jnp.einsum('bqhd,bshd->bhqs', qb, k).astype(jnp.float32) * inv
        mask = (kpos[None, :] < qp[:, None])[None, None]
        log_fail = jnp.where(mask, jax.nn.log_sigmoid(-z), 0.0)
        after = lax.cumsum(log_fail, axis=3, reverse=True) - log_fail
        a = jnp.where(mask, jnp.exp(jax.nn.log_sigmoid(z) + after), 0.0)
        return jnp.einsum('bhqs,bshd->bqhd', a.astype(v.dtype), v)

    if T % Q_BLOCK == 0 and T > Q_BLOCK:
        nb = T // Q_BLOCK
        qs = q.reshape(B, nb, Q_BLOCK, H, Dh).transpose(1, 0, 2, 3, 4)
        ps = qpos.reshape(nb, Q_BLOCK)
        out = lax.map(lambda a: block(a[0], a[1]), (qs, ps))
        return out.transpose(1, 0, 2, 3, 4).reshape(B, T, H, Dh)
    return block(q, qpos)


def head_rmsnorm(x, g):
    return rmsnorm(x, g)


def mem_kv(mem, g_mem, w_kv, k_gain):
    B, N, _ = mem.shape
    kv = rmsnorm(mem, g_mem) @ w_kv
    k = head_rmsnorm(kv[..., :D_MEM].reshape(B, N, MEM_HEADS, HEAD_DIM), k_gain)
    v = kv[..., D_MEM:].reshape(B, N, MEM_HEADS, HEAD_DIM)
    return k, v


def mem_attention(q, mk, mv):
    s = jnp.einsum('bthd,bmhd->bhtm', q, mk).astype(jnp.float32) / math.sqrt(HEAD_DIM)
    p = jax.nn.softmax(s, axis=-1)
    return jnp.einsum('bhtm,bmhd->bthd', p.astype(mv.dtype), mv)


def run_trunk(x, past_len, pool_prev, sb_past_k, sb_past_v, mem_k, mem_v,
              norm_mix, w_in_pool, w_in_sb, pool_w, pool_scale, q_norm,
              w_out, norm_ffn, w_up, w_down):
    B, T, _ = x.shape
    pool_new, sbk_new, sbv_new = [], [], []
    for i in range(DEPTH):
        j = i // 2
        h = rmsnorm(x, norm_mix[i])
        if i % 2 == 0:
            proj = h @ w_in_pool[j]
            mix, new_prev = pool_mix(proj[..., :D_TOK], pool_prev[j], past_len, pool_w[j], pool_scale[j])
            pool_new.append(new_prev)
            qm = proj[..., D_TOK:]
        else:
            proj = h @ w_in_sb[j]
            q = proj[..., :D_TOK].reshape(B, T, SB_HEADS, HEAD_DIM)
            k = proj[..., D_TOK:2 * D_TOK].reshape(B, T, SB_HEADS, HEAD_DIM)
            v = proj[..., 2 * D_TOK:3 * D_TOK].reshape(B, T, SB_HEADS, HEAD_DIM)
            kk = jnp.concatenate([sb_past_k[j].astype(k.dtype), k], axis=1)
            vv = jnp.concatenate([sb_past_v[j].astype(v.dtype), v], axis=1)
            mix = sb_attention(q, kk, vv, past_len).reshape(B, T, D_TOK)
            sbk_new.append(k)
            sbv_new.append(v)
            qm = proj[..., 3 * D_TOK:]
        qm = head_rmsnorm(qm.reshape(B, T, MEM_HEADS, HEAD_DIM), q_norm[i])
        mo = mem_attention(qm, mem_k[i].astype(qm.dtype), mem_v[i].astype(qm.dtype)).reshape(B, T, D_MEM)
        x = x + jnp.concatenate([mix, mo], axis=-1) @ w_out[i]
        h2 = rmsnorm(x, norm_ffn[i])
        x = x + jnp.square(jax.nn.relu(h2 @ w_up[i])) @ w_down[i]
    return x, jnp.stack(pool_new), jnp.stack(sbk_new), jnp.stack(sbv_new)


def setup_inputs(seed: int = 0) -> dict:
    key = jax.random.key(seed)
    ks = jax.random.split(key, 24)
    f32 = jnp.float32
    nrm = lambda k, s, sc=1.0: jax.random.normal(k, s, f32) * sc
    gain = lambda k, s: 1.0 + 0.05 * jax.random.normal(k, s, f32)
    return {
        'x_prompt': nrm(ks[0], (BATCH, SEQ, D_MODEL)),
        'x_sample': nrm(ks[1], (DEC_BATCH, DEC_SEQ, D_MODEL)),
        'mem_prompt': nrm(ks[2], (BATCH, N_MEM, D_MODEL)),
        'cache_pool': nrm(ks[3], (N_POOL_LAYERS, DEC_BATCH, POOL_STATE, D_TOK)),
        'cache_sb_k': nrm(ks[4], (N_SB_LAYERS, DEC_BATCH, PAST_LEN, SB_HEADS, HEAD_DIM)),
        'cache_sb_v': nrm(ks[5], (N_SB_LAYERS, DEC_BATCH, PAST_LEN, SB_HEADS, HEAD_DIM)),
        'cache_mem_k': nrm(ks[6], (DEPTH, DEC_BATCH, N_MEM, MEM_HEADS, HEAD_DIM)),
        'cache_mem_v': nrm(ks[7], (DEPTH, DEC_BATCH, N_MEM, MEM_HEADS, HEAD_DIM)),
        'norm_mix': gain(ks[8], (DEPTH, D_MODEL)),
        'w_in_pool': nrm(ks[9], (N_POOL_LAYERS, D_MODEL, D_TOK + D_MEM), D_MODEL ** -0.5),
        'w_in_sb': nrm(ks[10], (N_SB_LAYERS, D_MODEL, 3 * D_TOK + D_MEM), D_MODEL ** -0.5),
        'pool_w': nrm(ks[11], (N_POOL_LAYERS, len(POOL_WINDOWS), POOL_GROUP, POOL_GROUP), POOL_GROUP ** -0.5),
        'pool_scale': 1.0 + 0.1 * jax.random.normal(ks[12], (N_POOL_LAYERS, D_TOK), f32),
        'norm_mem': gain(ks[13], (DEPTH, D_MODEL)),
        'w_mem_kv': nrm(ks[14], (DEPTH, D_MODEL, 2 * D_MEM), D_MODEL ** -0.5),
        'q_norm': gain(ks[15], (DEPTH, HEAD_DIM)),
        'k_norm': gain(ks[16], (DEPTH, HEAD_DIM)),
        'w_out': nrm(ks[17], (DEPTH, D_TOK + D_MEM, D_MODEL), (D_TOK + D_MEM) ** -0.5),
        'norm_ffn': gain(ks[18], (DEPTH, D_MODEL)),
        'w_up': nrm(ks[19], (DEPTH, D_MODEL, D_FF), D_MODEL ** -0.5),
        'w_down': nrm(ks[20], (DEPTH, D_FF, D_MODEL), D_FF ** -0.5),
    }


def reference(x_prompt, x_sample, mem_prompt, cache_pool, cache_sb_k, cache_sb_v,
              cache_mem_k, cache_mem_v, norm_mix, w_in_pool, w_in_sb, pool_w, pool_scale,
              norm_mem, w_mem_kv, q_norm, k_norm, w_out, norm_ffn, w_up, w_down):
    mks, mvs = [], []
    for i in range(DEPTH):
        mk, mv = mem_kv(mem_prompt, norm_mem[i], w_mem_kv[i], k_norm[i])
        mks.append(mk)
        mvs.append(mv)
    mem_k_prompt = jnp.stack(mks)
    mem_v_prompt = jnp.stack(mvs)
    B = x_prompt.shape[0]
    pool_zero = jnp.zeros((N_POOL_LAYERS, B, POOL_STATE, D_TOK), x_prompt.dtype)
    sb_empty = jnp.zeros((N_SB_LAYERS, B, 0, SB_HEADS, HEAD_DIM), x_prompt.dtype)
    y_prompt, pool_prompt, sb_k_prompt, sb_v_prompt = run_trunk(
        x_prompt, 0, pool_zero, sb_empty, sb_empty, mem_k_prompt, mem_v_prompt,
        norm_mix, w_in_pool, w_in_sb, pool_w, pool_scale, q_norm,
        w_out, norm_ffn, w_up, w_down)
    past = cache_sb_k.shape[2]
    y_sample, pool_sample, sb_k_sample, sb_v_sample = run_trunk(
        x_sample, past, cache_pool, cache_sb_k, cache_sb_v, cache_mem_k, cache_mem_v,
        norm_mix, w_in_pool, w_in_sb, pool_w, pool_scale, q_norm,
        w_out, norm_ffn, w_up, w_down)
    return (y_prompt, y_sample, pool_prompt, pool_sample, sb_k_prompt, sb_v_prompt,
            sb_k_sample, sb_v_sample, mem_k_prompt, mem_v_prompt)
```

```python
import functools
import math

import jax
import jax.numpy as jnp
from jax import lax
from jax.experimental import pallas as pl
from jax.experimental.pallas import tpu as pltpu

D_MODEL = 2048
HEAD_DIM = 128
D_TOK = 3 * D_MODEL // 4
D_MEM = D_MODEL // 4
SB_HEADS = D_TOK // HEAD_DIM
MEM_HEADS = D_MEM // HEAD_DIM
POOL_WINDOWS = (2, 4, 8, 16)
POOL_GROUP = D_TOK // len(POOL_WINDOWS)
POOL_STATE = max(POOL_WINDOWS) - 1
POOL_HALO = POOL_STATE + 1
D_FF = 4 * D_MODEL
EPS = 1e-6
INV_SQRT_HD = 1.0 / math.sqrt(HEAD_DIM)

SB_BLOCK = 128
VMEM_LIMIT = 48 * 1024 * 1024

BF16 = jnp.bfloat16
F32 = jnp.float32


def _params(*sem):
    return pltpu.CompilerParams(dimension_semantics=sem, vmem_limit_bytes=VMEM_LIMIT)


def _rms(xf, g):
    return xf * lax.rsqrt(jnp.mean(xf * xf, axis=-1, keepdims=True) + EPS) * g


def _dot(a, b):
    return jnp.dot(a, b, preferred_element_type=F32)


def _dot_nt(a, b):
    return lax.dot_general(a, b, (((1,), (1,)), ((), ())), preferred_element_type=F32)


def _norm_matmul_body(x_ref, g_ref, w_ref, o_ref, h_ref):
    @pl.when(pl.program_id(1) == 0)
    def _():
        h_ref[...] = _rms(x_ref[...], g_ref[...]).astype(BF16)

    o_ref[...] = _dot(h_ref[...], w_ref[...])


def _norm_matmul(x, g, w, tn=512):
    n, d = x.shape
    m = w.shape[1]
    tm = min(n, 1024)
    return pl.pallas_call(
        _norm_matmul_body,
        grid=(n // tm, m // tn),
        in_specs=[
            pl.BlockSpec((tm, d), lambda i, j: (i, 0)),
            pl.BlockSpec((1, d), lambda i, j: (0, 0)),
            pl.BlockSpec((d, tn), lambda i, j: (0, j)),
        ],
        out_specs=pl.BlockSpec((tm, tn), lambda i, j: (i, j)),
        out_shape=jax.ShapeDtypeStruct((n, m), F32),
        scratch_shapes=[pltpu.VMEM((tm, d), BF16)],
        compiler_params=_params("parallel", "arbitrary"),
        name="norm_matmul",
    )(x, g.reshape(1, d), w)


def _inproj_sb_body(x_ref, g_ref, wq_ref, wk_ref, wv_ref, wm_ref,
                    q_ref, k_ref, v_ref, kb_ref, vb_ref, qm_ref, h_ref):
    @pl.when(pl.program_id(1) == 0)
    def _():
        h_ref[...] = _rms(x_ref[...], g_ref[...]).astype(BF16)

    h = h_ref[...]
    q_ref[...] = (_dot(h, wq_ref[...]) * INV_SQRT_HD).astype(BF16)
    k = _dot(h, wk_ref[...])
    k_ref[...] = k
    kb_ref[...] = k.astype(BF16)
    v = _dot(h, wv_ref[...])
    v_ref[...] = v
    vb_ref[...] = v.astype(BF16)
    qm_ref[...] = _dot(h, wm_ref[...])


def _inproj_sb(x, g, w):
    n, d = x.shape
    tm = min(n, 1024)
    nj = 4
    tn = D_TOK // nj
    tq = D_MEM // nj
    tok = lambda dt: jax.ShapeDtypeStruct((n, D_TOK), dt)
    wspec = lambda off: pl.BlockSpec((d, tn), lambda i, j: (0, off + j))
    ospec = pl.BlockSpec((tm, tn), lambda i, j: (i, j))
    return pl.pallas_call(
        _inproj_sb_body,
        grid=(n // tm, nj),
        in_specs=[
            pl.BlockSpec((tm, d), lambda i, j: (i, 0)),
            pl.BlockSpec((1, d), lambda i, j: (0, 0)),
            wspec(0), wspec(nj), wspec(2 * nj),
            pl.BlockSpec((d, tq), lambda i, j: (0, 3 * D_TOK // tq + j)),
        ],
        out_specs=[ospec, ospec, ospec, ospec, ospec,
                   pl.BlockSpec((tm, tq), lambda i, j: (i, j))],
        out_shape=[tok(BF16), tok(F32), tok(F32), tok(BF16), tok(BF16),
                   jax.ShapeDtypeStruct((n, D_MEM), F32)],
        scratch_shapes=[pltpu.VMEM((tm, d), BF16)],
        compiler_params=_params("parallel", "arbitrary"),
        name="inproj_sb",
    )(x, g.reshape(1, d), w, w, w, w)


def _mem_kv_body(x_ref, g_ref, w_ref, kg_ref, k_ref, v_ref):
    h = _rms(x_ref[...], g_ref[...]).astype(BF16)
    kv = _dot(h, w_ref[...])
    for hd in range(MEM_HEADS):
        sl = slice(hd * HEAD_DIM, (hd + 1) * HEAD_DIM)
        k_ref[:, sl] = _rms(kv[:, sl], kg_ref[...])
    v_ref[...] = kv[:, D_MEM:]


def _mem_kv(mem, g_mem, w_kv, k_gain):
    r, d = mem.shape
    depth = w_kv.shape[0]
    tm = min(r, 256)
    out = jax.ShapeDtypeStruct((depth, r, D_MEM), F32)
    ospec = pl.BlockSpec((None, tm, D_MEM), lambda l, i: (l, i, 0))
    return pl.pallas_call(
        _mem_kv_body,
        grid=(depth, r // tm),
        in_specs=[
            pl.BlockSpec((tm, d), lambda l, i: (i, 0)),
            pl.BlockSpec((None, 1, d), lambda l, i: (l, 0, 0)),
            pl.BlockSpec((None, d, 2 * D_MEM), lambda l, i: (l, 0, 0)),
            pl.BlockSpec((None, 1, HEAD_DIM), lambda l, i: (l, 0, 0)),
        ],
        out_specs=[ospec, ospec],
        out_shape=[out, out],
        compiler_params=_params("parallel", "parallel"),
        name="mem_kv",
    )(mem, g_mem.reshape(depth, 1, d), w_kv, k_gain.reshape(depth, 1, HEAD_DIM))


def _pool_body(u_ref, halo_ref, prev_ref, w_ref, s_ref, o_ref, ext_ref, *, tm, past):
    i = pl.program_id(1)

    @pl.when(i == 0)
    def _():
        ext_ref[0:POOL_HALO, :] = prev_ref[...]

    @pl.when(i > 0)
    def _():
        ext_ref[0:POOL_HALO, :] = halo_ref[...]

    ext_ref[POOL_HALO:, :] = u_ref[...]
    pos = (past + i * tm + lax.broadcasted_iota(jnp.int32, (tm, 1), 0)).astype(F32)
    for g, w in enumerate(POOL_WINDOWS):
        sl = slice(g * POOL_GROUP, (g + 1) * POOL_GROUP)
        cur = ext_ref[POOL_HALO:, sl]
        win = cur
        for back in range(1, w):
            win = win + ext_ref[POOL_HALO - back:POOL_HALO - back + tm, sl]
        cnt = jnp.minimum(pos + 1.0, float(w))
        dlt = (win / cnt - cur).astype(BF16)
        o_ref[:, sl] = (_dot(dlt, w_ref[g]) * s_ref[:, sl]).astype(BF16)


def _pool_mix(proj, prev, w_grp, scale, past):
    b, t, _ = proj.shape
    tm = min(t, 512)
    hb = tm // POOL_HALO
    return pl.pallas_call(
        functools.partial(_pool_body, tm=tm, past=past),
        grid=(b, t // tm),
        in_specs=[
            pl.BlockSpec((None, tm, D_TOK), lambda bi, i: (bi, i, 0)),
            pl.BlockSpec((None, POOL_HALO, D_TOK), lambda bi, i: (bi, jnp.maximum(i * hb - 1, 0), 0)),
            pl.BlockSpec((None, POOL_HALO, D_TOK), lambda bi, i: (bi, 0, 0)),
            pl.BlockSpec((len(POOL_WINDOWS), POOL_GROUP, POOL_GROUP), lambda bi, i: (0, 0, 0)),
            pl.BlockSpec((1, D_TOK), lambda bi, i: (0, 0)),
        ],
        out_specs=pl.BlockSpec((None, tm, D_TOK), lambda bi, i: (bi, i, 0)),
        out_shape=jax.ShapeDtypeStruct((b, t, D_TOK), BF16),
        scratch_shapes=[pltpu.VMEM((tm + POOL_HALO, D_TOK), F32)],
        compiler_params=_params("parallel", "arbitrary"),
        name="pool_mix",
    )(proj, proj, prev, w_grp, scale.reshape(1, D_TOK))


def _sb_step(q, k, v, run, acc, mask):
    tk = k.shape[0]
    z = _dot_nt(q, k)
    log_fail = -(jnp.maximum(z, 0.0) + jnp.log(1.0 + jnp.exp(-jnp.abs(z))))
    if mask is not None:
        log_fail = jnp.where(mask, log_fail, 0.0)
    later = (lax.broadcasted_iota(jnp.int32, (tk, tk), 0) >
             lax.broadcasted_iota(jnp.int32, (tk, tk), 1)).astype(BF16)
    hi = log_fail.astype(BF16)
    lo = (log_fail - hi.astype(F32)).astype(BF16)
    after = _dot(hi, later) + _dot(lo, later) + run
    a = jnp.exp(z + log_fail + after)
    if mask is not None:
        a = jnp.where(mask, a, 0.0)
    acc = acc + _dot(a.astype(BF16), v)
    run = run + jnp.sum(log_fail, axis=1, keepdims=True)
    return run, acc


def _sb_prompt_body(q_ref, k_ref, v_ref, o_ref, *, tq):
    qi = pl.program_id(2)
    q = q_ref[...]
    nsub = tq // SB_BLOCK
    run = jnp.zeros((tq, 1), F32)
    acc = jnp.zeros((tq, HEAD_DIM), F32)
    row = lax.broadcasted_iota(jnp.int32, (tq, SB_BLOCK), 0)
    col = lax.broadcasted_iota(jnp.int32, (tq, SB_BLOCK), 1)
    for sub in reversed(range(nsub)):
        start = pl.multiple_of((qi * nsub + sub) * SB_BLOCK, SB_BLOCK)
        mask = col + sub * SB_BLOCK < row
        run, acc = _sb_step(q, k_ref[pl.ds(start, SB_BLOCK), :], v_ref[pl.ds(start, SB_BLOCK), :],
                            run, acc, mask)

    def body(it, carry):
        kb = qi * nsub - 1 - it
        start = pl.multiple_of(kb * SB_BLOCK, SB_BLOCK)
        return _sb_step(q, k_ref[pl.ds(start, SB_BLOCK), :], v_ref[pl.ds(start, SB_BLOCK), :],
                        carry[0], carry[1], None)

    run, acc = lax.fori_loop(0, qi * nsub, body, (run, acc))
    o_ref[...] = acc.astype(o_ref.dtype)


def _sb_attn_prompt(q, k, v):
    b, t, _ = q.shape
    tq = min(t, 256)
    return pl.pallas_call(
        functools.partial(_sb_prompt_body, tq=tq),
        grid=(b, SB_HEADS, t // tq),
        in_specs=[
            pl.BlockSpec((None, tq, HEAD_DIM), lambda bi, h, i: (bi, i, h)),
            pl.BlockSpec((None, t, HEAD_DIM), lambda bi, h, i: (bi, 0, h)),
            pl.BlockSpec((None, t, HEAD_DIM), lambda bi, h, i: (bi, 0, h)),
        ],
        out_specs=pl.BlockSpec((None, tq, HEAD_DIM), lambda bi, h, i: (bi, i, h)),
        out_shape=jax.ShapeDtypeStruct((b, t, D_TOK), BF16),
        compiler_params=_params("parallel", "parallel", "arbitrary"),
        name="sb_attn_prompt",
    )(q, k, v)


def _sb_cached_body(q_ref, kn_ref, vn_ref, kc_ref, vc_ref, o_ref, *, t, past):
    q = q_ref[...]
    run = jnp.zeros((t, 1), F32)
    acc = jnp.zeros((t, HEAD_DIM), F32)
    mask = lax.broadcasted_iota(jnp.int32, (t, t), 1) < lax.broadcasted_iota(jnp.int32, (t, t), 0)
    run, acc = _sb_step(q, kn_ref[...], vn_ref[...], run, acc, mask)

    def body(it, carry):
        start = pl.multiple_of(past - (it + 1) * SB_BLOCK, SB_BLOCK)
        kc = kc_ref[pl.ds(start, SB_BLOCK), :].astype(BF16)
        vc = vc_ref[pl.ds(start, SB_BLOCK), :].astype(BF16)
        return _sb_step(q, kc, vc, carry[0], carry[1], None)

    run, acc = lax.fori_loop(0, past // SB_BLOCK, body, (run, acc))
    o_ref[...] = acc.astype(o_ref.dtype)


def _sb_attn_cached(q, k_new, v_new, k_past, v_past):
    b, t, _ = q.shape
    past = k_past.shape[1]
    assert past % SB_BLOCK == 0, past
    new = pl.BlockSpec((None, t, HEAD_DIM), lambda bi, h: (bi, 0, h))
    old = pl.BlockSpec((None, past, HEAD_DIM), lambda bi, h: (bi, 0, h))
    return pl.pallas_call(
        functools.partial(_sb_cached_body, t=t, past=past),
        grid=(b, SB_HEADS),
        in_specs=[new, new, new, old, old],
        out_specs=new,
        out_shape=jax.ShapeDtypeStruct((b, t, D_TOK), BF16),
        compiler_params=_params("parallel", "parallel"),
        name="sb_attn_cached",
    )(q, k_new, v_new, k_past, v_past)


def _mix_out_body(x_ref, mix_ref, qm_ref, mk_ref, mv_ref, qg_ref, w_ref, o_ref, cat_ref):
    @pl.when(pl.program_id(2) == 0)
    def _():
        cat_ref[:, :D_TOK] = mix_ref[...]
        for hd in range(MEM_HEADS):
            sl = slice(hd * HEAD_DIM, (hd + 1) * HEAD_DIM)
            qh = _rms(qm_ref[:, sl], qg_ref[...]).astype(BF16)
            s = _dot_nt(qh, mk_ref[:, sl].astype(BF16)) * INV_SQRT_HD
            p = jnp.exp(s - jnp.max(s, axis=-1, keepdims=True))
            p = p / jnp.sum(p, axis=-1, keepdims=True)
            mo = _dot(p.astype(BF16), mv_ref[:, sl].astype(BF16))
            cat_ref[:, D_TOK + hd * HEAD_DIM:D_TOK + (hd + 1) * HEAD_DIM] = mo.astype(BF16)

    o_ref[...] = x_ref[...] + _dot(cat_ref[...], w_ref[...])


def _mix_out(x, mix, qm_src, qm_col, mem_k, mem_v, q_gain, w_out, tn=512):
    b, t, d = x.shape
    n_mem = mem_k.shape[1]
    tm = min(t, 512)
    return pl.pallas_call(
        _mix_out_body,
        grid=(b, t // tm, d // tn),
        in_specs=[
            pl.BlockSpec((None, tm, tn), lambda bi, i, j: (bi, i, j)),
            pl.BlockSpec((None, tm, D_TOK), lambda bi, i, j: (bi, i, 0)),
            pl.BlockSpec((None, tm, D_MEM), lambda bi, i, j: (bi, i, qm_col)),
            pl.BlockSpec((None, n_mem, D_MEM), lambda bi, i, j: (bi, 0, 0)),
            pl.BlockSpec((None, n_mem, D_MEM), lambda bi, i, j: (bi, 0, 0)),
            pl.BlockSpec((1, HEAD_DIM), lambda bi, i, j: (0, 0)),
            pl.BlockSpec((d, tn), lambda bi, i, j: (0, j)),
        ],
        out_specs=pl.BlockSpec((None, tm, tn), lambda bi, i, j: (bi, i, j)),
        out_shape=jax.ShapeDtypeStruct((b, t, d), F32),
        scratch_shapes=[pltpu.VMEM((tm, d), BF16)],
        compiler_params=_params("parallel", "parallel", "arbitrary"),
        name="mix_out",
    )(x, mix, qm_src, mem_k, mem_v, q_gain.reshape(1, HEAD_DIM), w_out)


def _ffn_body(x_ref, g_ref, wu_ref, wd_ref, o_ref, h_ref):
    @pl.when(pl.program_id(1) == 0)
    def _():
        x = x_ref[...]
        h_ref[...] = _rms(x, g_ref[...]).astype(BF16)
        o_ref[...] = x

    up = jnp.maximum(_dot(h_ref[...], wu_ref[...]), 0.0)
    o_ref[...] += _dot((up * up).astype(BF16), wd_ref[...])


def _ffn(x, g, w_up, w_down, tk=512):
    n, d = x.shape
    f = w_up.shape[1]
    tm = min(n, 512)
    return pl.pallas_call(
        _ffn_body,
        grid=(n // tm, f // tk),
        in_specs=[
            pl.BlockSpec((tm, d), lambda i, k: (i, 0)),
            pl.BlockSpec((1, d), lambda i, k: (0, 0)),
            pl.BlockSpec((d, tk), lambda i, k: (0, k)),
            pl.BlockSpec((tk, d), lambda i, k: (k, 0)),
        ],
        out_specs=pl.BlockSpec((tm, d), lambda i, k: (i, 0)),
        out_shape=jax.ShapeDtypeStruct((n, d), F32),
        scratch_shapes=[pltpu.VMEM((tm, d), BF16)],
        compiler_params=_params("parallel", "arbitrary"),
        name="ffn",
    )(x, g.reshape(1, d), w_up, w_down)


def _run_trunk(x, past, pool_prev, sb_past_k, sb_past_v, mem_k, mem_v, p):
    b, t, d = x.shape
    n = b * t
    depth = p["w_out"].shape[0]
    pool_new, sbk_new, sbv_new = [], [], []
    for i in range(depth):
        j = i // 2
        x2 = x.reshape(n, d)
        if i % 2 == 0:
            proj = _norm_matmul(x2, p["norm_mix"][i], p["w_in_pool"][j]).reshape(b, t, d)
            mix = _pool_mix(proj, pool_prev[j], p["pool_w"][j], p["pool_scale"][j], past)
            pool_new.append(proj[:, t - POOL_STATE:, :D_TOK])
            qm_src, qm_col = proj, D_TOK // D_MEM
        else:
            q, k, v, kb, vb, qm = _inproj_sb(x2, p["norm_mix"][i], p["w_in_sb"][j])
            tok = lambda a: a.reshape(b, t, D_TOK)
            if sb_past_k is None:
                mix = _sb_attn_prompt(tok(q), tok(kb), tok(vb))
            else:
                mix = _sb_attn_cached(tok(q), tok(kb), tok(vb), sb_past_k[j], sb_past_v[j])
            sbk_new.append(k.reshape(b, t, SB_HEADS, HEAD_DIM))
            sbv_new.append(v.reshape(b, t, SB_HEADS, HEAD_DIM))
            qm_src, qm_col = qm.reshape(b, t, D_MEM), 0
        x = _mix_out(x, mix, qm_src, qm_col, mem_k[i], mem_v[i], p["q_norm"][i], p["w_out"][i])
        x = _ffn(x.reshape(n, d), p["norm_ffn"][i], p["w_up"][i], p["w_down"][i]).reshape(b, t, d)
    return x, jnp.stack(pool_new), jnp.stack(sbk_new), jnp.stack(sbv_new)


def kernel(x_prompt, x_sample, mem_prompt, cache_pool, cache_sb_k, cache_sb_v, cache_mem_k, cache_mem_v,
           norm_mix, w_in_pool, w_in_sb, pool_w, pool_scale, norm_mem, w_mem_kv, q_norm, k_norm, w_out,
           norm_ffn, w_up, w_down):
    b, t, d = x_prompt.shape
    depth = w_out.shape[0]
    n_mem = mem_prompt.shape[1]
    p = dict(norm_mix=norm_mix, pool_scale=pool_scale, q_norm=q_norm, norm_ffn=norm_ffn,
             w_in_pool=w_in_pool.astype(BF16), w_in_sb=w_in_sb.astype(BF16), pool_w=pool_w.astype(BF16),
             w_out=w_out.astype(BF16), w_up=w_up.astype(BF16), w_down=w_down.astype(BF16))

    mk, mv = _mem_kv(mem_prompt.reshape(b * n_mem, d), norm_mem, w_mem_kv.astype(BF16), k_norm)
    mk = mk.reshape(depth, b, n_mem, D_MEM)
    mv = mv.reshape(depth, b, n_mem, D_MEM)
    pool_zero = jnp.zeros((cache_pool.shape[0], b, POOL_HALO, D_TOK), F32)
    y_prompt, pool_prompt, sb_k_prompt, sb_v_prompt = _run_trunk(
        x_prompt, 0, pool_zero, None, None, mk, mv, p)

    bs = x_sample.shape[0]
    past = cache_sb_k.shape[2]
    pool_prev = jnp.pad(cache_pool, ((0, 0), (0, 0), (POOL_HALO - POOL_STATE, 0), (0, 0)))
    y_sample, pool_sample, sb_k_sample, sb_v_sample = _run_trunk(
        x_sample, past, pool_prev,
        cache_sb_k.reshape(-1, bs, past, D_TOK), cache_sb_v.reshape(-1, bs, past, D_TOK),
        cache_mem_k.reshape(depth, bs, n_mem, D_MEM), cache_mem_v.reshape(depth, bs, n_mem, D_MEM), p)

    heads = lambda a: a.reshape(depth, b, n_mem, MEM_HEADS, HEAD_DIM)
    return (y_prompt, y_sample, pool_prompt, pool_sample, sb_k_prompt, sb_v_prompt,
            sb_k_sample, sb_v_sample, heads(mk), heads(mv))
```

```python
import functools
import math

import jax
import jax.numpy as jnp
from jax import lax
from jax.experimental import pallas as pl
from jax.experimental.pallas import tpu as pltpu

D_MODEL = 2048
HEAD_DIM = 128
D_TOK = 3 * D_MODEL // 4
D_MEM = D_MODEL // 4
SB_HEADS = D_TOK // HEAD_DIM
MEM_HEADS = D_MEM // HEAD_DIM
POOL_WINDOWS = (2, 4, 8, 16)
POOL_GROUP = D_TOK // len(POOL_WINDOWS)
POOL_STATE = max(POOL_WINDOWS) - 1
POOL_HALO = POOL_STATE + 1
D_FF = 4 * D_MODEL
EPS = 1e-6
INV_SQRT_HD = 1.0 / math.sqrt(HEAD_DIM)

SB_BLOCK = 128
SB_GROUP = 8
EXP_ZERO_BELOW = -104.0
RUN_FINISHED = -1e30
ROW_TILE = 512
VMEM_LIMIT = 48 * 1024 * 1024

BF16 = jnp.bfloat16
F32 = jnp.float32


def _params(*sem):
    return pltpu.CompilerParams(dimension_semantics=sem, vmem_limit_bytes=VMEM_LIMIT)


def _rms(xf, g):
    return xf * lax.rsqrt(jnp.mean(xf * xf, axis=-1, keepdims=True) + EPS) * g


def _dot(a, b):
    return jnp.dot(a, b, preferred_element_type=F32)


def _dot_nt(a, b):
    return lax.dot_general(a, b, (((1,), (1,)), ((), ())), preferred_element_type=F32)


def _resident(shape):
    return pl.BlockSpec(shape, lambda *_: (0,) * len(shape), pipeline_mode=pl.Buffered(1))


def _norm_matmul_body(x_ref, g_ref, w_ref, o_ref):
    o_ref[...] = _dot(_rms(x_ref[...], g_ref[...]).astype(BF16), w_ref[...])


def _norm_matmul(x, g, w):
    n, d = x.shape
    m = w.shape[1]
    tm = min(n, ROW_TILE)
    return pl.pallas_call(
        _norm_matmul_body,
        grid=(n // tm,),
        in_specs=[
            pl.BlockSpec((tm, d), lambda i: (i, 0)),
            pl.BlockSpec((1, d), lambda i: (0, 0)),
            _resident((d, m)),
        ],
        out_specs=pl.BlockSpec((tm, m), lambda i: (i, 0)),
        out_shape=jax.ShapeDtypeStruct((n, m), F32),
        compiler_params=_params("parallel"),
        name="norm_matmul",
    )(x, g.reshape(1, d), w)


def _inproj_sb_body(x_ref, g_ref, wq_ref, wk_ref, wv_ref, wm_ref,
                    q_ref, k_ref, v_ref, kb_ref, vb_ref, qm_ref, h_ref):
    @pl.when(pl.program_id(1) == 0)
    def _():
        h_ref[...] = _rms(x_ref[...], g_ref[...]).astype(BF16)
        qm_ref[...] = _dot(h_ref[...], wm_ref[...])

    h = h_ref[...]
    q_ref[...] = (_dot(h, wq_ref[...]) * INV_SQRT_HD).astype(BF16)
    k = _dot(h, wk_ref[...])
    k_ref[...] = k
    kb_ref[...] = k.astype(BF16)
    v = _dot(h, wv_ref[...])
    v_ref[...] = v
    vb_ref[...] = v.astype(BF16)


def _inproj_sb(x, g, w):
    n, d = x.shape
    tm = min(n, ROW_TILE)
    tn = D_MEM
    nj = D_TOK // tn
    tok = lambda dt: jax.ShapeDtypeStruct((n, D_TOK), dt)
    wspec = lambda off: pl.BlockSpec((d, tn), lambda i, j: (0, off + j))
    ospec = pl.BlockSpec((tm, tn), lambda i, j: (i, j))
    return pl.pallas_call(
        _inproj_sb_body,
        grid=(n // tm, nj),
        in_specs=[
            pl.BlockSpec((tm, d), lambda i, j: (i, 0)),
            pl.BlockSpec((1, d), lambda i, j: (0, 0)),
            wspec(0), wspec(nj), wspec(2 * nj),
            pl.BlockSpec((d, D_MEM), lambda i, j: (0, 3 * nj)),
        ],
        out_specs=[ospec, ospec, ospec, ospec, ospec,
                   pl.BlockSpec((tm, D_MEM), lambda i, j: (i, 0))],
        out_shape=[tok(BF16), tok(F32), tok(F32), tok(BF16), tok(BF16),
                   jax.ShapeDtypeStruct((n, D_MEM), F32)],
        scratch_shapes=[pltpu.VMEM((tm, d), BF16)],
        compiler_params=_params("parallel", "arbitrary"),
        name="inproj_sb",
    )(x, g.reshape(1, d), w, w, w, w)


def _mem_kv_body(x_ref, g_ref, w_ref, kg_ref, k_ref, v_ref):
    h = _rms(x_ref[...], g_ref[...]).astype(BF16)
    kv = _dot(h, w_ref[...])
    for hd in range(MEM_HEADS):
        sl = slice(hd * HEAD_DIM, (hd + 1) * HEAD_DIM)
        k_ref[:, sl] = _rms(kv[:, sl], kg_ref[...])
    v_ref[...] = kv[:, D_MEM:]


def _mem_kv(mem, g_mem, w_kv, k_gain):
    r, d = mem.shape
    depth = w_kv.shape[0]
    tm = min(r, 256)
    out = jax.ShapeDtypeStruct((depth, r, D_MEM), F32)
    ospec = pl.BlockSpec((None, tm, D_MEM), lambda l, i: (l, i, 0))
    return pl.pallas_call(
        _mem_kv_body,
        grid=(depth, r // tm),
        in_specs=[
            pl.BlockSpec((tm, d), lambda l, i: (i, 0)),
            pl.BlockSpec((None, 1, d), lambda l, i: (l, 0, 0)),
            pl.BlockSpec((None, d, 2 * D_MEM), lambda l, i: (l, 0, 0)),
            pl.BlockSpec((None, 1, HEAD_DIM), lambda l, i: (l, 0, 0)),
        ],
        out_specs=[ospec, ospec],
        out_shape=[out, out],
        compiler_params=_params("parallel", "parallel"),
        name="mem_kv",
    )(mem, g_mem.reshape(depth, 1, d), w_kv, k_gain.reshape(depth, 1, HEAD_DIM))


def _pool_body(u_ref, halo_ref, prev_ref, w_ref, s_ref, o_ref, ext_ref, *, tm, past):
    i = pl.program_id(1)

    @pl.when(i == 0)
    def _():
        ext_ref[0:POOL_HALO, :] = prev_ref[...]

    @pl.when(i > 0)
    def _():
        ext_ref[0:POOL_HALO, :] = halo_ref[...]

    ext_ref[POOL_HALO:, :] = u_ref[...]
    pos = (past + i * tm + lax.broadcasted_iota(jnp.int32, (tm, 1), 0)).astype(F32)
    for g, w in enumerate(POOL_WINDOWS):
        sl = slice(g * POOL_GROUP, (g + 1) * POOL_GROUP)
        cur = ext_ref[POOL_HALO:, sl]
        win = cur
        for back in range(1, w):
            win = win + ext_ref[POOL_HALO - back:POOL_HALO - back + tm, sl]
        cnt = jnp.minimum(pos + 1.0, float(w))
        dlt = (win / cnt - cur).astype(BF16)
        o_ref[:, sl] = (_dot(dlt, w_ref[g]) * s_ref[:, sl]).astype(BF16)


def _pool_mix(proj, prev, w_grp, scale, past):
    b, t, _ = proj.shape
    tm = min(t, 512)
    hb = tm // POOL_HALO
    return pl.pallas_call(
        functools.partial(_pool_body, tm=tm, past=past),
        grid=(b, t // tm),
        in_specs=[
            pl.BlockSpec((None, tm, D_TOK), lambda bi, i: (bi, i, 0)),
            pl.BlockSpec((None, POOL_HALO, D_TOK), lambda bi, i: (bi, jnp.maximum(i * hb - 1, 0), 0)),
            pl.BlockSpec((None, POOL_HALO, D_TOK), lambda bi, i: (bi, 0, 0)),
            pl.BlockSpec((len(POOL_WINDOWS), POOL_GROUP, POOL_GROUP), lambda bi, i: (0, 0, 0)),
            pl.BlockSpec((1, D_TOK), lambda bi, i: (0, 0)),
        ],
        out_specs=pl.BlockSpec((None, tm, D_TOK), lambda bi, i: (bi, i, 0)),
        out_shape=jax.ShapeDtypeStruct((b, t, D_TOK), BF16),
        scratch_shapes=[pltpu.VMEM((tm + POOL_HALO, D_TOK), F32)],
        compiler_params=_params("parallel", "arbitrary"),
        name="pool_mix",
    )(proj, proj, prev, w_grp, scale.reshape(1, D_TOK))


def _scan_matrix():
    shape = (2 * SB_BLOCK, SB_BLOCK + HEAD_DIM)
    r = lax.broadcasted_iota(jnp.int32, shape, 0) % SB_BLOCK
    c = lax.broadcasted_iota(jnp.int32, shape, 1)
    return jnp.where((c >= SB_BLOCK) | (r > c), 1.0, 0.0).astype(BF16)


def _sb_steps(qs, ks, vs, runs, mask, scan):
    tq = qs[0].shape[0]
    zs = [_dot_nt(q, k) for q, k in zip(qs, ks)]
    fails, parts = [], []
    for z in zs:
        log_fail = -(jnp.maximum(z, 0.0) + jnp.log(1.0 + jnp.exp(-jnp.abs(z))))
        if mask is not None:
            log_fail = jnp.where(mask, log_fail, 0.0)
        hi = log_fail.astype(BF16)
        lo = (log_fail - hi.astype(F32)).astype(BF16)
        fails.append(log_fail)
        parts.append(jnp.concatenate([hi, lo], axis=1))
    sums = _dot(jnp.concatenate(parts, axis=0), scan)
    weights, new_runs = [], []
    for g, (z, log_fail, run) in enumerate(zip(zs, fails, runs)):
        s = sums[g * tq:(g + 1) * tq]
        a = jnp.exp(z + log_fail + s[:, :SB_BLOCK] + run)
        if mask is not None:
            a = jnp.where(mask, a, 0.0)
        weights.append(a.astype(BF16))
        new_runs.append(run + s[:, SB_BLOCK:])
    return new_runs, [_dot(a, v) for a, v in zip(weights, vs)]


def _largest(values):
    out = values[0]
    for v in values[1:]:
        out = jnp.maximum(out, v)
    return jnp.max(out)


def _sb_prompt_body(q_ref, k_ref, v_ref, o_ref, run_ref, acc_ref, scan_ref, *, t, group):
    scan_ref[...] = _scan_matrix()
    below_diag = (lax.broadcasted_iota(jnp.int32, (SB_BLOCK, SB_BLOCK), 1) <
                  lax.broadcasted_iota(jnp.int32, (SB_BLOCK, SB_BLOCK), 0))
    zero = jnp.zeros((SB_BLOCK, HEAD_DIM), F32)
    chain = [slice(g * SB_BLOCK, (g + 1) * SB_BLOCK) for g in range(group)]

    def block(i):
        return pl.ds(pl.multiple_of(i * SB_BLOCK, SB_BLOCK), SB_BLOCK)

    def group_scan(s, _):
        first = s * group
        tiles = [block(first + g) for g in range(group)]
        runs, pvs = _sb_steps([q_ref[r, :] for r in tiles], [k_ref[r, :] for r in tiles],
                              [v_ref[r, :] for r in tiles], [zero] * group, below_diag, scan_ref[...])
        for sl, run, pv in zip(chain, runs, pvs):
            run_ref[sl, :] = run
            acc_ref[sl, :] = pv

        def more(c):
            it, largest_run = c
            return jnp.logical_and(it < first + group, largest_run > EXP_ZERO_BELOW)

        def step(c):
            it, _ = c
            keys = [block(jnp.maximum(first + g - it, 0)) for g in range(group)]
            runs = [jnp.where(first + g - it >= 0, run_ref[chain[g], :], RUN_FINISHED) for g in range(group)]
            runs, pvs = _sb_steps([q_ref[r, :] for r in tiles], [k_ref[r, :] for r in keys],
                                  [v_ref[r, :] for r in keys], runs, None, scan_ref[...])
            for sl, run, pv in zip(chain, runs, pvs):
                run_ref[sl, :] = run
                acc_ref[sl, :] += pv
            return it + 1, _largest(runs)

        lax.while_loop(more, step, (jnp.int32(1), _largest(runs)))
        out_rows = pl.ds(pl.multiple_of(first * SB_BLOCK, SB_BLOCK), group * SB_BLOCK)
        o_ref[out_rows, :] = acc_ref[...].astype(o_ref.dtype)
        return 0

    lax.fori_loop(0, t // (group * SB_BLOCK), group_scan, 0)


def _sb_attn_prompt(q, k, v):
    b, t, _ = q.shape
    group = min(SB_GROUP, t // SB_BLOCK)
    assert t % (group * SB_BLOCK) == 0, t
    spec = pl.BlockSpec((None, t, HEAD_DIM), lambda bi, h: (bi, 0, h))
    return pl.pallas_call(
        functools.partial(_sb_prompt_body, t=t, group=group),
        grid=(b, SB_HEADS),
        in_specs=[spec, spec, spec],
        out_specs=spec,
        out_shape=jax.ShapeDtypeStruct((b, t, D_TOK), BF16),
        scratch_shapes=[pltpu.VMEM((group * SB_BLOCK, HEAD_DIM), F32),
                        pltpu.VMEM((group * SB_BLOCK, HEAD_DIM), F32),
                        pltpu.VMEM((2 * SB_BLOCK, SB_BLOCK + HEAD_DIM), BF16)],
        compiler_params=_params("parallel", "parallel"),
        name="sb_attn_prompt",
    )(q, k, v)


def _sb_cached_body(q_ref, kn_ref, vn_ref, kc_ref, vc_ref, o_ref, run_ref, acc_ref, scan_ref, *, t, past):
    scan_ref[...] = _scan_matrix()
    earlier = (lax.broadcasted_iota(jnp.int32, (t, SB_BLOCK), 1) <
               lax.broadcasted_iota(jnp.int32, (t, SB_BLOCK), 0))
    zero = jnp.zeros((t, HEAD_DIM), F32)
    pad = jnp.zeros((SB_BLOCK - t, HEAD_DIM), BF16)
    head = [slice(h * HEAD_DIM, (h + 1) * HEAD_DIM) for h in range(SB_HEADS)]
    chain = [slice(h * t, (h + 1) * t) for h in range(SB_HEADS)]

    runs, pvs = _sb_steps([q_ref[:, c] for c in head],
                          [jnp.concatenate([kn_ref[:, c], pad], axis=0) for c in head],
                          [jnp.concatenate([vn_ref[:, c], pad], axis=0) for c in head],
                          [zero] * SB_HEADS, earlier, scan_ref[...])
    for sl, run, pv in zip(chain, runs, pvs):
        run_ref[sl, :] = run
        acc_ref[sl, :] = pv

    def more(c):
        it, largest_run = c
        return jnp.logical_and(it < past // SB_BLOCK, largest_run > EXP_ZERO_BELOW)

    def step(c):
        it, _ = c
        keys = pl.ds(pl.multiple_of(past - (it + 1) * SB_BLOCK, SB_BLOCK), SB_BLOCK)
        runs, pvs = _sb_steps([q_ref[:, c] for c in head], [kc_ref[keys, c].astype(BF16) for c in head],
                              [vc_ref[keys, c].astype(BF16) for c in head], [run_ref[sl, :] for sl in chain],
                              None, scan_ref[...])
        for sl, run, pv in zip(chain, runs, pvs):
            run_ref[sl, :] = run
            acc_ref[sl, :] += pv
        return it + 1, _largest(runs)

    lax.while_loop(more, step, (jnp.int32(0), _largest(runs)))
    for c, sl in zip(head, chain):
        o_ref[:, c] = acc_ref[sl, :].astype(o_ref.dtype)


def _sb_attn_cached(q, k_new, v_new, k_past, v_past):
    b, t, _ = q.shape
    past = k_past.shape[1]
    assert past % SB_BLOCK == 0 and t <= SB_BLOCK and t % 16 == 0, (past, t)
    new = pl.BlockSpec((None, t, D_TOK), lambda bi: (bi, 0, 0))
    old = pl.BlockSpec((None, past, D_TOK), lambda bi: (bi, 0, 0))
    return pl.pallas_call(
        functools.partial(_sb_cached_body, t=t, past=past),
        grid=(b,),
        in_specs=[new, new, new, old, old],
        out_specs=new,
        out_shape=jax.ShapeDtypeStruct((b, t, D_TOK), BF16),
        scratch_shapes=[pltpu.VMEM((SB_HEADS * t, HEAD_DIM), F32),
                        pltpu.VMEM((SB_HEADS * t, HEAD_DIM), F32),
                        pltpu.VMEM((2 * SB_BLOCK, SB_BLOCK + HEAD_DIM), BF16)],
        compiler_params=_params("parallel"),
        name="sb_attn_cached",
    )(q, k_new, v_new, k_past, v_past)


def _mem_attn_body(qm_ref, mk_ref, mv_ref, qg_ref, o_ref):
    for hd in range(MEM_HEADS):
        sl = slice(hd * HEAD_DIM, (hd + 1) * HEAD_DIM)
        qh = _rms(qm_ref[:, sl], qg_ref[...]).astype(BF16)
        s = _dot_nt(qh, mk_ref[:, sl].astype(BF16)) * INV_SQRT_HD
        p = jnp.exp(s - jnp.max(s, axis=-1, keepdims=True))
        p = p / jnp.sum(p, axis=-1, keepdims=True)
        o_ref[:, sl] = _dot(p.astype(BF16), mv_ref[:, sl].astype(BF16)).astype(BF16)


def _mem_attn(qm_src, qm_col, mem_k, mem_v, q_gain):
    b, t, _ = qm_src.shape
    n_mem = mem_k.shape[1]
    tm = min(t, ROW_TILE)
    mem = pl.BlockSpec((None, n_mem, D_MEM), lambda bi, i: (bi, 0, 0))
    return pl.pallas_call(
        _mem_attn_body,
        grid=(b, t // tm),
        in_specs=[
            pl.BlockSpec((None, tm, D_MEM), lambda bi, i: (bi, i, qm_col)),
            mem, mem,
            pl.BlockSpec((1, HEAD_DIM), lambda bi, i: (0, 0)),
        ],
        out_specs=pl.BlockSpec((None, tm, D_MEM), lambda bi, i: (bi, i, 0)),
        out_shape=jax.ShapeDtypeStruct((b, t, D_MEM), BF16),
        compiler_params=_params("parallel", "parallel"),
        name="mem_attn",
    )(qm_src, mem_k, mem_v, q_gain.reshape(1, HEAD_DIM))


def _out_proj_body(x_ref, mix_ref, mo_ref, w_ref, o_ref):
    o_ref[...] = (x_ref[...] + _dot(mix_ref[...], w_ref[:D_TOK, :]) + _dot(mo_ref[...], w_ref[D_TOK:, :]))


def _out_proj(x, mix, mo, w_out):
    n, d = x.shape
    tm = min(n, ROW_TILE)
    row = lambda c: pl.BlockSpec((tm, c), lambda i: (i, 0))
    return pl.pallas_call(
        _out_proj_body,
        grid=(n // tm,),
        in_specs=[row(d), row(D_TOK), row(D_MEM), _resident((d, d))],
        out_specs=row(d),
        out_shape=jax.ShapeDtypeStruct((n, d), F32),
        compiler_params=_params("parallel"),
        name="out_proj",
    )(x, mix, mo, w_out)


def _ffn_body(x_ref, g_ref, wu_ref, wd_ref, o_ref, h_ref):
    @pl.when(pl.program_id(1) == 0)
    def _():
        x = x_ref[...]
        h_ref[...] = _rms(x, g_ref[...]).astype(BF16)
        o_ref[...] = x

    up = jnp.maximum(_dot(h_ref[...], wu_ref[...]), 0.0)
    o_ref[...] += _dot((up * up).astype(BF16), wd_ref[...])


def _ffn(x, g, w_up, w_down, tk=1024):
    n, d = x.shape
    f = w_up.shape[1]
    tm = min(n, ROW_TILE)
    return pl.pallas_call(
        _ffn_body,
        grid=(n // tm, f // tk),
        in_specs=[
            pl.BlockSpec((tm, d), lambda i, k: (i, 0)),
            pl.BlockSpec((1, d), lambda i, k: (0, 0)),
            pl.BlockSpec((d, tk), lambda i, k: (0, k)),
            pl.BlockSpec((tk, d), lambda i, k: (k, 0)),
        ],
        out_specs=pl.BlockSpec((tm, d), lambda i, k: (i, 0)),
        out_shape=jax.ShapeDtypeStruct((n, d), F32),
        scratch_shapes=[pltpu.VMEM((tm, d), BF16)],
        compiler_params=_params("parallel", "arbitrary"),
        name="ffn",
    )(x, g.reshape(1, d), w_up, w_down)


def _run_trunk(x, past, pool_prev, sb_past_k, sb_past_v, mem_k, mem_v, p):
    b, t, d = x.shape
    n = b * t
    depth = p["w_out"].shape[0]
    pool_new, sbk_new, sbv_new = [], [], []
    for i in range(depth):
        j = i // 2
        x2 = x.reshape(n, d)
        if i % 2 == 0:
            proj = _norm_matmul(x2, p["norm_mix"][i], p["w_in_pool"][j]).reshape(b, t, d)
            mix = _pool_mix(proj, pool_prev[j], p["pool_w"][j], p["pool_scale"][j], past)
            pool_new.append(proj[:, t - POOL_STATE:, :D_TOK])
            qm_src, qm_col = proj, D_TOK // D_MEM
        else:
            q, k, v, kb, vb, qm = _inproj_sb(x2, p["norm_mix"][i], p["w_in_sb"][j])
            tok = lambda a: a.reshape(b, t, D_TOK)
            if sb_past_k is None:
                mix = _sb_attn_prompt(tok(q), tok(kb), tok(vb))
            else:
                mix = _sb_attn_cached(tok(q), tok(kb), tok(vb), sb_past_k[j], sb_past_v[j])
            sbk_new.append(k.reshape(b, t, SB_HEADS, HEAD_DIM))
            sbv_new.append(v.reshape(b, t, SB_HEADS, HEAD_DIM))
            qm_src, qm_col = qm.reshape(b, t, D_MEM), 0
        mo = _mem_attn(qm_src, qm_col, mem_k[i], mem_v[i], p["q_norm"][i])
        x2 = _out_proj(x2, mix.reshape(n, D_TOK), mo.reshape(n, D_MEM), p["w_out"][i])
        x = _ffn(x2, p["norm_ffn"][i], p["w_up"][i], p["w_down"][i]).reshape(b, t, d)
    return x, jnp.stack(pool_new), jnp.stack(sbk_new), jnp.stack(sbv_new)


def kernel(x_prompt, x_sample, mem_prompt, cache_pool, cache_sb_k, cache_sb_v, cache_mem_k, cache_mem_v,
           norm_mix, w_in_pool, w_in_sb, pool_w, pool_scale, norm_mem, w_mem_kv, q_norm, k_norm, w_out,
           norm_ffn, w_up, w_down):
    b, t, d = x_prompt.shape
    depth = w_out.shape[0]
    n_mem = mem_prompt.shape[1]
    p = dict(norm_mix=norm_mix, pool_scale=pool_scale, q_norm=q_norm, norm_ffn=norm_ffn,
             w_in_pool=w_in_pool.astype(BF16), w_in_sb=w_in_sb.astype(BF16), pool_w=pool_w.astype(BF16),
             w_out=w_out.astype(BF16), w_up=w_up.astype(BF16), w_down=w_down.astype(BF16))

    mk, mv = _mem_kv(mem_prompt.reshape(b * n_mem, d), norm_mem, w_mem_kv.astype(BF16), k_norm)
    mk = mk.reshape(depth, b, n_mem, D_MEM)
    mv = mv.reshape(depth, b, n_mem, D_MEM)
    pool_zero = jnp.zeros((cache_pool.shape[0], b, POOL_HALO, D_TOK), F32)
    y_prompt, pool_prompt, sb_k_prompt, sb_v_prompt = _run_trunk(
        x_prompt, 0, pool_zero, None, None, mk, mv, p)

    bs = x_sample.shape[0]
    past = cache_sb_k.shape[2]
    pool_prev = jnp.pad(cache_pool, ((0, 0), (0, 0), (POOL_HALO - POOL_STATE, 0), (0, 0)))
    y_sample, pool_sample, sb_k_sample, sb_v_sample = _run_trunk(
        x_sample, past, pool_prev,
        cache_sb_k.reshape(-1, bs, past, D_TOK), cache_sb_v.reshape(-1, bs, past, D_TOK),
        cache_mem_k.reshape(depth, bs, n_mem, D_MEM), cache_mem_v.reshape(depth, bs, n_mem, D_MEM), p)

    heads = lambda a: a.reshape(depth, b, n_mem, MEM_HEADS, HEAD_DIM)
    return (y_prompt, y_sample, pool_prompt, pool_sample, sb_k_prompt, sb_v_prompt,
            sb_k_sample, sb_v_sample, heads(mk), heads(mv))
```

```python
import functools
import math

import jax
import jax.numpy as jnp
from jax import lax
from jax.experimental import pallas as pl
from jax.experimental.pallas import tpu as pltpu

D_MODEL = 2048
HEAD_DIM = 128
D_TOK = 3 * D_MODEL // 4
D_MEM = D_MODEL // 4
SB_HEADS = D_TOK // HEAD_DIM
MEM_HEADS = D_MEM // HEAD_DIM
POOL_WINDOWS = (2, 4, 8, 16)
POOL_GROUP = D_TOK // len(POOL_WINDOWS)
POOL_STATE = max(POOL_WINDOWS) - 1
POOL_HALO = POOL_STATE + 1
D_FF = 4 * D_MODEL
EPS = 1e-6
INV_SQRT_HD = 1.0 / math.sqrt(HEAD_DIM)

SB_BLOCK = 128
SB_GROUP = 8
EXP_ZERO_BELOW = -104.0
RUN_FINISHED = -1e30
ROW_TILE = 512
VMEM_LIMIT = 48 * 1024 * 1024

BF16 = jnp.bfloat16
F32 = jnp.float32


def _params(*sem):
    return pltpu.CompilerParams(dimension_semantics=sem, vmem_limit_bytes=VMEM_LIMIT)


def _rms(xf, g):
    return xf * lax.rsqrt(jnp.mean(xf * xf, axis=-1, keepdims=True) + EPS) * g


def _dot(a, b):
    return jnp.dot(a, b, preferred_element_type=F32)


def _dot_nt(a, b):
    return lax.dot_general(a, b, (((1,), (1,)), ((), ())), preferred_element_type=F32)


def _resident(shape, layer):
    return pl.BlockSpec((None,) + shape, lambda *_: (layer,) + (0,) * len(shape), pipeline_mode=pl.Buffered(1))


def _norm_matmul_body(x_ref, g_ref, w_ref, o_ref):
    o_ref[...] = _dot(_rms(x_ref[...], g_ref[...]).astype(BF16), w_ref[...])


def _norm_matmul(x, g, w, layer):
    n, d = x.shape
    m = w.shape[2]
    tm = min(n, ROW_TILE)
    return pl.pallas_call(
        _norm_matmul_body,
        grid=(n // tm,),
        in_specs=[
            pl.BlockSpec((tm, d), lambda i: (i, 0)),
            pl.BlockSpec((1, d), lambda i: (0, 0)),
            _resident((d, m), layer),
        ],
        out_specs=pl.BlockSpec((tm, m), lambda i: (i, 0)),
        out_shape=jax.ShapeDtypeStruct((n, m), F32),
        compiler_params=_params("parallel"),
        name="norm_matmul",
    )(x, g.reshape(1, d), w)


def _inproj_sb_body(x_ref, g_ref, wq_ref, wk_ref, wv_ref, wm_ref,
                    q_ref, k_ref, v_ref, kb_ref, vb_ref, qm_ref, h_ref):
    @pl.when(pl.program_id(1) == 0)
    def _():
        h_ref[...] = _rms(x_ref[...], g_ref[...]).astype(BF16)
        qm_ref[...] = _dot(h_ref[...], wm_ref[...])

    def put(refs, val):
        bb, heads, tt, _ = refs[0].shape
        for bi in range(bb):
            for hd in range(heads):
                piece = val[bi * tt:(bi + 1) * tt, hd * HEAD_DIM:(hd + 1) * HEAD_DIM]
                for ref in refs:
                    ref[bi, hd] = piece.astype(ref.dtype)

    h = h_ref[...]
    put([q_ref], _dot(h, wq_ref[...]) * INV_SQRT_HD)
    put([k_ref, kb_ref], _dot(h, wk_ref[...]))
    put([v_ref, vb_ref], _dot(h, wv_ref[...]))


def _inproj_sb(x, g, w, layer, b, t):
    n, d = x.shape
    tm = min(n, ROW_TILE)
    tn = D_MEM
    nj = D_TOK // tn
    hj = tn // HEAD_DIM
    if t >= tm:
        per_seq = t // tm
        oblock, omap = (1, hj, tm, HEAD_DIM), lambda i, j: (i // per_seq, j, i % per_seq, 0)
    else:
        oblock, omap = (tm // t, hj, t, HEAD_DIM), lambda i, j: (i, j, 0, 0)
    tok = lambda dt: jax.ShapeDtypeStruct((b, SB_HEADS, t, HEAD_DIM), dt)
    wspec = lambda off: pl.BlockSpec((None, d, tn), lambda i, j: (layer, 0, off + j))
    ospec = pl.BlockSpec(oblock, omap)
    return pl.pallas_call(
        _inproj_sb_body,
        grid=(n // tm, nj),
        in_specs=[
            pl.BlockSpec((tm, d), lambda i, j: (i, 0)),
            pl.BlockSpec((1, d), lambda i, j: (0, 0)),
            wspec(0), wspec(nj), wspec(2 * nj),
            pl.BlockSpec((None, d, D_MEM), lambda i, j: (layer, 0, 3 * nj)),
        ],
        out_specs=[ospec, ospec, ospec, ospec, ospec,
                   pl.BlockSpec((tm, D_MEM), lambda i, j: (i, 0))],
        out_shape=[tok(BF16), tok(F32), tok(F32), tok(BF16), tok(BF16),
                   jax.ShapeDtypeStruct((n, D_MEM), F32)],
        scratch_shapes=[pltpu.VMEM((tm, d), BF16)],
        compiler_params=_params("parallel", "arbitrary"),
        name="inproj_sb",
    )(x, g.reshape(1, d), w, w, w, w)


def _mem_kv_body(x_ref, g_ref, w_ref, kg_ref, k_ref, v_ref):
    h = _rms(x_ref[...], g_ref[...]).astype(BF16)
    kv = _dot(h, w_ref[...])
    for hd in range(MEM_HEADS):
        sl = slice(hd * HEAD_DIM, (hd + 1) * HEAD_DIM)
        k_ref[:, sl] = _rms(kv[:, sl], kg_ref[...])
    v_ref[...] = kv[:, D_MEM:]


def _mem_kv(mem, g_mem, w_kv, k_gain):
    r, d = mem.shape
    depth = w_kv.shape[0]
    tm = min(r, 256)
    out = jax.ShapeDtypeStruct((depth, r, D_MEM), F32)
    ospec = pl.BlockSpec((None, tm, D_MEM), lambda l, i: (l, i, 0))
    return pl.pallas_call(
        _mem_kv_body,
        grid=(depth, r // tm),
        in_specs=[
            pl.BlockSpec((tm, d), lambda l, i: (i, 0)),
            pl.BlockSpec((None, 1, d), lambda l, i: (l, 0, 0)),
            pl.BlockSpec((None, d, 2 * D_MEM), lambda l, i: (l, 0, 0)),
            pl.BlockSpec((None, 1, HEAD_DIM), lambda l, i: (l, 0, 0)),
        ],
        out_specs=[ospec, ospec],
        out_shape=[out, out],
        compiler_params=_params("parallel", "parallel"),
        name="mem_kv",
    )(mem, g_mem.reshape(depth, 1, d), w_kv, k_gain.reshape(depth, 1, HEAD_DIM))


def _pool_body(u_ref, halo_ref, prev_ref, w_ref, s_ref, o_ref, ext_ref, *, tm, past):
    i = pl.program_id(1)

    @pl.when(i == 0)
    def _():
        ext_ref[0:POOL_HALO, :] = prev_ref[...]

    @pl.when(i > 0)
    def _():
        ext_ref[0:POOL_HALO, :] = halo_ref[...]

    ext_ref[POOL_HALO:, :] = u_ref[...]
    pos = (past + i * tm + lax.broadcasted_iota(jnp.int32, (tm, 1), 0)).astype(F32)
    for g, w in enumerate(POOL_WINDOWS):
        sl = slice(g * POOL_GROUP, (g + 1) * POOL_GROUP)
        cur = ext_ref[POOL_HALO:, sl]
        win = cur
        for back in range(1, w):
            win = win + ext_ref[POOL_HALO - back:POOL_HALO - back + tm, sl]
        cnt = jnp.minimum(pos + 1.0, float(w))
        dlt = (win / cnt - cur).astype(BF16)
        o_ref[:, sl] = (_dot(dlt, w_ref[g]) * s_ref[:, sl]).astype(BF16)


def _pool_mix(proj, prev, w_grp, scale, past):
    b, t, _ = proj.shape
    tm = min(t, 512)
    hb = tm // POOL_HALO
    return pl.pallas_call(
        functools.partial(_pool_body, tm=tm, past=past),
        grid=(b, t // tm),
        in_specs=[
            pl.BlockSpec((None, tm, D_TOK), lambda bi, i: (bi, i, 0)),
            pl.BlockSpec((None, POOL_HALO, D_TOK), lambda bi, i: (bi, jnp.maximum(i * hb - 1, 0), 0)),
            pl.BlockSpec((None, POOL_HALO, D_TOK), lambda bi, i: (bi, 0, 0)),
            pl.BlockSpec((len(POOL_WINDOWS), POOL_GROUP, POOL_GROUP), lambda bi, i: (0, 0, 0)),
            pl.BlockSpec((1, D_TOK), lambda bi, i: (0, 0)),
        ],
        out_specs=pl.BlockSpec((None, tm, D_TOK), lambda bi, i: (bi, i, 0)),
        out_shape=jax.ShapeDtypeStruct((b, t, D_TOK), BF16),
        scratch_shapes=[pltpu.VMEM((tm + POOL_HALO, D_TOK), F32)],
        compiler_params=_params("parallel", "arbitrary"),
        name="pool_mix",
    )(proj, proj, prev, w_grp, scale.reshape(1, D_TOK))


def _scan_matrix():
    shape = (2 * SB_BLOCK, SB_BLOCK + HEAD_DIM)
    r = lax.broadcasted_iota(jnp.int32, shape, 0) % SB_BLOCK
    c = lax.broadcasted_iota(jnp.int32, shape, 1)
    return jnp.where((c >= SB_BLOCK) | (r > c), 1.0, 0.0).astype(BF16)


def _sb_steps(qs, ks, vs, runs, mask, scan):
    tq = qs[0].shape[0]
    zs = [_dot_nt(q, k) for q, k in zip(qs, ks)]
    fails, parts = [], []
    for z in zs:
        log_fail = -(jnp.maximum(z, 0.0) + jnp.log(1.0 + jnp.exp(-jnp.abs(z))))
        if mask is not None:
            log_fail = jnp.where(mask, log_fail, 0.0)
        hi = log_fail.astype(BF16)
        lo = (log_fail - hi.astype(F32)).astype(BF16)
        fails.append(log_fail)
        parts.append(jnp.concatenate([hi, lo], axis=1))
    sums = _dot(jnp.concatenate(parts, axis=0), scan)
    weights, new_runs = [], []
    for g, (z, log_fail, run) in enumerate(zip(zs, fails, runs)):
        s = sums[g * tq:(g + 1) * tq]
        a = jnp.exp(z + log_fail + s[:, :SB_BLOCK] + run)
        if mask is not None:
            a = jnp.where(mask, a, 0.0)
        weights.append(a.astype(BF16))
        new_runs.append(run + s[:, SB_BLOCK:])
    return new_runs, [_dot(a, v) for a, v in zip(weights, vs)]


def _largest(values):
    out = values[0]
    for v in values[1:]:
        out = jnp.maximum(out, v)
    return jnp.max(out)


def _sb_prompt_body(q_ref, k_ref, v_ref, o_ref, run_ref, acc_ref, scan_ref, *, t, group):
    scan_ref[...] = _scan_matrix()
    below_diag = (lax.broadcasted_iota(jnp.int32, (SB_BLOCK, SB_BLOCK), 1) <
                  lax.broadcasted_iota(jnp.int32, (SB_BLOCK, SB_BLOCK), 0))
    zero = jnp.zeros((SB_BLOCK, HEAD_DIM), F32)
    chain = [slice(g * SB_BLOCK, (g + 1) * SB_BLOCK) for g in range(group)]

    def block(i):
        return pl.ds(pl.multiple_of(i * SB_BLOCK, SB_BLOCK), SB_BLOCK)

    def group_scan(s, _):
        first = s * group
        tiles = [block(first + g) for g in range(group)]
        runs, pvs = _sb_steps([q_ref[r, :] for r in tiles], [k_ref[r, :] for r in tiles],
                              [v_ref[r, :] for r in tiles], [zero] * group, below_diag, scan_ref[...])
        for sl, run, pv in zip(chain, runs, pvs):
            run_ref[sl, :] = run
            acc_ref[sl, :] = pv

        def more(c):
            it, largest_run = c
            return jnp.logical_and(it < first + group, largest_run > EXP_ZERO_BELOW)

        def step(c):
            it, _ = c
            keys = [block(jnp.maximum(first + g - it, 0)) for g in range(group)]
            runs = [jnp.where(first + g - it >= 0, run_ref[chain[g], :], RUN_FINISHED) for g in range(group)]
            runs, pvs = _sb_steps([q_ref[r, :] for r in tiles], [k_ref[r, :] for r in keys],
                                  [v_ref[r, :] for r in keys], runs, None, scan_ref[...])
            for sl, run, pv in zip(chain, runs, pvs):
                run_ref[sl, :] = run
                acc_ref[sl, :] += pv
            return it + 1, _largest(runs)

        lax.while_loop(more, step, (jnp.int32(1), _largest(runs)))
        out_rows = pl.ds(pl.multiple_of(first * SB_BLOCK, SB_BLOCK), group * SB_BLOCK)
        o_ref[out_rows, :] = acc_ref[...].astype(o_ref.dtype)
        return 0

    lax.fori_loop(0, t // (group * SB_BLOCK), group_scan, 0)


def _sb_attn_prompt(q, k, v):
    b, _, t, _ = q.shape
    group = min(SB_GROUP, t // SB_BLOCK)
    assert t % (group * SB_BLOCK) == 0, t
    spec = pl.BlockSpec((None, None, t, HEAD_DIM), lambda bi, h: (bi, h, 0, 0))
    return pl.pallas_call(
        functools.partial(_sb_prompt_body, t=t, group=group),
        grid=(b, SB_HEADS),
        in_specs=[spec, spec, spec],
        out_specs=pl.BlockSpec((None, t, HEAD_DIM), lambda bi, h: (bi, 0, h)),
        out_shape=jax.ShapeDtypeStruct((b, t, D_TOK), BF16),
        scratch_shapes=[pltpu.VMEM((group * SB_BLOCK, HEAD_DIM), F32),
                        pltpu.VMEM((group * SB_BLOCK, HEAD_DIM), F32),
                        pltpu.VMEM((2 * SB_BLOCK, SB_BLOCK + HEAD_DIM), BF16)],
        compiler_params=_params("parallel", "parallel"),
        name="sb_attn_prompt",
    )(q, k, v)


def _sb_cached_body(q_ref, kn_ref, vn_ref, kc_ref, vc_ref, o_ref, run_ref, acc_ref, scan_ref, *, t, past):
    scan_ref[...] = _scan_matrix()
    earlier = (lax.broadcasted_iota(jnp.int32, (t, SB_BLOCK), 1) <
               lax.broadcasted_iota(jnp.int32, (t, SB_BLOCK), 0))
    zero = jnp.zeros((t, HEAD_DIM), F32)
    pad = jnp.zeros((SB_BLOCK - t, HEAD_DIM), BF16)
    heads = range(SB_HEADS)
    chain = [slice(h * t, (h + 1) * t) for h in heads]

    runs, pvs = _sb_steps([q_ref[h] for h in heads],
                          [jnp.concatenate([kn_ref[h], pad], axis=0) for h in heads],
                          [jnp.concatenate([vn_ref[h], pad], axis=0) for h in heads],
                          [zero] * SB_HEADS, earlier, scan_ref[...])
    for sl, run, pv in zip(chain, runs, pvs):
        run_ref[sl, :] = run
        acc_ref[sl, :] = pv

    def more(c):
        it, largest_run = c
        return jnp.logical_and(it < past // SB_BLOCK, largest_run > EXP_ZERO_BELOW)

    def step(c):
        it, _ = c
        keys = pl.ds(pl.multiple_of(past - (it + 1) * SB_BLOCK, SB_BLOCK), SB_BLOCK)
        runs, pvs = _sb_steps([q_ref[h] for h in heads], [kc_ref[h, keys, :].astype(BF16) for h in heads],
                              [vc_ref[h, keys, :].astype(BF16) for h in heads], [run_ref[sl, :] for sl in chain],
                              None, scan_ref[...])
        for sl, run, pv in zip(chain, runs, pvs):
            run_ref[sl, :] = run
            acc_ref[sl, :] += pv
        return it + 1, _largest(runs)

    lax.while_loop(more, step, (jnp.int32(0), _largest(runs)))
    for h, sl in zip(heads, chain):
        o_ref[:, h * HEAD_DIM:(h + 1) * HEAD_DIM] = acc_ref[sl, :].astype(o_ref.dtype)


def _sb_attn_cached(q, k_new, v_new, k_past, v_past):
    b, _, t, _ = q.shape
    past = k_past.shape[2]
    assert past % SB_BLOCK == 0 and t <= SB_BLOCK and t % 16 == 0, (past, t)
    new = pl.BlockSpec((None, SB_HEADS, t, HEAD_DIM), lambda bi: (bi, 0, 0, 0))
    old = pl.BlockSpec((None, SB_HEADS, past, HEAD_DIM), lambda bi: (bi, 0, 0, 0))
    return pl.pallas_call(
        functools.partial(_sb_cached_body, t=t, past=past),
        grid=(b,),
        in_specs=[new, new, new, old, old],
        out_specs=pl.BlockSpec((None, t, D_TOK), lambda bi: (bi, 0, 0)),
        out_shape=jax.ShapeDtypeStruct((b, t, D_TOK), BF16),
        scratch_shapes=[pltpu.VMEM((SB_HEADS * t, HEAD_DIM), F32),
                        pltpu.VMEM((SB_HEADS * t, HEAD_DIM), F32),
                        pltpu.VMEM((2 * SB_BLOCK, SB_BLOCK + HEAD_DIM), BF16)],
        compiler_params=_params("parallel"),
        name="sb_attn_cached",
    )(q, k_new, v_new, k_past, v_past)


def _mem_attn_body(qm_ref, mk_ref, mv_ref, qg_ref, o_ref):
    for hd in range(MEM_HEADS):
        sl = slice(hd * HEAD_DIM, (hd + 1) * HEAD_DIM)
        qh = _rms(qm_ref[:, sl], qg_ref[...]).astype(BF16)
        s = _dot_nt(qh, mk_ref[:, sl].astype(BF16)) * INV_SQRT_HD
        p = jnp.exp(s - jnp.max(s, axis=-1, keepdims=True))
        p = p / jnp.sum(p, axis=-1, keepdims=True)
        o_ref[:, sl] = _dot(p.astype(BF16), mv_ref[:, sl].astype(BF16)).astype(BF16)


def _mem_attn(qm_src, qm_col, mem_k, mem_v, layer, q_gain):
    b, t, _ = qm_src.shape
    n_mem = mem_k.shape[2]
    tm = min(t, ROW_TILE)
    mem = pl.BlockSpec((None, None, n_mem, D_MEM), lambda bi, i: (layer, bi, 0, 0))
    return pl.pallas_call(
        _mem_attn_body,
        grid=(b, t // tm),
        in_specs=[
            pl.BlockSpec((None, tm, D_MEM), lambda bi, i: (bi, i, qm_col)),
            mem, mem,
            pl.BlockSpec((1, HEAD_DIM), lambda bi, i: (0, 0)),
        ],
        out_specs=pl.BlockSpec((None, tm, D_MEM), lambda bi, i: (bi, i, 0)),
        out_shape=jax.ShapeDtypeStruct((b, t, D_MEM), BF16),
        compiler_params=_params("parallel", "parallel"),
        name="mem_attn",
    )(qm_src, mem_k, mem_v, q_gain.reshape(1, HEAD_DIM))


def _out_proj_body(x_ref, mix_ref, mo_ref, w_ref, o_ref):
    o_ref[...] = (x_ref[...] + _dot(mix_ref[...], w_ref[:D_TOK, :]) + _dot(mo_ref[...], w_ref[D_TOK:, :]))


def _out_proj(x, mix, mo, w_out, layer):
    n, d = x.shape
    tm = min(n, ROW_TILE)
    row = lambda c: pl.BlockSpec((tm, c), lambda i: (i, 0))
    return pl.pallas_call(
        _out_proj_body,
        grid=(n // tm,),
        in_specs=[row(d), row(D_TOK), row(D_MEM), _resident((d, d), layer)],
        out_specs=row(d),
        out_shape=jax.ShapeDtypeStruct((n, d), F32),
        compiler_params=_params("parallel"),
        name="out_proj",
    )(x, mix, mo, w_out)


def _ffn_body(x_ref, g_ref, wu_ref, wd_ref, o_ref, h_ref):
    @pl.when(pl.program_id(1) == 0)
    def _():
        x = x_ref[...]
        h_ref[...] = _rms(x, g_ref[...]).astype(BF16)
        o_ref[...] = x

    up = jnp.maximum(_dot(h_ref[...], wu_ref[...]), 0.0)
    o_ref[...] += _dot((up * up).astype(BF16), wd_ref[...])


def _ffn(x, g, w_up, w_down, layer, tk=1024):
    n, d = x.shape
    f = w_up.shape[2]
    tm = min(n, ROW_TILE)
    return pl.pallas_call(
        _ffn_body,
        grid=(n // tm, f // tk),
        in_specs=[
            pl.BlockSpec((tm, d), lambda i, k: (i, 0)),
            pl.BlockSpec((1, d), lambda i, k: (0, 0)),
            pl.BlockSpec((None, d, tk), lambda i, k: (layer, 0, k)),
            pl.BlockSpec((None, tk, d), lambda i, k: (layer, k, 0)),
        ],
        out_specs=pl.BlockSpec((tm, d), lambda i, k: (i, 0)),
        out_shape=jax.ShapeDtypeStruct((n, d), F32),
        scratch_shapes=[pltpu.VMEM((tm, d), BF16)],
        compiler_params=_params("parallel", "arbitrary"),
        name="ffn",
    )(x, g.reshape(1, d), w_up, w_down)


def _run_trunk(x, past, pool_prev, sb_past_k, sb_past_v, mem_k, mem_v, p):
    b, t, d = x.shape
    n = b * t
    depth = p["w_out"].shape[0]
    pool_new, sbk_new, sbv_new = [], [], []
    x2 = x.reshape(n, d)
    for i in range(depth):
        j = i // 2
        if i % 2 == 0:
            proj = _norm_matmul(x2, p["norm_mix"][i], p["w_in_pool"], j).reshape(b, t, d)
            mix = _pool_mix(proj, pool_prev[j], p["pool_w"][j], p["pool_scale"][j], past)
            pool_new.append(proj[:, t - POOL_STATE:, :D_TOK])
            qm_src, qm_col = proj, D_TOK // D_MEM
        else:
            q, k, v, kb, vb, qm = _inproj_sb(x2, p["norm_mix"][i], p["w_in_sb"], j, b, t)
            if sb_past_k is None:
                mix = _sb_attn_prompt(q, kb, vb)
            else:
                mix = _sb_attn_cached(q, kb, vb, sb_past_k[j], sb_past_v[j])
            sbk_new.append(k)
            sbv_new.append(v)
            qm_src, qm_col = qm.reshape(b, t, D_MEM), 0
        mo = _mem_attn(qm_src, qm_col, mem_k, mem_v, i, p["q_norm"][i])
        x2 = _out_proj(x2, mix.reshape(n, D_TOK), mo.reshape(n, D_MEM), p["w_out"], i)
        x2 = _ffn(x2, p["norm_ffn"][i], p["w_up"], p["w_down"], i)
    return x2.reshape(b, t, d), jnp.stack(pool_new), jnp.stack(sbk_new), jnp.stack(sbv_new)


def kernel(x_prompt, x_sample, mem_prompt, cache_pool, cache_sb_k, cache_sb_v, cache_mem_k, cache_mem_v,
           norm_mix, w_in_pool, w_in_sb, pool_w, pool_scale, norm_mem, w_mem_kv, q_norm, k_norm, w_out,
           norm_ffn, w_up, w_down):
    b, t, d = x_prompt.shape
    depth = w_out.shape[0]
    n_mem = mem_prompt.shape[1]
    p = dict(norm_mix=norm_mix, pool_scale=pool_scale, q_norm=q_norm, norm_ffn=norm_ffn,
             w_in_pool=w_in_pool.astype(BF16), w_in_sb=w_in_sb.astype(BF16), pool_w=pool_w.astype(BF16),
             w_out=w_out.astype(BF16), w_up=w_up.astype(BF16), w_down=w_down.astype(BF16))

    mk, mv = _mem_kv(mem_prompt.reshape(b * n_mem, d), norm_mem, w_mem_kv.astype(BF16), k_norm)
    mk = mk.reshape(depth, b, n_mem, D_MEM)
    mv = mv.reshape(depth, b, n_mem, D_MEM)
    pool_zero = jnp.zeros((cache_pool.shape[0], b, POOL_HALO, D_TOK), F32)
    y_prompt, pool_prompt, sb_k_prompt, sb_v_prompt = _run_trunk(
        x_prompt, 0, pool_zero, None, None, mk, mv, p)

    bs = x_sample.shape[0]
    past = cache_sb_k.shape[2]
    pool_prev = jnp.pad(cache_pool, ((0, 0), (0, 0), (POOL_HALO - POOL_STATE, 0), (0, 0)))
    head_major = lambda a: jnp.transpose(a, (0, 1, 3, 2, 4))
    y_sample, pool_sample, sb_k_sample, sb_v_sample = _run_trunk(
        x_sample, past, pool_prev, head_major(cache_sb_k), head_major(cache_sb_v),
        cache_mem_k.reshape(depth, bs, n_mem, D_MEM), cache_mem_v.reshape(depth, bs, n_mem, D_MEM), p)

    heads = lambda a: a.reshape(depth, b, n_mem, MEM_HEADS, HEAD_DIM)
    return (y_prompt, y_sample, pool_prompt, pool_sample, head_major(sb_k_prompt), head_major(sb_v_prompt),
            head_major(sb_k_sample), head_major(sb_v_sample), heads(mk), heads(mv))
```

```python
import functools
import math

import jax
import jax.numpy as jnp
from jax import lax
from jax.experimental import pallas as pl
from jax.experimental.pallas import tpu as pltpu

D_MODEL = 2048
HEAD_DIM = 128
D_TOK = 3 * D_MODEL // 4
D_MEM = D_MODEL // 4
SB_HEADS = D_TOK // HEAD_DIM
MEM_HEADS = D_MEM // HEAD_DIM
POOL_WINDOWS = (2, 4, 8, 16)
POOL_GROUP = D_TOK // len(POOL_WINDOWS)
POOL_STATE = max(POOL_WINDOWS) - 1
POOL_HALO = POOL_STATE + 1
D_FF = 4 * D_MODEL
EPS = 1e-6
INV_SQRT_HD = 1.0 / math.sqrt(HEAD_DIM)

SB_BLOCK = 128
SB_GROUP = 16
EXP_ZERO_BELOW = -104.0
RUN_FINISHED = -1e30
ROW_TILE = 512
VMEM_LIMIT = 48 * 1024 * 1024

BF16 = jnp.bfloat16
F32 = jnp.float32


def _params(*sem):
    return pltpu.CompilerParams(dimension_semantics=sem, vmem_limit_bytes=VMEM_LIMIT)


def _rms(xf, g):
    return xf * lax.rsqrt(jnp.mean(xf * xf, axis=-1, keepdims=True) + EPS) * g


def _dot(a, b):
    return jnp.dot(a, b, preferred_element_type=F32)


def _dot_nt(a, b):
    return lax.dot_general(a, b, (((1,), (1,)), ((), ())), preferred_element_type=F32)


def _resident(shape, layer):
    return pl.BlockSpec((None,) + shape, lambda *_: (layer,) + (0,) * len(shape), pipeline_mode=pl.Buffered(1))


def _norm_matmul_body(x_ref, g_ref, w_ref, o_ref):
    o_ref[...] = _dot(_rms(x_ref[...], g_ref[...]).astype(BF16), w_ref[...])


def _norm_matmul(x, g, w, layer):
    n, d = x.shape
    m = w.shape[2]
    tm = min(n, ROW_TILE)
    return pl.pallas_call(
        _norm_matmul_body,
        grid=(n // tm,),
        in_specs=[
            pl.BlockSpec((tm, d), lambda i: (i, 0)),
            pl.BlockSpec((1, d), lambda i: (0, 0)),
            _resident((d, m), layer),
        ],
        out_specs=pl.BlockSpec((tm, m), lambda i: (i, 0)),
        out_shape=jax.ShapeDtypeStruct((n, m), F32),
        compiler_params=_params("parallel"),
        name="norm_matmul",
    )(x, g.reshape(1, d), w)


def _inproj_sb_body(x_ref, g_ref, wq_ref, wk_ref, wv_ref, wm_ref,
                    q_ref, k_ref, v_ref, kb_ref, vb_ref, qm_ref, h_ref):
    @pl.when(pl.program_id(1) == 0)
    def _():
        h_ref[...] = _rms(x_ref[...], g_ref[...]).astype(BF16)
        qm_ref[...] = _dot(h_ref[...], wm_ref[...])

    def put(refs, val):
        bb, heads, tt, _ = refs[0].shape
        for bi in range(bb):
            for hd in range(heads):
                piece = val[bi * tt:(bi + 1) * tt, hd * HEAD_DIM:(hd + 1) * HEAD_DIM]
                for ref in refs:
                    ref[bi, hd] = piece.astype(ref.dtype)

    h = h_ref[...]
    put([q_ref], _dot(h, wq_ref[...]) * INV_SQRT_HD)
    put([k_ref, kb_ref], _dot(h, wk_ref[...]))
    put([v_ref, vb_ref], _dot(h, wv_ref[...]))


def _inproj_sb(x, g, w, layer, b, t):
    n, d = x.shape
    tm = min(n, ROW_TILE)
    tn = D_MEM
    nj = D_TOK // tn
    hj = tn // HEAD_DIM
    if t >= tm:
        per_seq = t // tm
        oblock, omap = (1, hj, tm, HEAD_DIM), lambda i, j: (i // per_seq, j, i % per_seq, 0)
    else:
        oblock, omap = (tm // t, hj, t, HEAD_DIM), lambda i, j: (i, j, 0, 0)
    tok = lambda dt: jax.ShapeDtypeStruct((b, SB_HEADS, t, HEAD_DIM), dt)
    wspec = lambda off: pl.BlockSpec((None, d, tn), lambda i, j: (layer, 0, off + j))
    ospec = pl.BlockSpec(oblock, omap)
    return pl.pallas_call(
        _inproj_sb_body,
        grid=(n // tm, nj),
        in_specs=[
            pl.BlockSpec((tm, d), lambda i, j: (i, 0)),
            pl.BlockSpec((1, d), lambda i, j: (0, 0)),
            wspec(0), wspec(nj), wspec(2 * nj),
            pl.BlockSpec((None, d, D_MEM), lambda i, j: (layer, 0, 3 * nj)),
        ],
        out_specs=[ospec, ospec, ospec, ospec, ospec,
                   pl.BlockSpec((tm, D_MEM), lambda i, j: (i, 0))],
        out_shape=[tok(BF16), tok(F32), tok(F32), tok(BF16), tok(BF16),
                   jax.ShapeDtypeStruct((n, D_MEM), F32)],
        scratch_shapes=[pltpu.VMEM((tm, d), BF16)],
        compiler_params=_params("parallel", "arbitrary"),
        name="inproj_sb",
    )(x, g.reshape(1, d), w, w, w, w)


def _mem_kv_body(x_ref, g_ref, w_ref, kg_ref, k_ref, v_ref):
    h = _rms(x_ref[...], g_ref[...]).astype(BF16)
    kv = _dot(h, w_ref[...])
    for hd in range(MEM_HEADS):
        sl = slice(hd * HEAD_DIM, (hd + 1) * HEAD_DIM)
        k_ref[:, sl] = _rms(kv[:, sl], kg_ref[...])
    v_ref[...] = kv[:, D_MEM:]


def _mem_kv(mem, g_mem, w_kv, k_gain):
    r, d = mem.shape
    depth = w_kv.shape[0]
    tm = min(r, 256)
    out = jax.ShapeDtypeStruct((depth, r, D_MEM), F32)
    ospec = pl.BlockSpec((None, tm, D_MEM), lambda l, i: (l, i, 0))
    return pl.pallas_call(
        _mem_kv_body,
        grid=(depth, r // tm),
        in_specs=[
            pl.BlockSpec((tm, d), lambda l, i: (i, 0)),
            pl.BlockSpec((None, 1, d), lambda l, i: (l, 0, 0)),
            pl.BlockSpec((None, d, 2 * D_MEM), lambda l, i: (l, 0, 0)),
            pl.BlockSpec((None, 1, HEAD_DIM), lambda l, i: (l, 0, 0)),
        ],
        out_specs=[ospec, ospec],
        out_shape=[out, out],
        compiler_params=_params("parallel", "parallel"),
        name="mem_kv",
    )(mem, g_mem.reshape(depth, 1, d), w_kv, k_gain.reshape(depth, 1, HEAD_DIM))


def _pool_body(u_ref, halo_ref, prev_ref, w_ref, s_ref, o_ref, ext_ref, *, tm, past):
    i = pl.program_id(1)

    @pl.when(i == 0)
    def _():
        ext_ref[0:POOL_HALO, :] = prev_ref[...]

    @pl.when(i > 0)
    def _():
        ext_ref[0:POOL_HALO, :] = halo_ref[...]

    ext_ref[POOL_HALO:, :] = u_ref[...]
    pos = (past + i * tm + lax.broadcasted_iota(jnp.int32, (tm, 1), 0)).astype(F32)
    for g, w in enumerate(POOL_WINDOWS):
        sl = slice(g * POOL_GROUP, (g + 1) * POOL_GROUP)
        cur = ext_ref[POOL_HALO:, sl]
        win = cur
        for back in range(1, w):
            win = win + ext_ref[POOL_HALO - back:POOL_HALO - back + tm, sl]
        cnt = jnp.minimum(pos + 1.0, float(w))
        dlt = (win / cnt - cur).astype(BF16)
        o_ref[:, sl] = (_dot(dlt, w_ref[g]) * s_ref[:, sl]).astype(BF16)


def _pool_mix(proj, prev, w_grp, scale, past):
    b, t, _ = proj.shape
    tm = min(t, 512)
    hb = tm // POOL_HALO
    return pl.pallas_call(
        functools.partial(_pool_body, tm=tm, past=past),
        grid=(b, t // tm),
        in_specs=[
            pl.BlockSpec((None, tm, D_TOK), lambda bi, i: (bi, i, 0)),
            pl.BlockSpec((None, POOL_HALO, D_TOK), lambda bi, i: (bi, jnp.maximum(i * hb - 1, 0), 0)),
            pl.BlockSpec((None, POOL_HALO, D_TOK), lambda bi, i: (bi, 0, 0)),
            pl.BlockSpec((len(POOL_WINDOWS), POOL_GROUP, POOL_GROUP), lambda bi, i: (0, 0, 0)),
            pl.BlockSpec((1, D_TOK), lambda bi, i: (0, 0)),
        ],
        out_specs=pl.BlockSpec((None, tm, D_TOK), lambda bi, i: (bi, i, 0)),
        out_shape=jax.ShapeDtypeStruct((b, t, D_TOK), BF16),
        scratch_shapes=[pltpu.VMEM((tm + POOL_HALO, D_TOK), F32)],
        compiler_params=_params("parallel", "arbitrary"),
        name="pool_mix",
    )(proj, proj, prev, w_grp, scale.reshape(1, D_TOK))


def _scan_matrix():
    shape = (2 * SB_BLOCK, SB_BLOCK + HEAD_DIM)
    r = lax.broadcasted_iota(jnp.int32, shape, 0) % SB_BLOCK
    c = lax.broadcasted_iota(jnp.int32, shape, 1)
    return jnp.where((c >= SB_BLOCK) | (r > c), 1.0, 0.0).astype(BF16)


def _sb_steps(qs, ks, vs, runs, mask, scan):
    tq = qs[0].shape[0]
    zs = [_dot_nt(q, k) for q, k in zip(qs, ks)]
    fails, parts = [], []
    for z in zs:
        log_fail = -(jnp.maximum(z, 0.0) + jnp.log(1.0 + jnp.exp(-jnp.abs(z))))
        if mask is not None:
            log_fail = jnp.where(mask, log_fail, 0.0)
        hi = log_fail.astype(BF16)
        lo = (log_fail - hi.astype(F32)).astype(BF16)
        fails.append(log_fail)
        parts.append(jnp.concatenate([hi, lo], axis=1))
    sums = _dot(jnp.concatenate(parts, axis=0), scan)
    weights, new_runs = [], []
    for g, (z, log_fail, run) in enumerate(zip(zs, fails, runs)):
        s = sums[g * tq:(g + 1) * tq]
        a = jnp.exp(z + log_fail + s[:, :SB_BLOCK] + run)
        if mask is not None:
            a = jnp.where(mask, a, 0.0)
        weights.append(a.astype(BF16))
        new_runs.append(run + s[:, SB_BLOCK:])
    return new_runs, [_dot(a, v) for a, v in zip(weights, vs)]


def _largest(values):
    out = values[0]
    for v in values[1:]:
        out = jnp.maximum(out, v)
    return jnp.max(out)


def _sb_prompt_body(q_ref, k_ref, v_ref, o_ref, run_ref, acc_ref, scan_ref, *, t, group):
    scan_ref[...] = _scan_matrix()
    below_diag = (lax.broadcasted_iota(jnp.int32, (SB_BLOCK, SB_BLOCK), 1) <
                  lax.broadcasted_iota(jnp.int32, (SB_BLOCK, SB_BLOCK), 0))
    zero = jnp.zeros((SB_BLOCK, HEAD_DIM), F32)
    chain = [slice(g * SB_BLOCK, (g + 1) * SB_BLOCK) for g in range(group)]

    def block(i):
        return pl.ds(pl.multiple_of(i * SB_BLOCK, SB_BLOCK), SB_BLOCK)

    def group_scan(s, _):
        first = s * group
        tiles = [block(first + g) for g in range(group)]
        runs, pvs = _sb_steps([q_ref[r, :] for r in tiles], [k_ref[r, :] for r in tiles],
                              [v_ref[r, :] for r in tiles], [zero] * group, below_diag, scan_ref[...])
        for sl, run, pv in zip(chain, runs, pvs):
            run_ref[sl, :] = run
            acc_ref[sl, :] = pv

        def more(c):
            it, largest_run = c
            return jnp.logical_and(it < first + group, largest_run > EXP_ZERO_BELOW)

        def step(c):
            it, _ = c
            keys = [block(jnp.maximum(first + g - it, 0)) for g in range(group)]
            runs = [jnp.where(first + g - it >= 0, run_ref[chain[g], :], RUN_FINISHED) for g in range(group)]
            runs, pvs = _sb_steps([q_ref[r, :] for r in tiles], [k_ref[r, :] for r in keys],
                                  [v_ref[r, :] for r in keys], runs, None, scan_ref[...])
            for sl, run, pv in zip(chain, runs, pvs):
                run_ref[sl, :] = run
                acc_ref[sl, :] += pv
            return it + 1, _largest(runs)

        lax.while_loop(more, step, (jnp.int32(1), _largest(runs)))
        out_rows = pl.ds(pl.multiple_of(first * SB_BLOCK, SB_BLOCK), group * SB_BLOCK)
        o_ref[out_rows, :] = acc_ref[...].astype(o_ref.dtype)
        return 0

    lax.fori_loop(0, t // (group * SB_BLOCK), group_scan, 0)


def _sb_attn_prompt(q, k, v):
    b, _, t, _ = q.shape
    group = min(SB_GROUP, t // SB_BLOCK)
    assert t % (group * SB_BLOCK) == 0, t
    spec = pl.BlockSpec((None, None, t, HEAD_DIM), lambda bi, h: (bi, h, 0, 0))
    return pl.pallas_call(
        functools.partial(_sb_prompt_body, t=t, group=group),
        grid=(b, SB_HEADS),
        in_specs=[spec, spec, spec],
        out_specs=pl.BlockSpec((None, t, HEAD_DIM), lambda bi, h: (bi, 0, h)),
        out_shape=jax.ShapeDtypeStruct((b, t, D_TOK), BF16),
        scratch_shapes=[pltpu.VMEM((group * SB_BLOCK, HEAD_DIM), F32),
                        pltpu.VMEM((group * SB_BLOCK, HEAD_DIM), F32),
                        pltpu.VMEM((2 * SB_BLOCK, SB_BLOCK + HEAD_DIM), BF16)],
        compiler_params=_params("parallel", "parallel"),
        name="sb_attn_prompt",
    )(q, k, v)


def _sb_cached_body(q_ref, kn_ref, vn_ref, kc_ref, vc_ref, o_ref, run_ref, acc_ref, scan_ref, *, t, past):
    scan_ref[...] = _scan_matrix()
    earlier = (lax.broadcasted_iota(jnp.int32, (t, SB_BLOCK), 1) <
               lax.broadcasted_iota(jnp.int32, (t, SB_BLOCK), 0))
    zero = jnp.zeros((t, HEAD_DIM), F32)
    pad = jnp.zeros((SB_BLOCK - t, HEAD_DIM), BF16)
    heads = range(SB_HEADS)
    chain = [slice(h * t, (h + 1) * t) for h in heads]

    runs, pvs = _sb_steps([q_ref[h] for h in heads],
                          [jnp.concatenate([kn_ref[h], pad], axis=0) for h in heads],
                          [jnp.concatenate([vn_ref[h], pad], axis=0) for h in heads],
                          [zero] * SB_HEADS, earlier, scan_ref[...])
    for sl, run, pv in zip(chain, runs, pvs):
        run_ref[sl, :] = run
        acc_ref[sl, :] = pv

    def more(c):
        it, largest_run = c
        return jnp.logical_and(it < past // SB_BLOCK, largest_run > EXP_ZERO_BELOW)

    def step(c):
        it, _ = c
        keys = pl.ds(pl.multiple_of(past - (it + 1) * SB_BLOCK, SB_BLOCK), SB_BLOCK)
        runs, pvs = _sb_steps([q_ref[h] for h in heads], [kc_ref[h, keys, :].astype(BF16) for h in heads],
                              [vc_ref[h, keys, :].astype(BF16) for h in heads], [run_ref[sl, :] for sl in chain],
                              None, scan_ref[...])
        for sl, run, pv in zip(chain, runs, pvs):
            run_ref[sl, :] = run
            acc_ref[sl, :] += pv
        return it + 1, _largest(runs)

    lax.while_loop(more, step, (jnp.int32(0), _largest(runs)))
    for h, sl in zip(heads, chain):
        o_ref[:, h * HEAD_DIM:(h + 1) * HEAD_DIM] = acc_ref[sl, :].astype(o_ref.dtype)


def _sb_attn_cached(q, k_new, v_new, k_past, v_past):
    b, _, t, _ = q.shape
    past = k_past.shape[2]
    assert past % SB_BLOCK == 0 and t <= SB_BLOCK and t % 16 == 0, (past, t)
    new = pl.BlockSpec((None, SB_HEADS, t, HEAD_DIM), lambda bi: (bi, 0, 0, 0))
    old = pl.BlockSpec((None, SB_HEADS, past, HEAD_DIM), lambda bi: (bi, 0, 0, 0))
    return pl.pallas_call(
        functools.partial(_sb_cached_body, t=t, past=past),
        grid=(b,),
        in_specs=[new, new, new, old, old],
        out_specs=pl.BlockSpec((None, t, D_TOK), lambda bi: (bi, 0, 0)),
        out_shape=jax.ShapeDtypeStruct((b, t, D_TOK), BF16),
        scratch_shapes=[pltpu.VMEM((SB_HEADS * t, HEAD_DIM), F32),
                        pltpu.VMEM((SB_HEADS * t, HEAD_DIM), F32),
                        pltpu.VMEM((2 * SB_BLOCK, SB_BLOCK + HEAD_DIM), BF16)],
        compiler_params=_params("parallel"),
        name="sb_attn_cached",
    )(q, k_new, v_new, k_past, v_past)


def _mix_out_body(x_ref, mix_ref, qm_ref, mk_ref, mv_ref, qg_ref, w_ref, o_ref):
    bb = mk_ref.shape[0]
    tt = x_ref.shape[0] // bb
    probs = []
    for bi in range(bb):
        rows = slice(bi * tt, (bi + 1) * tt)
        for hd in range(MEM_HEADS):
            sl = slice(hd * HEAD_DIM, (hd + 1) * HEAD_DIM)
            qh = _rms(qm_ref[rows, sl], qg_ref[...]).astype(BF16)
            s = _dot_nt(qh, mk_ref[bi, :, sl].astype(BF16)) * INV_SQRT_HD
            p = jnp.exp(s - jnp.max(s, axis=-1, keepdims=True))
            probs.append((p / jnp.sum(p, axis=-1, keepdims=True)).astype(BF16))
    y = x_ref[...] + _dot(mix_ref[...], w_ref[:D_TOK, :])
    mo = jnp.concatenate(
        [jnp.concatenate([_dot(probs[bi * MEM_HEADS + hd],
                               mv_ref[bi, :, hd * HEAD_DIM:(hd + 1) * HEAD_DIM].astype(BF16))
                          for hd in range(MEM_HEADS)], axis=1) for bi in range(bb)], axis=0)
    o_ref[...] = y + _dot(mo.astype(BF16), w_ref[D_TOK:, :])


def _mix_out(x, mix, qm_src, qm_col, mem_k, mem_v, q_gain, w_out, layer, t):
    n, d = x.shape
    n_mem = mem_k.shape[2]
    tm = min(n, ROW_TILE)
    if t >= tm:
        per_seq = t // tm
        mem = pl.BlockSpec((None, 1, n_mem, D_MEM), lambda i: (layer, i // per_seq, 0, 0))
    else:
        mem = pl.BlockSpec((None, tm // t, n_mem, D_MEM), lambda i: (layer, i, 0, 0),
                           pipeline_mode=pl.Buffered(1))
    row = lambda c, col=0: pl.BlockSpec((tm, c), lambda i: (i, col))
    return pl.pallas_call(
        _mix_out_body,
        grid=(n // tm,),
        in_specs=[row(d), row(D_TOK), row(D_MEM, qm_col), mem, mem,
                  pl.BlockSpec((1, HEAD_DIM), lambda i: (0, 0)), _resident((d, d), layer)],
        out_specs=row(d),
        out_shape=jax.ShapeDtypeStruct((n, d), F32),
        compiler_params=_params("parallel"),
        name="mix_out",
    )(x, mix, qm_src, mem_k, mem_v, q_gain.reshape(1, HEAD_DIM), w_out)


def _ffn_body(x_ref, g_ref, wu_ref, wd_ref, o_ref, *rest):
    h_ref = rest[-1]

    @pl.when(pl.program_id(1) == 0)
    def _():
        x = x_ref[...]
        h_ref[...] = _rms(x, g_ref[...]).astype(BF16)
        o_ref[...] = x

    wu = wu_ref[...].astype(BF16)
    wd = wd_ref[...].astype(BF16)
    if len(rest) == 3:
        rest[0][...] = wu
        rest[1][...] = wd
    up = jnp.maximum(_dot(h_ref[...], wu), 0.0)
    o_ref[...] += _dot((up * up).astype(BF16), wd)


def _ffn(x, g, w_up, w_down):
    n, d = x.shape
    f = w_up.shape[1]
    tm = min(n, ROW_TILE)
    tk = 1024
    return pl.pallas_call(
        _ffn_body,
        grid=(n // tm, f // tk),
        in_specs=[
            pl.BlockSpec((tm, d), lambda i, k: (i, 0)),
            pl.BlockSpec((1, d), lambda i, k: (0, 0)),
            pl.BlockSpec((d, tk), lambda i, k: (0, k)),
            pl.BlockSpec((tk, d), lambda i, k: (k, 0)),
        ],
        out_specs=pl.BlockSpec((tm, d), lambda i, k: (i, 0)),
        out_shape=jax.ShapeDtypeStruct((n, d), F32),
        scratch_shapes=[pltpu.VMEM((tm, d), BF16)],
        compiler_params=_params("parallel", "arbitrary"),
        name="ffn",
    )(x, g.reshape(1, d), w_up, w_down)


def _ffn_casting(x, g, w_up, w_down, layer):
    n, d = x.shape
    f = w_up.shape[2]
    tk = 512
    assert n <= ROW_TILE, n
    return pl.pallas_call(
        _ffn_body,
        grid=(1, f // tk),
        in_specs=[
            pl.BlockSpec((n, d), lambda i, k: (0, 0)),
            pl.BlockSpec((1, d), lambda i, k: (0, 0)),
            pl.BlockSpec((None, d, tk), lambda i, k: (layer, 0, k)),
            pl.BlockSpec((None, tk, d), lambda i, k: (layer, k, 0)),
        ],
        out_specs=[pl.BlockSpec((n, d), lambda i, k: (0, 0)),
                   pl.BlockSpec((d, tk), lambda i, k: (0, k)),
                   pl.BlockSpec((tk, d), lambda i, k: (k, 0))],
        out_shape=[jax.ShapeDtypeStruct((n, d), F32), jax.ShapeDtypeStruct((d, f), BF16),
                   jax.ShapeDtypeStruct((f, d), BF16)],
        scratch_shapes=[pltpu.VMEM((n, d), BF16)],
        compiler_params=_params("arbitrary", "arbitrary"),
        name="ffn_casting",
    )(x, g.reshape(1, d), w_up, w_down)


def _run_trunk(x, past, pool_prev, sb_past_k, sb_past_v, mem_k, mem_v, p, ffn_bf16):
    b, t, d = x.shape
    n = b * t
    depth = p["w_out"].shape[0]
    pool_new, sbk_new, sbv_new, made = [], [], [], []
    x2 = x.reshape(n, d)
    for i in range(depth):
        j = i // 2
        if i % 2 == 0:
            proj = _norm_matmul(x2, p["norm_mix"][i], p["w_in_pool"], j)
            proj3 = proj.reshape(b, t, d)
            mix = _pool_mix(proj3, pool_prev[j], p["pool_w"][j], p["pool_scale"][j], past)
            pool_new.append(proj3[:, t - POOL_STATE:, :D_TOK])
            qm_src, qm_col = proj, D_TOK // D_MEM
        else:
            q, k, v, kb, vb, qm = _inproj_sb(x2, p["norm_mix"][i], p["w_in_sb"], j, b, t)
            if sb_past_k is None:
                mix = _sb_attn_prompt(q, kb, vb)
            else:
                mix = _sb_attn_cached(q, kb, vb, sb_past_k[j], sb_past_v[j])
            sbk_new.append(k)
            sbv_new.append(v)
            qm_src, qm_col = qm, 0
        x2 = _mix_out(x2, mix.reshape(n, D_TOK), qm_src, qm_col, mem_k, mem_v, p["q_norm"][i], p["w_out"], i, t)
        if ffn_bf16 is None:
            x2, wu, wd = _ffn_casting(x2, p["norm_ffn"][i], p["w_up"], p["w_down"], i)
            made.append((wu, wd))
        else:
            x2 = _ffn(x2, p["norm_ffn"][i], *ffn_bf16[i])
    return x2.reshape(b, t, d), jnp.stack(pool_new), jnp.stack(sbk_new), jnp.stack(sbv_new), made


def kernel(x_prompt, x_sample, mem_prompt, cache_pool, cache_sb_k, cache_sb_v, cache_mem_k, cache_mem_v,
           norm_mix, w_in_pool, w_in_sb, pool_w, pool_scale, norm_mem, w_mem_kv, q_norm, k_norm, w_out,
           norm_ffn, w_up, w_down):
    b, t, d = x_prompt.shape
    depth = w_out.shape[0]
    n_mem = mem_prompt.shape[1]
    p = dict(norm_mix=norm_mix, pool_scale=pool_scale, q_norm=q_norm, norm_ffn=norm_ffn,
             w_in_pool=w_in_pool.astype(BF16), w_in_sb=w_in_sb.astype(BF16), pool_w=pool_w.astype(BF16),
             w_out=w_out.astype(BF16), w_up=w_up, w_down=w_down)

    bs = x_sample.shape[0]
    past = cache_sb_k.shape[2]
    pool_prev = jnp.pad(cache_pool, ((0, 0), (0, 0), (POOL_HALO - POOL_STATE, 0), (0, 0)))
    head_major = lambda a: jnp.transpose(a, (0, 1, 3, 2, 4))
    y_sample, pool_sample, sb_k_sample, sb_v_sample, ffn_bf16 = _run_trunk(
        x_sample, past, pool_prev, head_major(cache_sb_k), head_major(cache_sb_v),
        cache_mem_k.reshape(depth, bs, n_mem, D_MEM), cache_mem_v.reshape(depth, bs, n_mem, D_MEM), p, None)

    mk, mv = _mem_kv(mem_prompt.reshape(b * n_mem, d), norm_mem, w_mem_kv.astype(BF16), k_norm)
    mk = mk.reshape(depth, b, n_mem, D_MEM)
    mv = mv.reshape(depth, b, n_mem, D_MEM)
    pool_zero = jnp.zeros((cache_pool.shape[0], b, POOL_HALO, D_TOK), F32)
    y_prompt, pool_prompt, sb_k_prompt, sb_v_prompt, _ = _run_trunk(
        x_prompt, 0, pool_zero, None, None, mk, mv, p, ffn_bf16)

    heads = lambda a: a.reshape(depth, b, n_mem, MEM_HEADS, HEAD_DIM)
    return (y_prompt, y_sample, pool_prompt, pool_sample, head_major(sb_k_prompt), head_major(sb_v_prompt),
            head_major(sb_k_sample), head_major(sb_v_sample), heads(mk), heads(mv))
```

```python
import functools
import math

import jax
import jax.numpy as jnp
from jax import lax
from jax.experimental import pallas as pl
from jax.experimental.pallas import tpu as pltpu

D_MODEL = 2048
HEAD_DIM = 128
D_TOK = 3 * D_MODEL // 4
D_MEM = D_MODEL // 4
SB_HEADS = D_TOK // HEAD_DIM
MEM_HEADS = D_MEM // HEAD_DIM
POOL_WINDOWS = (2, 4, 8, 16)
POOL_GROUP = D_TOK // len(POOL_WINDOWS)
POOL_STATE = max(POOL_WINDOWS) - 1
POOL_HALO = POOL_STATE + 1
D_FF = 4 * D_MODEL
EPS = 1e-6
INV_SQRT_HD = 1.0 / math.sqrt(HEAD_DIM)

SB_BLOCK = 128
SB_GROUP = 32
EXP_ZERO_BELOW = -104.0
RUN_FINISHED = -1e30
ROW_TILE = 512
VMEM_LIMIT = 48 * 1024 * 1024

BF16 = jnp.bfloat16
F32 = jnp.float32


def _params(*sem):
    return pltpu.CompilerParams(dimension_semantics=sem, vmem_limit_bytes=VMEM_LIMIT)


def _rms(xf, g):
    return xf * lax.rsqrt(jnp.mean(xf * xf, axis=-1, keepdims=True) + EPS) * g


def _dot(a, b):
    return jnp.dot(a, b, preferred_element_type=F32)


def _dot_nt(a, b):
    return lax.dot_general(a, b, (((1,), (1,)), ((), ())), preferred_element_type=F32)


def _resident(shape, layer):
    return pl.BlockSpec((None,) + shape, lambda *_: (layer,) + (0,) * len(shape), pipeline_mode=pl.Buffered(1))


def _norm_matmul_body(x_ref, g_ref, w_ref, o_ref):
    o_ref[...] = _dot(_rms(x_ref[...], g_ref[...]).astype(BF16), w_ref[...])


def _norm_matmul(x, g, w, layer):
    n, d = x.shape
    m = w.shape[2]
    tm = min(n, ROW_TILE)
    return pl.pallas_call(
        _norm_matmul_body,
        grid=(n // tm,),
        in_specs=[
            pl.BlockSpec((tm, d), lambda i: (i, 0)),
            pl.BlockSpec((1, d), lambda i: (0, 0)),
            _resident((d, m), layer),
        ],
        out_specs=pl.BlockSpec((tm, m), lambda i: (i, 0)),
        out_shape=jax.ShapeDtypeStruct((n, m), F32),
        compiler_params=_params("parallel"),
        name="norm_matmul",
    )(x, g.reshape(1, d), w)


def _inproj_sb_body(x_ref, g_ref, wq_ref, wk_ref, wv_ref, wm_ref,
                    q_ref, k_ref, v_ref, kb_ref, vb_ref, qm_ref, h_ref):
    @pl.when(pl.program_id(1) == 0)
    def _():
        h_ref[...] = _rms(x_ref[...], g_ref[...]).astype(BF16)
        qm_ref[...] = _dot(h_ref[...], wm_ref[...])

    def put(refs, val):
        bb, heads, tt, _ = refs[0].shape
        for bi in range(bb):
            for hd in range(heads):
                piece = val[bi * tt:(bi + 1) * tt, hd * HEAD_DIM:(hd + 1) * HEAD_DIM]
                for ref in refs:
                    ref[bi, hd] = piece.astype(ref.dtype)

    h = h_ref[...]
    put([q_ref], _dot(h, wq_ref[...]) * INV_SQRT_HD)
    put([k_ref, kb_ref], _dot(h, wk_ref[...]))
    put([v_ref, vb_ref], _dot(h, wv_ref[...]))


def _inproj_sb(x, g, w, layer, b, t):
    n, d = x.shape
    tm = min(n, 2 * ROW_TILE)
    tn = 2 * HEAD_DIM
    nj = D_TOK // tn
    hj = tn // HEAD_DIM
    if t >= tm:
        per_seq = t // tm
        oblock, omap = (1, hj, tm, HEAD_DIM), lambda i, j: (i // per_seq, j, i % per_seq, 0)
    else:
        oblock, omap = (tm // t, hj, t, HEAD_DIM), lambda i, j: (i, j, 0, 0)
    tok = lambda dt: jax.ShapeDtypeStruct((b, SB_HEADS, t, HEAD_DIM), dt)
    wspec = lambda off: pl.BlockSpec((None, d, tn), lambda i, j: (layer, 0, off + j))
    ospec = pl.BlockSpec(oblock, omap)
    return pl.pallas_call(
        _inproj_sb_body,
        grid=(n // tm, nj),
        in_specs=[
            pl.BlockSpec((tm, d), lambda i, j: (i, 0)),
            pl.BlockSpec((1, d), lambda i, j: (0, 0)),
            wspec(0), wspec(nj), wspec(2 * nj),
            pl.BlockSpec((None, d, D_MEM), lambda i, j: (layer, 0, 3 * D_TOK // D_MEM)),
        ],
        out_specs=[ospec, ospec, ospec, ospec, ospec,
                   pl.BlockSpec((tm, D_MEM), lambda i, j: (i, 0))],
        out_shape=[tok(BF16), tok(F32), tok(F32), tok(BF16), tok(BF16),
                   jax.ShapeDtypeStruct((n, D_MEM), F32)],
        scratch_shapes=[pltpu.VMEM((tm, d), BF16)],
        compiler_params=_params("parallel", "arbitrary"),
        name="inproj_sb",
    )(x, g.reshape(1, d), w, w, w, w)


def _mem_kv_body(x_ref, g_ref, w_ref, kg_ref, k_ref, v_ref):
    h = _rms(x_ref[...], g_ref[...]).astype(BF16)
    kv = _dot(h, w_ref[...])
    for hd in range(MEM_HEADS):
        sl = slice(hd * HEAD_DIM, (hd + 1) * HEAD_DIM)
        k_ref[:, sl] = _rms(kv[:, sl], kg_ref[...])
    v_ref[...] = kv[:, D_MEM:]


def _mem_kv(mem, g_mem, w_kv, k_gain):
    r, d = mem.shape
    depth = w_kv.shape[0]
    tm = min(r, 256)
    out = jax.ShapeDtypeStruct((depth, r, D_MEM), F32)
    ospec = pl.BlockSpec((None, tm, D_MEM), lambda l, i: (l, i, 0))
    return pl.pallas_call(
        _mem_kv_body,
        grid=(depth, r // tm),
        in_specs=[
            pl.BlockSpec((tm, d), lambda l, i: (i, 0)),
            pl.BlockSpec((None, 1, d), lambda l, i: (l, 0, 0)),
            pl.BlockSpec((None, d, 2 * D_MEM), lambda l, i: (l, 0, 0)),
            pl.BlockSpec((None, 1, HEAD_DIM), lambda l, i: (l, 0, 0)),
        ],
        out_specs=[ospec, ospec],
        out_shape=[out, out],
        compiler_params=_params("parallel", "parallel"),
        name="mem_kv",
    )(mem, g_mem.reshape(depth, 1, d), w_kv, k_gain.reshape(depth, 1, HEAD_DIM))


def _pool_body(u_ref, halo_ref, prev_ref, w_ref, s_ref, o_ref, ext_ref, a_ref, b_ref, *, tm, past):
    i = pl.program_id(1)
    base = 2 * POOL_HALO
    rows = tm + POOL_HALO
    ext_ref[0:POOL_HALO, :] = jnp.zeros((POOL_HALO, D_TOK), F32)
    a_ref[0:POOL_HALO, :] = jnp.zeros((POOL_HALO, POOL_GROUP), F32)
    b_ref[0:POOL_HALO, :] = jnp.zeros((POOL_HALO, POOL_GROUP), F32)

    @pl.when(i == 0)
    def _():
        ext_ref[POOL_HALO:base, :] = prev_ref[...]

    @pl.when(i > 0)
    def _():
        ext_ref[POOL_HALO:base, :] = halo_ref[...]

    ext_ref[base:, :] = u_ref[...]
    pos = (past + i * tm + lax.broadcasted_iota(jnp.int32, (tm, 1), 0)).astype(F32)
    for g, w in enumerate(POOL_WINDOWS):
        sl = slice(g * POOL_GROUP, (g + 1) * POOL_GROUP)
        src, cols, nxt = ext_ref, sl, (a_ref, b_ref)
        span = 1
        while span < w:
            part = src[POOL_HALO:POOL_HALO + rows, cols] + src[POOL_HALO - span:POOL_HALO - span + rows, cols]
            span *= 2
            if span < w:
                nxt[0][POOL_HALO:POOL_HALO + rows, :] = part
                src, cols, nxt = nxt[0], slice(None), nxt[::-1]
        win = part[POOL_HALO:, :]
        cur = ext_ref[base:, sl]
        cnt = jnp.minimum(pos + 1.0, float(w))
        dlt = (win / cnt - cur).astype(BF16)
        o_ref[:, sl] = (_dot(dlt, w_ref[g]) * s_ref[:, sl]).astype(BF16)


def _pool_mix(proj, prev, w_grp, scale, past):
    b, t, _ = proj.shape
    tm = min(t, 512)
    hb = tm // POOL_HALO
    return pl.pallas_call(
        functools.partial(_pool_body, tm=tm, past=past),
        grid=(b, t // tm),
        in_specs=[
            pl.BlockSpec((None, tm, D_TOK), lambda bi, i: (bi, i, 0)),
            pl.BlockSpec((None, POOL_HALO, D_TOK), lambda bi, i: (bi, jnp.maximum(i * hb - 1, 0), 0)),
            pl.BlockSpec((None, POOL_HALO, D_TOK), lambda bi, i: (bi, 0, 0)),
            pl.BlockSpec((len(POOL_WINDOWS), POOL_GROUP, POOL_GROUP), lambda bi, i: (0, 0, 0)),
            pl.BlockSpec((1, D_TOK), lambda bi, i: (0, 0)),
        ],
        out_specs=pl.BlockSpec((None, tm, D_TOK), lambda bi, i: (bi, i, 0)),
        out_shape=jax.ShapeDtypeStruct((b, t, D_TOK), BF16),
        scratch_shapes=[pltpu.VMEM((tm + 2 * POOL_HALO, D_TOK), F32),
                        pltpu.VMEM((tm + 2 * POOL_HALO, POOL_GROUP), F32),
                        pltpu.VMEM((tm + 2 * POOL_HALO, POOL_GROUP), F32)],
        compiler_params=_params("parallel", "arbitrary"),
        name="pool_mix",
    )(proj, proj, prev, w_grp, scale.reshape(1, D_TOK))


def _scan_matrix():
    shape = (2 * SB_BLOCK, SB_BLOCK + HEAD_DIM)
    r = lax.broadcasted_iota(jnp.int32, shape, 0) % SB_BLOCK
    c = lax.broadcasted_iota(jnp.int32, shape, 1)
    return jnp.where((c >= SB_BLOCK) | (r > c), 1.0, 0.0).astype(BF16)


def _sb_steps(qs, ks, vs, runs, mask, scan):
    tq = qs[0].shape[0]
    zs = [_dot_nt(q, k) for q, k in zip(qs, ks)]
    fails, parts = [], []
    for z in zs:
        log_fail = -(jnp.maximum(z, 0.0) + jnp.log(1.0 + jnp.exp(-jnp.abs(z))))
        if mask is not None:
            log_fail = jnp.where(mask, log_fail, 0.0)
        hi = log_fail.astype(BF16)
        lo = (log_fail - hi.astype(F32)).astype(BF16)
        fails.append(log_fail)
        parts.append(jnp.concatenate([hi, lo], axis=1))
    sums = _dot(jnp.concatenate(parts, axis=0), scan)
    weights, new_runs = [], []
    for g, (z, log_fail, run) in enumerate(zip(zs, fails, runs)):
        s = sums[g * tq:(g + 1) * tq]
        a = jnp.exp(z + log_fail + s[:, :SB_BLOCK] + run)
        if mask is not None:
            a = jnp.where(mask, a, 0.0)
        weights.append(a.astype(BF16))
        new_runs.append(run + s[:, SB_BLOCK:])
    return new_runs, [_dot(a, v) for a, v in zip(weights, vs)]


def _largest(values):
    out = values[0]
    for v in values[1:]:
        out = jnp.maximum(out, v)
    return jnp.max(out)


def _sb_prompt_body(q_ref, k_ref, v_ref, o_ref, run_ref, acc_ref, scan_ref, *, t, group):
    scan_ref[...] = _scan_matrix()
    below_diag = (lax.broadcasted_iota(jnp.int32, (SB_BLOCK, SB_BLOCK), 1) <
                  lax.broadcasted_iota(jnp.int32, (SB_BLOCK, SB_BLOCK), 0))
    zero = jnp.zeros((SB_BLOCK, HEAD_DIM), F32)
    chain = [slice(g * SB_BLOCK, (g + 1) * SB_BLOCK) for g in range(group)]

    def block(i):
        return pl.ds(pl.multiple_of(i * SB_BLOCK, SB_BLOCK), SB_BLOCK)

    def group_scan(s, _):
        first = s * group
        tiles = [block(first + g) for g in range(group)]
        runs, pvs = _sb_steps([q_ref[r, :] for r in tiles], [k_ref[r, :] for r in tiles],
                              [v_ref[r, :] for r in tiles], [zero] * group, below_diag, scan_ref[...])
        for sl, run, pv in zip(chain, runs, pvs):
            run_ref[sl, :] = run
            acc_ref[sl, :] = pv

        def more(c):
            it, largest_run = c
            return jnp.logical_and(it < first + group, largest_run > EXP_ZERO_BELOW)

        def step(c):
            it, _ = c
            keys = [block(jnp.maximum(first + g - it, 0)) for g in range(group)]
            runs = [jnp.where(first + g - it >= 0, run_ref[chain[g], :], RUN_FINISHED) for g in range(group)]
            runs, pvs = _sb_steps([q_ref[r, :] for r in tiles], [k_ref[r, :] for r in keys],
                                  [v_ref[r, :] for r in keys], runs, None, scan_ref[...])
            for sl, run, pv in zip(chain, runs, pvs):
                run_ref[sl, :] = run
                acc_ref[sl, :] += pv
            return it + 1, _largest(runs)

        lax.while_loop(more, step, (jnp.int32(1), _largest(runs)))
        out_rows = pl.ds(pl.multiple_of(first * SB_BLOCK, SB_BLOCK), group * SB_BLOCK)
        o_ref[out_rows, :] = acc_ref[...].astype(o_ref.dtype)
        return 0

    lax.fori_loop(0, t // (group * SB_BLOCK), group_scan, 0)


def _sb_attn_prompt(q, k, v):
    b, _, t, _ = q.shape
    group = min(SB_GROUP, t // SB_BLOCK)
    assert t % (group * SB_BLOCK) == 0, t
    spec = pl.BlockSpec((None, None, t, HEAD_DIM), lambda bi, h: (bi, h, 0, 0))
    return pl.pallas_call(
        functools.partial(_sb_prompt_body, t=t, group=group),
        grid=(b, SB_HEADS),
        in_specs=[spec, spec, spec],
        out_specs=pl.BlockSpec((None, t, HEAD_DIM), lambda bi, h: (bi, 0, h)),
        out_shape=jax.ShapeDtypeStruct((b, t, D_TOK), BF16),
        scratch_shapes=[pltpu.VMEM((group * SB_BLOCK, HEAD_DIM), F32),
                        pltpu.VMEM((group * SB_BLOCK, HEAD_DIM), F32),
                        pltpu.VMEM((2 * SB_BLOCK, SB_BLOCK + HEAD_DIM), BF16)],
        compiler_params=_params("parallel", "parallel"),
        name="sb_attn_prompt",
    )(q, k, v)


def _sb_cached_body(q_ref, kn_ref, vn_ref, kc_ref, vc_ref, o_ref, run_ref, acc_ref, scan_ref, *, t, past):
    scan_ref[...] = _scan_matrix()
    earlier = (lax.broadcasted_iota(jnp.int32, (t, SB_BLOCK), 1) <
               lax.broadcasted_iota(jnp.int32, (t, SB_BLOCK), 0))
    zero = jnp.zeros((t, HEAD_DIM), F32)
    pad = jnp.zeros((SB_BLOCK - t, HEAD_DIM), BF16)
    heads = range(SB_HEADS)
    chain = [slice(h * t, (h + 1) * t) for h in heads]

    runs, pvs = _sb_steps([q_ref[h] for h in heads],
                          [jnp.concatenate([kn_ref[h], pad], axis=0) for h in heads],
                          [jnp.concatenate([vn_ref[h], pad], axis=0) for h in heads],
                          [zero] * SB_HEADS, earlier, scan_ref[...])
    for sl, run, pv in zip(chain, runs, pvs):
        run_ref[sl, :] = run
        acc_ref[sl, :] = pv

    def more(c):
        it, largest_run = c
        return jnp.logical_and(it < past // SB_BLOCK, largest_run > EXP_ZERO_BELOW)

    def step(c):
        it, _ = c
        keys = pl.ds(pl.multiple_of(past - (it + 1) * SB_BLOCK, SB_BLOCK), SB_BLOCK)
        runs, pvs = _sb_steps([q_ref[h] for h in heads], [kc_ref[h, keys, :].astype(BF16) for h in heads],
                              [vc_ref[h, keys, :].astype(BF16) for h in heads], [run_ref[sl, :] for sl in chain],
                              None, scan_ref[...])
        for sl, run, pv in zip(chain, runs, pvs):
            run_ref[sl, :] = run
            acc_ref[sl, :] += pv
        return it + 1, _largest(runs)

    lax.while_loop(more, step, (jnp.int32(0), _largest(runs)))
    for h, sl in zip(heads, chain):
        o_ref[:, h * HEAD_DIM:(h + 1) * HEAD_DIM] = acc_ref[sl, :].astype(o_ref.dtype)


def _sb_attn_cached(q, k_new, v_new, k_past, v_past):
    b, _, t, _ = q.shape
    past = k_past.shape[2]
    assert past % SB_BLOCK == 0 and t <= SB_BLOCK and t % 16 == 0, (past, t)
    new = pl.BlockSpec((None, SB_HEADS, t, HEAD_DIM), lambda bi: (bi, 0, 0, 0))
    old = pl.BlockSpec((None, SB_HEADS, past, HEAD_DIM), lambda bi: (bi, 0, 0, 0))
    return pl.pallas_call(
        functools.partial(_sb_cached_body, t=t, past=past),
        grid=(b,),
        in_specs=[new, new, new, old, old],
        out_specs=pl.BlockSpec((None, t, D_TOK), lambda bi: (bi, 0, 0)),
        out_shape=jax.ShapeDtypeStruct((b, t, D_TOK), BF16),
        scratch_shapes=[pltpu.VMEM((SB_HEADS * t, HEAD_DIM), F32),
                        pltpu.VMEM((SB_HEADS * t, HEAD_DIM), F32),
                        pltpu.VMEM((2 * SB_BLOCK, SB_BLOCK + HEAD_DIM), BF16)],
        compiler_params=_params("parallel"),
        name="sb_attn_cached",
    )(q, k_new, v_new, k_past, v_past)


def _mix_out_body(x_ref, mix_ref, qm_ref, mk_ref, mv_ref, qg_ref, w_ref, o_ref):
    bb = mk_ref.shape[0]
    tt = x_ref.shape[0] // bb
    probs = []
    for bi in range(bb):
        rows = slice(bi * tt, (bi + 1) * tt)
        for hd in range(MEM_HEADS):
            sl = slice(hd * HEAD_DIM, (hd + 1) * HEAD_DIM)
            qh = _rms(qm_ref[rows, sl], qg_ref[...]).astype(BF16)
            s = _dot_nt(qh, mk_ref[bi, :, sl].astype(BF16)) * INV_SQRT_HD
            p = jnp.exp(s - jnp.max(s, axis=-1, keepdims=True))
            probs.append((p / jnp.sum(p, axis=-1, keepdims=True)).astype(BF16))
    y = x_ref[...] + _dot(mix_ref[...], w_ref[:D_TOK, :])
    mo = jnp.concatenate(
        [jnp.concatenate([_dot(probs[bi * MEM_HEADS + hd],
                               mv_ref[bi, :, hd * HEAD_DIM:(hd + 1) * HEAD_DIM].astype(BF16))
                          for hd in range(MEM_HEADS)], axis=1) for bi in range(bb)], axis=0)
    o_ref[...] = y + _dot(mo.astype(BF16), w_ref[D_TOK:, :])


def _mix_out(x, mix, qm_src, qm_col, mem_k, mem_v, q_gain, w_out, layer, t):
    n, d = x.shape
    n_mem = mem_k.shape[2]
    tm = min(n, ROW_TILE)
    if t >= tm:
        per_seq = t // tm
        mem = pl.BlockSpec((None, 1, n_mem, D_MEM), lambda i: (layer, i // per_seq, 0, 0))
    else:
        mem = pl.BlockSpec((None, tm // t, n_mem, D_MEM), lambda i: (layer, i, 0, 0),
                           pipeline_mode=pl.Buffered(1))
    row = lambda c, col=0: pl.BlockSpec((tm, c), lambda i: (i, col))
    return pl.pallas_call(
        _mix_out_body,
        grid=(n // tm,),
        in_specs=[row(d), row(D_TOK), row(D_MEM, qm_col), mem, mem,
                  pl.BlockSpec((1, HEAD_DIM), lambda i: (0, 0)), _resident((d, d), layer)],
        out_specs=row(d),
        out_shape=jax.ShapeDtypeStruct((n, d), F32),
        compiler_params=_params("parallel"),
        name="mix_out",
    )(x, mix, qm_src, mem_k, mem_v, q_gain.reshape(1, HEAD_DIM), w_out)


def _ffn_body(x_ref, g_ref, wu_ref, wd_ref, o_ref, *rest):
    h_ref = rest[-1]

    @pl.when(pl.program_id(1) == 0)
    def _():
        x = x_ref[...]
        h_ref[...] = _rms(x, g_ref[...]).astype(BF16)
        o_ref[...] = x

    wu = wu_ref[...].astype(BF16)
    wd = wd_ref[...].astype(BF16)
    if len(rest) == 3:
        rest[0][...] = wu
        rest[1][...] = wd
    up = jnp.maximum(_dot(h_ref[...], wu), 0.0)
    o_ref[...] += _dot((up * up).astype(BF16), wd)


def _ffn(x, g, w_up, w_down):
    n, d = x.shape
    f = w_up.shape[1]
    tm = min(n, ROW_TILE)
    tk = 1024
    return pl.pallas_call(
        _ffn_body,
        grid=(n // tm, f // tk),
        in_specs=[
            pl.BlockSpec((tm, d), lambda i, k: (i, 0)),
            pl.BlockSpec((1, d), lambda i, k: (0, 0)),
            pl.BlockSpec((d, tk), lambda i, k: (0, k)),
            pl.BlockSpec((tk, d), lambda i, k: (k, 0)),
        ],
        out_specs=pl.BlockSpec((tm, d), lambda i, k: (i, 0)),
        out_shape=jax.ShapeDtypeStruct((n, d), F32),
        scratch_shapes=[pltpu.VMEM((tm, d), BF16)],
        compiler_params=_params("parallel", "arbitrary"),
        name="ffn",
    )(x, g.reshape(1, d), w_up, w_down)


def _ffn_casting(x, g, w_up, w_down, layer):
    n, d = x.shape
    f = w_up.shape[2]
    tk = 512
    assert n <= ROW_TILE, n
    return pl.pallas_call(
        _ffn_body,
        grid=(1, f // tk),
        in_specs=[
            pl.BlockSpec((n, d), lambda i, k: (0, 0)),
            pl.BlockSpec((1, d), lambda i, k: (0, 0)),
            pl.BlockSpec((None, d, tk), lambda i, k: (layer, 0, k)),
            pl.BlockSpec((None, tk, d), lambda i, k: (layer, k, 0)),
        ],
        out_specs=[pl.BlockSpec((n, d), lambda i, k: (0, 0)),
                   pl.BlockSpec((d, tk), lambda i, k: (0, k)),
                   pl.BlockSpec((tk, d), lambda i, k: (k, 0))],
        out_shape=[jax.ShapeDtypeStruct((n, d), F32), jax.ShapeDtypeStruct((d, f), BF16),
                   jax.ShapeDtypeStruct((f, d), BF16)],
        scratch_shapes=[pltpu.VMEM((n, d), BF16)],
        compiler_params=_params("arbitrary", "arbitrary"),
        name="ffn_casting",
    )(x, g.reshape(1, d), w_up, w_down)


def _run_trunk(x, past, pool_prev, sb_past_k, sb_past_v, mem_k, mem_v, p, ffn_bf16):
    b, t, d = x.shape
    n = b * t
    depth = p["w_out"].shape[0]
    pool_new, sbk_new, sbv_new, made = [], [], [], []
    x2 = x.reshape(n, d)
    for i in range(depth):
        j = i // 2
        if i % 2 == 0:
            proj = _norm_matmul(x2, p["norm_mix"][i], p["w_in_pool"], j)
            proj3 = proj.reshape(b, t, d)
            mix = _pool_mix(proj3, pool_prev[j], p["pool_w"][j], p["pool_scale"][j], past)
            pool_new.append(proj3[:, t - POOL_STATE:, :D_TOK])
            qm_src, qm_col = proj, D_TOK // D_MEM
        else:
            q, k, v, kb, vb, qm = _inproj_sb(x2, p["norm_mix"][i], p["w_in_sb"], j, b, t)
            if sb_past_k is None:
                mix = _sb_attn_prompt(q, kb, vb)
            else:
                mix = _sb_attn_cached(q, kb, vb, sb_past_k[j], sb_past_v[j])
            sbk_new.append(k)
            sbv_new.append(v)
            qm_src, qm_col = qm, 0
        x2 = _mix_out(x2, mix.reshape(n, D_TOK), qm_src, qm_col, mem_k, mem_v, p["q_norm"][i], p["w_out"], i, t)
        if ffn_bf16 is None:
            x2, wu, wd = _ffn_casting(x2, p["norm_ffn"][i], p["w_up"], p["w_down"], i)
            made.append((wu, wd))
        else:
            x2 = _ffn(x2, p["norm_ffn"][i], *ffn_bf16[i])
    return x2.reshape(b, t, d), jnp.stack(pool_new), jnp.stack(sbk_new), jnp.stack(sbv_new), made


def kernel(x_prompt, x_sample, mem_prompt, cache_pool, cache_sb_k, cache_sb_v, cache_mem_k, cache_mem_v,
           norm_mix, w_in_pool, w_in_sb, pool_w, pool_scale, norm_mem, w_mem_kv, q_norm, k_norm, w_out,
           norm_ffn, w_up, w_down):
    b, t, d = x_prompt.shape
    depth = w_out.shape[0]
    n_mem = mem_prompt.shape[1]
    p = dict(norm_mix=norm_mix, pool_scale=pool_scale, q_norm=q_norm, norm_ffn=norm_ffn,
             w_in_pool=w_in_pool.astype(BF16), w_in_sb=w_in_sb.astype(BF16), pool_w=pool_w.astype(BF16),
             w_out=w_out.astype(BF16), w_up=w_up, w_down=w_down)

    bs = x_sample.shape[0]
    past = cache_sb_k.shape[2]
    pool_prev = jnp.pad(cache_pool, ((0, 0), (0, 0), (POOL_HALO - POOL_STATE, 0), (0, 0)))
    head_major = lambda a: jnp.transpose(a, (0, 1, 3, 2, 4))
    y_sample, pool_sample, sb_k_sample, sb_v_sample, ffn_bf16 = _run_trunk(
        x_sample, past, pool_prev, head_major(cache_sb_k), head_major(cache_sb_v),
        cache_mem_k.reshape(depth, bs, n_mem, D_MEM), cache_mem_v.reshape(depth, bs, n_mem, D_MEM), p, None)

    mk, mv = _mem_kv(mem_prompt.reshape(b * n_mem, d), norm_mem, w_mem_kv.astype(BF16), k_norm)
    mk = mk.reshape(depth, b, n_mem, D_MEM)
    mv = mv.reshape(depth, b, n_mem, D_MEM)
    pool_zero = jnp.zeros((cache_pool.shape[0], b, POOL_HALO, D_TOK), F32)
    y_prompt, pool_prompt, sb_k_prompt, sb_v_prompt, _ = _run_trunk(
        x_prompt, 0, pool_zero, None, None, mk, mv, p, ffn_bf16)

    heads = lambda a: a.reshape(depth, b, n_mem, MEM_HEADS, HEAD_DIM)
    return (y_prompt, y_sample, pool_prompt, pool_sample, head_major(sb_k_prompt), head_major(sb_v_prompt),
            head_major(sb_k_sample), head_major(sb_v_sample), heads(mk), heads(mv))
```

```python
import functools
import math

import jax
import jax.numpy as jnp
from jax import lax
from jax.experimental import pallas as pl
from jax.experimental.pallas import tpu as pltpu

D_MODEL = 2048
HEAD_DIM = 128
D_TOK = 3 * D_MODEL // 4
D_MEM = D_MODEL // 4
SB_HEADS = D_TOK // HEAD_DIM
MEM_HEADS = D_MEM // HEAD_DIM
POOL_WINDOWS = (2, 4, 8, 16)
POOL_GROUP = D_TOK // len(POOL_WINDOWS)
POOL_STATE = max(POOL_WINDOWS) - 1
POOL_HALO = POOL_STATE + 1
D_FF = 4 * D_MODEL
EPS = 1e-6
INV_SQRT_HD = 1.0 / math.sqrt(HEAD_DIM)

SB_BLOCK = 128
SB_GROUP = 32
EXP_ZERO_BELOW = -104.0
RUN_FINISHED = -1e30
ROW_TILE = 512
VMEM_LIMIT = 48 * 1024 * 1024

BF16 = jnp.bfloat16
F32 = jnp.float32


def _params(*sem):
    return pltpu.CompilerParams(dimension_semantics=sem, vmem_limit_bytes=VMEM_LIMIT)


def _rms(xf, g):
    return xf * lax.rsqrt(jnp.mean(xf * xf, axis=-1, keepdims=True) + EPS) * g


def _dot(a, b):
    return jnp.dot(a, b, preferred_element_type=F32)


def _dot_nt(a, b):
    return lax.dot_general(a, b, (((1,), (1,)), ((), ())), preferred_element_type=F32)


def _resident(shape, layer):
    return pl.BlockSpec((None,) + shape, lambda *_: (layer,) + (0,) * len(shape), pipeline_mode=pl.Buffered(1))


def _norm_matmul_body(x_ref, g_ref, w_ref, o_ref):
    o_ref[...] = _dot(_rms(x_ref[...], g_ref[...]).astype(BF16), w_ref[...])


def _norm_matmul(x, g, w, layer):
    n, d = x.shape
    m = w.shape[2]
    tm = min(n, ROW_TILE)
    return pl.pallas_call(
        _norm_matmul_body,
        grid=(n // tm,),
        in_specs=[
            pl.BlockSpec((tm, d), lambda i: (i, 0)),
            pl.BlockSpec((1, d), lambda i: (0, 0)),
            _resident((d, m), layer),
        ],
        out_specs=pl.BlockSpec((tm, m), lambda i: (i, 0)),
        out_shape=jax.ShapeDtypeStruct((n, m), F32),
        compiler_params=_params("parallel"),
        name="norm_matmul",
    )(x, g.reshape(1, d), w)


def _inproj_sb_body(x_ref, g_ref, wq_ref, wk_ref, wv_ref, wm_ref,
                    q_ref, k_ref, v_ref, kb_ref, vb_ref, qm_ref, h0_ref, h1_ref):
    h_refs = (h0_ref, h1_ref)
    i, j = pl.program_id(0), pl.program_id(1)
    last = pl.num_programs(1) - 1

    def put(refs, val):
        bb, heads, tt, _ = refs[0].shape
        for bi in range(bb):
            for hd in range(heads):
                piece = val[bi * tt:(bi + 1) * tt, hd * HEAD_DIM:(hd + 1) * HEAD_DIM]
                for ref in refs:
                    ref[bi, hd] = piece.astype(ref.dtype)

    def project(h_ref):
        h = h_ref[...]
        put([q_ref], _dot(h, wq_ref[...]) * INV_SQRT_HD)
        put([k_ref, kb_ref], _dot(h, wk_ref[...]))
        put([v_ref, vb_ref], _dot(h, wv_ref[...]))

    @pl.when(jnp.logical_and(i == 0, j == 0))
    def _():
        h0_ref[...] = _rms(x_ref[...], g_ref[...]).astype(BF16)

    for parity in range(2):
        mine = i % 2 == parity

        @pl.when(jnp.logical_and(mine, j == 0))
        def _():
            qm_ref[...] = _dot(h_refs[parity][...], wm_ref[...])
            project(h_refs[parity])

        @pl.when(jnp.logical_and(mine, jnp.logical_and(j > 0, j < last)))
        def _():
            project(h_refs[parity])

        @pl.when(jnp.logical_and(mine, j == last))
        def _():
            h_refs[1 - parity][...] = _rms(x_ref[...], g_ref[...]).astype(BF16)
            project(h_refs[parity])


def _inproj_sb(x, g, w, layer, b, t):
    n, d = x.shape
    tm = min(n, 2 * ROW_TILE)
    tn = 2 * HEAD_DIM
    nj = D_TOK // tn
    hj = tn // HEAD_DIM
    if t >= tm:
        per_seq = t // tm
        oblock, omap = (1, hj, tm, HEAD_DIM), lambda i, j: (i // per_seq, j, i % per_seq, 0)
    else:
        oblock, omap = (tm // t, hj, t, HEAD_DIM), lambda i, j: (i, j, 0, 0)
    tok = lambda dt: jax.ShapeDtypeStruct((b, SB_HEADS, t, HEAD_DIM), dt)
    wspec = lambda off: pl.BlockSpec((None, d, tn), lambda i, j: (layer, 0, off + j))
    ospec = pl.BlockSpec(oblock, omap)
    return pl.pallas_call(
        _inproj_sb_body,
        grid=(n // tm, nj),
        in_specs=[
            pl.BlockSpec((tm, d), lambda i, j: (jnp.minimum(jnp.where(j >= nj // 2, i + 1, i), n // tm - 1), 0)),
            pl.BlockSpec((1, d), lambda i, j: (0, 0)),
            wspec(0), wspec(nj), wspec(2 * nj),
            pl.BlockSpec((None, d, D_MEM), lambda i, j: (layer, 0, 3 * D_TOK // D_MEM),
                         pipeline_mode=pl.Buffered(1)),
        ],
        out_specs=[ospec, ospec, ospec, ospec, ospec,
                   pl.BlockSpec((tm, D_MEM), lambda i, j: (i, 0))],
        out_shape=[tok(BF16), tok(F32), tok(F32), tok(BF16), tok(BF16),
                   jax.ShapeDtypeStruct((n, D_MEM), F32)],
        scratch_shapes=[pltpu.VMEM((tm, d), BF16), pltpu.VMEM((tm, d), BF16)],
        compiler_params=_params("arbitrary", "arbitrary"),
        name="inproj_sb",
    )(x, g.reshape(1, d), w, w, w, w)


def _mem_kv_body(x_ref, g_ref, w_ref, kg_ref, k_ref, v_ref):
    h = _rms(x_ref[...], g_ref[...]).astype(BF16)
    kv = _dot(h, w_ref[...])
    for hd in range(MEM_HEADS):
        sl = slice(hd * HEAD_DIM, (hd + 1) * HEAD_DIM)
        k_ref[:, sl] = _rms(kv[:, sl], kg_ref[...])
    v_ref[...] = kv[:, D_MEM:]


def _mem_kv(mem, g_mem, w_kv, k_gain):
    r, d = mem.shape
    depth = w_kv.shape[0]
    tm = min(r, 256)
    out = jax.ShapeDtypeStruct((depth, r, D_MEM), F32)
    ospec = pl.BlockSpec((None, tm, D_MEM), lambda l, i: (l, i, 0))
    return pl.pallas_call(
        _mem_kv_body,
        grid=(depth, r // tm),
        in_specs=[
            pl.BlockSpec((tm, d), lambda l, i: (i, 0)),
            pl.BlockSpec((None, 1, d), lambda l, i: (l, 0, 0)),
            pl.BlockSpec((None, d, 2 * D_MEM), lambda l, i: (l, 0, 0)),
            pl.BlockSpec((None, 1, HEAD_DIM), lambda l, i: (l, 0, 0)),
        ],
        out_specs=[ospec, ospec],
        out_shape=[out, out],
        compiler_params=_params("parallel", "parallel"),
        name="mem_kv",
    )(mem, g_mem.reshape(depth, 1, d), w_kv, k_gain.reshape(depth, 1, HEAD_DIM))


def _pool_body(u_ref, halo_ref, prev_ref, w_ref, s_ref, o_ref, ext_ref, a_ref, b_ref, *, tm, past):
    i = pl.program_id(1)
    base = 2 * POOL_HALO
    rows = tm + POOL_HALO
    ext_ref[0:POOL_HALO, :] = jnp.zeros((POOL_HALO, D_TOK), F32)
    a_ref[0:POOL_HALO, :] = jnp.zeros((POOL_HALO, POOL_GROUP), F32)
    b_ref[0:POOL_HALO, :] = jnp.zeros((POOL_HALO, POOL_GROUP), F32)

    @pl.when(i == 0)
    def _():
        ext_ref[POOL_HALO:base, :] = prev_ref[...]

    @pl.when(i > 0)
    def _():
        ext_ref[POOL_HALO:base, :] = halo_ref[...]

    ext_ref[base:, :] = u_ref[...]
    pos = (past + i * tm + lax.broadcasted_iota(jnp.int32, (tm, 1), 0)).astype(F32)
    for g, w in enumerate(POOL_WINDOWS):
        sl = slice(g * POOL_GROUP, (g + 1) * POOL_GROUP)
        src, cols, nxt = ext_ref, sl, (a_ref, b_ref)
        span = 1
        while span < w:
            part = src[POOL_HALO:POOL_HALO + rows, cols] + src[POOL_HALO - span:POOL_HALO - span + rows, cols]
            span *= 2
            if span < w:
                nxt[0][POOL_HALO:POOL_HALO + rows, :] = part
                src, cols, nxt = nxt[0], slice(None), nxt[::-1]
        win = part[POOL_HALO:, :]
        cur = ext_ref[base:, sl]
        cnt = jnp.minimum(pos + 1.0, float(w))
        dlt = (win / cnt - cur).astype(BF16)
        o_ref[:, sl] = (_dot(dlt, w_ref[g]) * s_ref[:, sl]).astype(BF16)


def _pool_mix(proj, prev, w_grp, scale, past):
    b, t, _ = proj.shape
    tm = min(t, 512)
    hb = tm // POOL_HALO
    return pl.pallas_call(
        functools.partial(_pool_body, tm=tm, past=past),
        grid=(b, t // tm),
        in_specs=[
            pl.BlockSpec((None, tm, D_TOK), lambda bi, i: (bi, i, 0)),
            pl.BlockSpec((None, POOL_HALO, D_TOK), lambda bi, i: (bi, jnp.maximum(i * hb - 1, 0), 0)),
            pl.BlockSpec((None, POOL_HALO, D_TOK), lambda bi, i: (bi, 0, 0)),
            pl.BlockSpec((len(POOL_WINDOWS), POOL_GROUP, POOL_GROUP), lambda bi, i: (0, 0, 0)),
            pl.BlockSpec((1, D_TOK), lambda bi, i: (0, 0)),
        ],
        out_specs=pl.BlockSpec((None, tm, D_TOK), lambda bi, i: (bi, i, 0)),
        out_shape=jax.ShapeDtypeStruct((b, t, D_TOK), BF16),
        scratch_shapes=[pltpu.VMEM((tm + 2 * POOL_HALO, D_TOK), F32),
                        pltpu.VMEM((tm + 2 * POOL_HALO, POOL_GROUP), F32),
                        pltpu.VMEM((tm + 2 * POOL_HALO, POOL_GROUP), F32)],
        compiler_params=_params("parallel", "arbitrary"),
        name="pool_mix",
    )(proj, proj, prev, w_grp, scale.reshape(1, D_TOK))


def _scan_matrix():
    shape = (2 * SB_BLOCK, SB_BLOCK + HEAD_DIM)
    r = lax.broadcasted_iota(jnp.int32, shape, 0) % SB_BLOCK
    c = lax.broadcasted_iota(jnp.int32, shape, 1)
    return jnp.where((c >= SB_BLOCK) | (r > c), 1.0, 0.0).astype(BF16)


def _sb_steps(qs, ks, vs, runs, mask, scan):
    tq = qs[0].shape[0]
    zs = [_dot_nt(q, k) for q, k in zip(qs, ks)]
    fails, parts = [], []
    for z in zs:
        log_fail = -(jnp.maximum(z, 0.0) + jnp.log(1.0 + jnp.exp(-jnp.abs(z))))
        if mask is not None:
            log_fail = jnp.where(mask, log_fail, 0.0)
        hi = log_fail.astype(BF16)
        lo = (log_fail - hi.astype(F32)).astype(BF16)
        fails.append(log_fail)
        parts.append(jnp.concatenate([hi, lo], axis=1))
    sums = _dot(jnp.concatenate(parts, axis=0), scan)
    weights, new_runs = [], []
    for g, (z, log_fail, run) in enumerate(zip(zs, fails, runs)):
        s = sums[g * tq:(g + 1) * tq]
        a = jnp.exp(z + log_fail + s[:, :SB_BLOCK] + run)
        if mask is not None:
            a = jnp.where(mask, a, 0.0)
        weights.append(a.astype(BF16))
        new_runs.append(run + s[:, SB_BLOCK:])
    return new_runs, [_dot(a, v) for a, v in zip(weights, vs)]


def _largest(values):
    out = values[0]
    for v in values[1:]:
        out = jnp.maximum(out, v)
    return jnp.max(out)


def _sb_prompt_body(q_ref, k_ref, v_ref, o_ref, run_ref, acc_ref, scan_ref, *, t, group):
    scan_ref[...] = _scan_matrix()
    below_diag = (lax.broadcasted_iota(jnp.int32, (SB_BLOCK, SB_BLOCK), 1) <
                  lax.broadcasted_iota(jnp.int32, (SB_BLOCK, SB_BLOCK), 0))
    zero = jnp.zeros((SB_BLOCK, HEAD_DIM), F32)
    chain = [slice(g * SB_BLOCK, (g + 1) * SB_BLOCK) for g in range(group)]

    def block(i):
        return pl.ds(pl.multiple_of(i * SB_BLOCK, SB_BLOCK), SB_BLOCK)

    def group_scan(s, _):
        first = s * group
        tiles = [block(first + g) for g in range(group)]
        runs, pvs = _sb_steps([q_ref[r, :] for r in tiles], [k_ref[r, :] for r in tiles],
                              [v_ref[r, :] for r in tiles], [zero] * group, below_diag, scan_ref[...])
        for sl, run, pv in zip(chain, runs, pvs):
            run_ref[sl, :] = run
            acc_ref[sl, :] = pv

        def more(c):
            it, largest_run = c
            return jnp.logical_and(it < first + group, largest_run > EXP_ZERO_BELOW)

        def step(c):
            it, _ = c
            keys = [block(jnp.maximum(first + g - it, 0)) for g in range(group)]
            runs = [jnp.where(first + g - it >= 0, run_ref[chain[g], :], RUN_FINISHED) for g in range(group)]
            runs, pvs = _sb_steps([q_ref[r, :] for r in tiles], [k_ref[r, :] for r in keys],
                                  [v_ref[r, :] for r in keys], runs, None, scan_ref[...])
            for sl, run, pv in zip(chain, runs, pvs):
                run_ref[sl, :] = run
                acc_ref[sl, :] += pv
            return it + 1, _largest(runs)

        lax.while_loop(more, step, (jnp.int32(1), _largest(runs)))
        out_rows = pl.ds(pl.multiple_of(first * SB_BLOCK, SB_BLOCK), group * SB_BLOCK)
        o_ref[out_rows, :] = acc_ref[...].astype(o_ref.dtype)
        return 0

    lax.fori_loop(0, t // (group * SB_BLOCK), group_scan, 0)


def _sb_attn_prompt(q, k, v):
    b, _, t, _ = q.shape
    group = min(SB_GROUP, t // SB_BLOCK)
    assert t % (group * SB_BLOCK) == 0, t
    spec = pl.BlockSpec((None, None, t, HEAD_DIM), lambda bi, h: (bi, h, 0, 0))
    return pl.pallas_call(
        functools.partial(_sb_prompt_body, t=t, group=group),
        grid=(b, SB_HEADS),
        in_specs=[spec, spec, spec],
        out_specs=pl.BlockSpec((None, t, HEAD_DIM), lambda bi, h: (bi, 0, h)),
        out_shape=jax.ShapeDtypeStruct((b, t, D_TOK), BF16),
        scratch_shapes=[pltpu.VMEM((group * SB_BLOCK, HEAD_DIM), F32),
                        pltpu.VMEM((group * SB_BLOCK, HEAD_DIM), F32),
                        pltpu.VMEM((2 * SB_BLOCK, SB_BLOCK + HEAD_DIM), BF16)],
        compiler_params=_params("parallel", "parallel"),
        name="sb_attn_prompt",
    )(q, k, v)


def _sb_cached_body(q_ref, kn_ref, vn_ref, kc_ref, vc_ref, o_ref, run_ref, acc_ref, scan_ref, *, t, past):
    scan_ref[...] = _scan_matrix()
    earlier = (lax.broadcasted_iota(jnp.int32, (t, SB_BLOCK), 1) <
               lax.broadcasted_iota(jnp.int32, (t, SB_BLOCK), 0))
    zero = jnp.zeros((t, HEAD_DIM), F32)
    pad = jnp.zeros((SB_BLOCK - t, HEAD_DIM), BF16)
    heads = range(SB_HEADS)
    chain = [slice(h * t, (h + 1) * t) for h in heads]

    runs, pvs = _sb_steps([q_ref[h] for h in heads],
                          [jnp.concatenate([kn_ref[h], pad], axis=0) for h in heads],
                          [jnp.concatenate([vn_ref[h], pad], axis=0) for h in heads],
                          [zero] * SB_HEADS, earlier, scan_ref[...])
    for sl, run, pv in zip(chain, runs, pvs):
        run_ref[sl, :] = run
        acc_ref[sl, :] = pv

    def more(c):
        it, largest_run = c
        return jnp.logical_and(it < past // SB_BLOCK, largest_run > EXP_ZERO_BELOW)

    def step(c):
        it, _ = c
        keys = pl.ds(pl.multiple_of(past - (it + 1) * SB_BLOCK, SB_BLOCK), SB_BLOCK)
        runs, pvs = _sb_steps([q_ref[h] for h in heads], [kc_ref[h, keys, :].astype(BF16) for h in heads],
                              [vc_ref[h, keys, :].astype(BF16) for h in heads], [run_ref[sl, :] for sl in chain],
                              None, scan_ref[...])
        for sl, run, pv in zip(chain, runs, pvs):
            run_ref[sl, :] = run
            acc_ref[sl, :] += pv
        return it + 1, _largest(runs)

    lax.while_loop(more, step, (jnp.int32(0), _largest(runs)))
    for h, sl in zip(heads, chain):
        o_ref[:, h * HEAD_DIM:(h + 1) * HEAD_DIM] = acc_ref[sl, :].astype(o_ref.dtype)


def _sb_attn_cached(q, k_new, v_new, k_past, v_past):
    b, _, t, _ = q.shape
    past = k_past.shape[2]
    assert past % SB_BLOCK == 0 and t <= SB_BLOCK and t % 16 == 0, (past, t)
    new = pl.BlockSpec((None, SB_HEADS, t, HEAD_DIM), lambda bi: (bi, 0, 0, 0))
    old = pl.BlockSpec((None, SB_HEADS, past, HEAD_DIM), lambda bi: (bi, 0, 0, 0))
    return pl.pallas_call(
        functools.partial(_sb_cached_body, t=t, past=past),
        grid=(b,),
        in_specs=[new, new, new, old, old],
        out_specs=pl.BlockSpec((None, t, D_TOK), lambda bi: (bi, 0, 0)),
        out_shape=jax.ShapeDtypeStruct((b, t, D_TOK), BF16),
        scratch_shapes=[pltpu.VMEM((SB_HEADS * t, HEAD_DIM), F32),
                        pltpu.VMEM((SB_HEADS * t, HEAD_DIM), F32),
                        pltpu.VMEM((2 * SB_BLOCK, SB_BLOCK + HEAD_DIM), BF16)],
        compiler_params=_params("parallel"),
        name="sb_attn_cached",
    )(q, k_new, v_new, k_past, v_past)


def _mix_out_body(x_ref, mix_ref, qm_ref, mk_ref, mv_ref, qg_ref, w_ref, o_ref):
    bb = mk_ref.shape[0]
    tt = x_ref.shape[0] // bb
    probs = []
    for bi in range(bb):
        rows = slice(bi * tt, (bi + 1) * tt)
        for hd in range(MEM_HEADS):
            sl = slice(hd * HEAD_DIM, (hd + 1) * HEAD_DIM)
            qh = _rms(qm_ref[rows, sl], qg_ref[...]).astype(BF16)
            s = _dot_nt(qh, mk_ref[bi, :, sl].astype(BF16)) * INV_SQRT_HD
            p = jnp.exp(s - jnp.max(s, axis=-1, keepdims=True))
            probs.append((p / jnp.sum(p, axis=-1, keepdims=True)).astype(BF16))
    y = x_ref[...] + _dot(mix_ref[...], w_ref[:D_TOK, :])
    mo = jnp.concatenate(
        [jnp.concatenate([_dot(probs[bi * MEM_HEADS + hd],
                               mv_ref[bi, :, hd * HEAD_DIM:(hd + 1) * HEAD_DIM].astype(BF16))
                          for hd in range(MEM_HEADS)], axis=1) for bi in range(bb)], axis=0)
    o_ref[...] = y + _dot(mo.astype(BF16), w_ref[D_TOK:, :])


def _mix_out(x, mix, qm_src, qm_col, mem_k, mem_v, q_gain, w_out, layer, t):
    n, d = x.shape
    n_mem = mem_k.shape[2]
    tm = min(n, ROW_TILE)
    if t >= tm:
        per_seq = t // tm
        mem = pl.BlockSpec((None, 1, n_mem, D_MEM), lambda i: (layer, i // per_seq, 0, 0))
    else:
        mem = pl.BlockSpec((None, tm // t, n_mem, D_MEM), lambda i: (layer, i, 0, 0),
                           pipeline_mode=pl.Buffered(1))
    row = lambda c, col=0: pl.BlockSpec((tm, c), lambda i: (i, col))
    return pl.pallas_call(
        _mix_out_body,
        grid=(n // tm,),
        in_specs=[row(d), row(D_TOK), row(D_MEM, qm_col), mem, mem,
                  pl.BlockSpec((1, HEAD_DIM), lambda i: (0, 0)), _resident((d, d), layer)],
        out_specs=row(d),
        out_shape=jax.ShapeDtypeStruct((n, d), F32),
        compiler_params=_params("parallel"),
        name="mix_out",
    )(x, mix, qm_src, mem_k, mem_v, q_gain.reshape(1, HEAD_DIM), w_out)


def _ffn_body(x_ref, g_ref, wu_ref, wd_ref, o_ref, *rest):
    h_refs = rest[-2:]
    i, k = pl.program_id(0), pl.program_id(1)
    last = pl.num_programs(1) - 1

    def mlp_step(h_ref):
        wu = wu_ref[...].astype(BF16)
        wd = wd_ref[...].astype(BF16)
        if len(rest) == 4:
            rest[0][...] = wu
            rest[1][...] = wd
        up = jnp.maximum(_dot(h_ref[...], wu), 0.0)
        o_ref[...] += _dot((up * up).astype(BF16), wd)

    @pl.when(jnp.logical_and(i == 0, k == 0))
    def _():
        h_refs[0][...] = _rms(x_ref[...], g_ref[...]).astype(BF16)

    @pl.when(k == 0)
    def _():
        o_ref[...] = x_ref[...]

    for parity in range(2):
        mine = i % 2 == parity

        @pl.when(jnp.logical_and(mine, k < last))
        def _():
            mlp_step(h_refs[parity])

        @pl.when(jnp.logical_and(mine, k == last))
        def _():
            h_refs[1 - parity][...] = _rms(x_ref[...], g_ref[...]).astype(BF16)
            mlp_step(h_refs[parity])


def _next_tile_at_last(n_tiles, n_steps):
    return lambda i, k: (jnp.minimum(jnp.where(k == n_steps - 1, i + 1, i), n_tiles - 1), 0)


def _ffn(x, g, w_up, w_down):
    n, d = x.shape
    f = w_up.shape[1]
    tm = min(n, ROW_TILE)
    tk = 1024
    return pl.pallas_call(
        _ffn_body,
        grid=(n // tm, f // tk),
        in_specs=[
            pl.BlockSpec((tm, d), _next_tile_at_last(n // tm, f // tk)),
            pl.BlockSpec((1, d), lambda i, k: (0, 0)),
            pl.BlockSpec((d, tk), lambda i, k: (0, k)),
            pl.BlockSpec((tk, d), lambda i, k: (k, 0)),
        ],
        out_specs=pl.BlockSpec((tm, d), lambda i, k: (i, 0)),
        out_shape=jax.ShapeDtypeStruct((n, d), F32),
        scratch_shapes=[pltpu.VMEM((tm, d), BF16), pltpu.VMEM((tm, d), BF16)],
        compiler_params=_params("arbitrary", "arbitrary"),
        name="ffn",
    )(x, g.reshape(1, d), w_up, w_down)


def _ffn_casting(x, g, w_up, w_down, layer):
    n, d = x.shape
    f = w_up.shape[2]
    tk = 512
    assert n <= ROW_TILE, n
    return pl.pallas_call(
        _ffn_body,
        grid=(1, f // tk),
        in_specs=[
            pl.BlockSpec((n, d), lambda i, k: (0, 0)),
            pl.BlockSpec((1, d), lambda i, k: (0, 0)),
            pl.BlockSpec((None, d, tk), lambda i, k: (layer, 0, k)),
            pl.BlockSpec((None, tk, d), lambda i, k: (layer, k, 0)),
        ],
        out_specs=[pl.BlockSpec((n, d), lambda i, k: (0, 0)),
                   pl.BlockSpec((d, tk), lambda i, k: (0, k)),
                   pl.BlockSpec((tk, d), lambda i, k: (k, 0))],
        out_shape=[jax.ShapeDtypeStruct((n, d), F32), jax.ShapeDtypeStruct((d, f), BF16),
                   jax.ShapeDtypeStruct((f, d), BF16)],
        scratch_shapes=[pltpu.VMEM((n, d), BF16), pltpu.VMEM((n, d), BF16)],
        compiler_params=_params("arbitrary", "arbitrary"),
        name="ffn_casting",
    )(x, g.reshape(1, d), w_up, w_down)


def _run_trunk(x, past, pool_prev, sb_past_k, sb_past_v, mem_k, mem_v, p, ffn_bf16):
    b, t, d = x.shape
    n = b * t
    depth = p["w_out"].shape[0]
    pool_new, sbk_new, sbv_new, made = [], [], [], []
    x2 = x.reshape(n, d)
    for i in range(depth):
        j = i // 2
        if i % 2 == 0:
            proj = _norm_matmul(x2, p["norm_mix"][i], p["w_in_pool"], j)
            proj3 = proj.reshape(b, t, d)
            mix = _pool_mix(proj3, pool_prev[j], p["pool_w"][j], p["pool_scale"][j], past)
            pool_new.append(proj3[:, t - POOL_STATE:, :D_TOK])
            qm_src, qm_col = proj, D_TOK // D_MEM
        else:
            q, k, v, kb, vb, qm = _inproj_sb(x2, p["norm_mix"][i], p["w_in_sb"], j, b, t)
            if sb_past_k is None:
                mix = _sb_attn_prompt(q, kb, vb)
            else:
                mix = _sb_attn_cached(q, kb, vb, sb_past_k[j], sb_past_v[j])
            sbk_new.append(k)
            sbv_new.append(v)
            qm_src, qm_col = qm, 0
        x2 = _mix_out(x2, mix.reshape(n, D_TOK), qm_src, qm_col, mem_k, mem_v, p["q_norm"][i], p["w_out"], i, t)
        if ffn_bf16 is None:
            x2, wu, wd = _ffn_casting(x2, p["norm_ffn"][i], p["w_up"], p["w_down"], i)
            made.append((wu, wd))
        else:
            x2 = _ffn(x2, p["norm_ffn"][i], *ffn_bf16[i])
    return x2.reshape(b, t, d), jnp.stack(pool_new), jnp.stack(sbk_new), jnp.stack(sbv_new), made


def kernel(x_prompt, x_sample, mem_prompt, cache_pool, cache_sb_k, cache_sb_v, cache_mem_k, cache_mem_v,
           norm_mix, w_in_pool, w_in_sb, pool_w, pool_scale, norm_mem, w_mem_kv, q_norm, k_norm, w_out,
           norm_ffn, w_up, w_down):
    b, t, d = x_prompt.shape
    depth = w_out.shape[0]
    n_mem = mem_prompt.shape[1]
    p = dict(norm_mix=norm_mix, pool_scale=pool_scale, q_norm=q_norm, norm_ffn=norm_ffn,
             w_in_pool=w_in_pool.astype(BF16), w_in_sb=w_in_sb.astype(BF16), pool_w=pool_w.astype(BF16),
             w_out=w_out.astype(BF16), w_up=w_up, w_down=w_down)

    bs = x_sample.shape[0]
    past = cache_sb_k.shape[2]
    pool_prev = jnp.pad(cache_pool, ((0, 0), (0, 0), (POOL_HALO - POOL_STATE, 0), (0, 0)))
    head_major = lambda a: jnp.transpose(a, (0, 1, 3, 2, 4))
    y_sample, pool_sample, sb_k_sample, sb_v_sample, ffn_bf16 = _run_trunk(
        x_sample, past, pool_prev, head_major(cache_sb_k), head_major(cache_sb_v),
        cache_mem_k.reshape(depth, bs, n_mem, D_MEM), cache_mem_v.reshape(depth, bs, n_mem, D_MEM), p, None)

    mk, mv = _mem_kv(mem_prompt.reshape(b * n_mem, d), norm_mem, w_mem_kv.astype(BF16), k_norm)
    mk = mk.reshape(depth, b, n_mem, D_MEM)
    mv = mv.reshape(depth, b, n_mem, D_MEM)
    pool_zero = jnp.zeros((cache_pool.shape[0], b, POOL_HALO, D_TOK), F32)
    y_prompt, pool_prompt, sb_k_prompt, sb_v_prompt, _ = _run_trunk(
        x_prompt, 0, pool_zero, None, None, mk, mv, p, ffn_bf16)

    heads = lambda a: a.reshape(depth, b, n_mem, MEM_HEADS, HEAD_DIM)
    return (y_prompt, y_sample, pool_prompt, pool_sample, head_major(sb_k_prompt), head_major(sb_v_prompt),
            head_major(sb_k_sample), head_major(sb_v_sample), heads(mk), heads(mv))
```

```python
import functools
import math

import jax
import jax.numpy as jnp
from jax import lax
from jax.experimental import pallas as pl
from jax.experimental.pallas import tpu as pltpu

D_MODEL = 2048
HEAD_DIM = 128
D_TOK = 3 * D_MODEL // 4
D_MEM = D_MODEL // 4
SB_HEADS = D_TOK // HEAD_DIM
MEM_HEADS = D_MEM // HEAD_DIM
POOL_WINDOWS = (2, 4, 8, 16)
POOL_GROUP = D_TOK // len(POOL_WINDOWS)
POOL_STATE = max(POOL_WINDOWS) - 1
POOL_HALO = POOL_STATE + 1
D_FF = 4 * D_MODEL
EPS = 1e-6
INV_SQRT_HD = 1.0 / math.sqrt(HEAD_DIM)

SB_BLOCK = 128
SB_GROUP = 32
EXP_ZERO_BELOW = -104.0
RUN_FINISHED = -1e30
ROW_TILE = 512
VMEM_LIMIT = 48 * 1024 * 1024
FFN_VMEM_LIMIT = 60 * 1024 * 1024

BF16 = jnp.bfloat16
F32 = jnp.float32


def _params(*sem):
    return pltpu.CompilerParams(dimension_semantics=sem, vmem_limit_bytes=VMEM_LIMIT)


def _rms(xf, g):
    return xf * lax.rsqrt(jnp.mean(xf * xf, axis=-1, keepdims=True) + EPS) * g


def _dot(a, b):
    return jnp.dot(a, b, preferred_element_type=F32)


def _dot_nt(a, b):
    return lax.dot_general(a, b, (((1,), (1,)), ((), ())), preferred_element_type=F32)


def _resident(shape, layer):
    return pl.BlockSpec((None,) + shape, lambda *_: (layer,) + (0,) * len(shape), pipeline_mode=pl.Buffered(1))


def _norm_matmul_body(x_ref, g_ref, w_ref, o_ref):
    o_ref[...] = _dot(_rms(x_ref[...], g_ref[...]).astype(BF16), w_ref[...])


def _norm_matmul(x, g, w, layer):
    n, d = x.shape
    m = w.shape[2]
    tm = min(n, ROW_TILE)
    return pl.pallas_call(
        _norm_matmul_body,
        grid=(n // tm,),
        in_specs=[
            pl.BlockSpec((tm, d), lambda i: (i, 0)),
            pl.BlockSpec((1, d), lambda i: (0, 0)),
            _resident((d, m), layer),
        ],
        out_specs=pl.BlockSpec((tm, m), lambda i: (i, 0)),
        out_shape=jax.ShapeDtypeStruct((n, m), F32),
        compiler_params=_params("parallel"),
        name="norm_matmul",
    )(x, g.reshape(1, d), w)


def _inproj_sb_body(x_ref, g_ref, wq_ref, wk_ref, wv_ref, wm_ref,
                    q_ref, k_ref, v_ref, kb_ref, vb_ref, qm_ref, h_ref):
    @pl.when(pl.program_id(1) == 0)
    def _():
        h_ref[...] = _rms(x_ref[...], g_ref[...]).astype(BF16)
        qm_ref[...] = _dot(h_ref[...], wm_ref[...])

    def put(refs, val):
        bb, heads, tt, _ = refs[0].shape
        for bi in range(bb):
            for hd in range(heads):
                piece = val[bi * tt:(bi + 1) * tt, hd * HEAD_DIM:(hd + 1) * HEAD_DIM]
                for ref in refs:
                    ref[bi, hd] = piece.astype(ref.dtype)

    h = h_ref[...]
    put([q_ref], _dot(h, wq_ref[...]) * INV_SQRT_HD)
    put([k_ref, kb_ref], _dot(h, wk_ref[...]))
    put([v_ref, vb_ref], _dot(h, wv_ref[...]))


def _inproj_sb(x, g, w, layer, b, t):
    n, d = x.shape
    tm = min(n, 2 * ROW_TILE)
    tn = 2 * HEAD_DIM
    nj = D_TOK // tn
    hj = tn // HEAD_DIM
    if t >= tm:
        per_seq = t // tm
        oblock, omap = (1, hj, tm, HEAD_DIM), lambda i, j: (i // per_seq, j, i % per_seq, 0)
    else:
        oblock, omap = (tm // t, hj, t, HEAD_DIM), lambda i, j: (i, j, 0, 0)
    tok = lambda dt: jax.ShapeDtypeStruct((b, SB_HEADS, t, HEAD_DIM), dt)
    wspec = lambda off: pl.BlockSpec((None, d, tn), lambda i, j: (layer, 0, off + j))
    ospec = pl.BlockSpec(oblock, omap)
    return pl.pallas_call(
        _inproj_sb_body,
        grid=(n // tm, nj),
        in_specs=[
            pl.BlockSpec((tm, d), lambda i, j: (i, 0)),
            pl.BlockSpec((1, d), lambda i, j: (0, 0)),
            wspec(0), wspec(nj), wspec(2 * nj),
            pl.BlockSpec((None, d, D_MEM), lambda i, j: (layer, 0, 3 * D_TOK // D_MEM)),
        ],
        out_specs=[ospec, ospec, ospec, ospec, ospec,
                   pl.BlockSpec((tm, D_MEM), lambda i, j: (i, 0))],
        out_shape=[tok(BF16), tok(F32), tok(F32), tok(BF16), tok(BF16),
                   jax.ShapeDtypeStruct((n, D_MEM), F32)],
        scratch_shapes=[pltpu.VMEM((tm, d), BF16)],
        compiler_params=_params("parallel", "arbitrary"),
        name="inproj_sb",
    )(x, g.reshape(1, d), w, w, w, w)


def _mem_kv_body(x_ref, g_ref, w_ref, kg_ref, k_ref, v_ref):
    h = _rms(x_ref[...], g_ref[...]).astype(BF16)
    kv = _dot(h, w_ref[...])
    for hd in range(MEM_HEADS):
        sl = slice(hd * HEAD_DIM, (hd + 1) * HEAD_DIM)
        k_ref[:, sl] = _rms(kv[:, sl], kg_ref[...])
    v_ref[...] = kv[:, D_MEM:]


def _mem_kv(mem, g_mem, w_kv, k_gain):
    r, d = mem.shape
    depth = w_kv.shape[0]
    tm = min(r, 256)
    out = jax.ShapeDtypeStruct((depth, r, D_MEM), F32)
    ospec = pl.BlockSpec((None, tm, D_MEM), lambda l, i: (l, i, 0))
    return pl.pallas_call(
        _mem_kv_body,
        grid=(depth, r // tm),
        in_specs=[
            pl.BlockSpec((tm, d), lambda l, i: (i, 0)),
            pl.BlockSpec((None, 1, d), lambda l, i: (l, 0, 0)),
            pl.BlockSpec((None, d, 2 * D_MEM), lambda l, i: (l, 0, 0)),
            pl.BlockSpec((None, 1, HEAD_DIM), lambda l, i: (l, 0, 0)),
        ],
        out_specs=[ospec, ospec],
        out_shape=[out, out],
        compiler_params=_params("parallel", "parallel"),
        name="mem_kv",
    )(mem, g_mem.reshape(depth, 1, d), w_kv, k_gain.reshape(depth, 1, HEAD_DIM))


def _pool_body(u_ref, halo_ref, prev_ref, w_ref, s_ref, o_ref, ext_ref, a_ref, b_ref, *, tm, past):
    i = pl.program_id(1)
    base = 2 * POOL_HALO
    rows = tm + POOL_HALO
    ext_ref[0:POOL_HALO, :] = jnp.zeros((POOL_HALO, D_TOK), F32)
    a_ref[0:POOL_HALO, :] = jnp.zeros((POOL_HALO, POOL_GROUP), F32)
    b_ref[0:POOL_HALO, :] = jnp.zeros((POOL_HALO, POOL_GROUP), F32)

    @pl.when(i == 0)
    def _():
        ext_ref[POOL_HALO:base, :] = prev_ref[...]

    @pl.when(i > 0)
    def _():
        ext_ref[POOL_HALO:base, :] = halo_ref[...]

    ext_ref[base:, :] = u_ref[...]
    pos = (past + i * tm + lax.broadcasted_iota(jnp.int32, (tm, 1), 0)).astype(F32)
    for g, w in enumerate(POOL_WINDOWS):
        sl = slice(g * POOL_GROUP, (g + 1) * POOL_GROUP)
        src, cols, nxt = ext_ref, sl, (a_ref, b_ref)
        span = 1
        while span < w:
            part = src[POOL_HALO:POOL_HALO + rows, cols] + src[POOL_HALO - span:POOL_HALO - span + rows, cols]
            span *= 2
            if span < w:
                nxt[0][POOL_HALO:POOL_HALO + rows, :] = part
                src, cols, nxt = nxt[0], slice(None), nxt[::-1]
        win = part[POOL_HALO:, :]
        cur = ext_ref[base:, sl]
        cnt = jnp.minimum(pos + 1.0, float(w))
        dlt = (win / cnt - cur).astype(BF16)
        o_ref[:, sl] = (_dot(dlt, w_ref[g]) * s_ref[:, sl]).astype(BF16)


def _pool_mix(proj, prev, w_grp, scale, past):
    b, t, _ = proj.shape
    tm = min(t, 512)
    hb = tm // POOL_HALO
    return pl.pallas_call(
        functools.partial(_pool_body, tm=tm, past=past),
        grid=(b, t // tm),
        in_specs=[
            pl.BlockSpec((None, tm, D_TOK), lambda bi, i: (bi, i, 0)),
            pl.BlockSpec((None, POOL_HALO, D_TOK), lambda bi, i: (bi, jnp.maximum(i * hb - 1, 0), 0)),
            pl.BlockSpec((None, POOL_HALO, D_TOK), lambda bi, i: (bi, 0, 0)),
            pl.BlockSpec((len(POOL_WINDOWS), POOL_GROUP, POOL_GROUP), lambda bi, i: (0, 0, 0)),
            pl.BlockSpec((1, D_TOK), lambda bi, i: (0, 0)),
        ],
        out_specs=pl.BlockSpec((None, tm, D_TOK), lambda bi, i: (bi, i, 0)),
        out_shape=jax.ShapeDtypeStruct((b, t, D_TOK), BF16),
        scratch_shapes=[pltpu.VMEM((tm + 2 * POOL_HALO, D_TOK), F32),
                        pltpu.VMEM((tm + 2 * POOL_HALO, POOL_GROUP), F32),
                        pltpu.VMEM((tm + 2 * POOL_HALO, POOL_GROUP), F32)],
        compiler_params=_params("parallel", "arbitrary"),
        name="pool_mix",
    )(proj, proj, prev, w_grp, scale.reshape(1, D_TOK))


def _scan_matrix():
    shape = (2 * SB_BLOCK, SB_BLOCK + HEAD_DIM)
    r = lax.broadcasted_iota(jnp.int32, shape, 0) % SB_BLOCK
    c = lax.broadcasted_iota(jnp.int32, shape, 1)
    return jnp.where((c >= SB_BLOCK) | (r > c), 1.0, 0.0).astype(BF16)


def _sb_steps(qs, ks, vs, runs, mask, scan):
    tq = qs[0].shape[0]
    zs = [_dot_nt(q, k) for q, k in zip(qs, ks)]
    fails, parts = [], []
    for z in zs:
        log_fail = -(jnp.maximum(z, 0.0) + jnp.log(1.0 + jnp.exp(-jnp.abs(z))))
        if mask is not None:
            log_fail = jnp.where(mask, log_fail, 0.0)
        hi = log_fail.astype(BF16)
        lo = (log_fail - hi.astype(F32)).astype(BF16)
        fails.append(log_fail)
        parts.append(jnp.concatenate([hi, lo], axis=1))
    sums = _dot(jnp.concatenate(parts, axis=0), scan)
    weights, new_runs = [], []
    for g, (z, log_fail, run) in enumerate(zip(zs, fails, runs)):
        s = sums[g * tq:(g + 1) * tq]
        a = jnp.exp(z + log_fail + s[:, :SB_BLOCK] + run)
        if mask is not None:
            a = jnp.where(mask, a, 0.0)
        weights.append(a.astype(BF16))
        new_runs.append(run + s[:, SB_BLOCK:])
    return new_runs, [_dot(a, v) for a, v in zip(weights, vs)]


def _largest(values):
    out = values[0]
    for v in values[1:]:
        out = jnp.maximum(out, v)
    return jnp.max(out)


def _sb_prompt_body(q_ref, k_ref, v_ref, o_ref, run_ref, acc_ref, scan_ref, *, t, group):
    scan_ref[...] = _scan_matrix()
    below_diag = (lax.broadcasted_iota(jnp.int32, (SB_BLOCK, SB_BLOCK), 1) <
                  lax.broadcasted_iota(jnp.int32, (SB_BLOCK, SB_BLOCK), 0))
    zero = jnp.zeros((SB_BLOCK, HEAD_DIM), F32)
    chain = [slice(g * SB_BLOCK, (g + 1) * SB_BLOCK) for g in range(group)]

    def block(i):
        return pl.ds(pl.multiple_of(i * SB_BLOCK, SB_BLOCK), SB_BLOCK)

    def group_scan(s, _):
        first = s * group
        tiles = [block(first + g) for g in range(group)]
        runs, pvs = _sb_steps([q_ref[r, :] for r in tiles], [k_ref[r, :] for r in tiles],
                              [v_ref[r, :] for r in tiles], [zero] * group, below_diag, scan_ref[...])
        for sl, run, pv in zip(chain, runs, pvs):
            run_ref[sl, :] = run
            acc_ref[sl, :] = pv

        def more(c):
            it, largest_run = c
            return jnp.logical_and(it < first + group, largest_run > EXP_ZERO_BELOW)

        def step(c):
            it, _ = c
            keys = [block(jnp.maximum(first + g - it, 0)) for g in range(group)]
            runs = [jnp.where(first + g - it >= 0, run_ref[chain[g], :], RUN_FINISHED) for g in range(group)]
            runs, pvs = _sb_steps([q_ref[r, :] for r in tiles], [k_ref[r, :] for r in keys],
                                  [v_ref[r, :] for r in keys], runs, None, scan_ref[...])
            for sl, run, pv in zip(chain, runs, pvs):
                run_ref[sl, :] = run
                acc_ref[sl, :] += pv
            return it + 1, _largest(runs)

        lax.while_loop(more, step, (jnp.int32(1), _largest(runs)))
        out_rows = pl.ds(pl.multiple_of(first * SB_BLOCK, SB_BLOCK), group * SB_BLOCK)
        o_ref[out_rows, :] = acc_ref[...].astype(o_ref.dtype)
        return 0

    lax.fori_loop(0, t // (group * SB_BLOCK), group_scan, 0)


def _sb_attn_prompt(q, k, v):
    b, _, t, _ = q.shape
    group = min(SB_GROUP, t // SB_BLOCK)
    assert t % (group * SB_BLOCK) == 0, t
    spec = pl.BlockSpec((None, None, t, HEAD_DIM), lambda bi, h: (bi, h, 0, 0))
    return pl.pallas_call(
        functools.partial(_sb_prompt_body, t=t, group=group),
        grid=(b, SB_HEADS),
        in_specs=[spec, spec, spec],
        out_specs=pl.BlockSpec((None, t, HEAD_DIM), lambda bi, h: (bi, 0, h)),
        out_shape=jax.ShapeDtypeStruct((b, t, D_TOK), BF16),
        scratch_shapes=[pltpu.VMEM((group * SB_BLOCK, HEAD_DIM), F32),
                        pltpu.VMEM((group * SB_BLOCK, HEAD_DIM), F32),
                        pltpu.VMEM((2 * SB_BLOCK, SB_BLOCK + HEAD_DIM), BF16)],
        compiler_params=_params("parallel", "parallel"),
        name="sb_attn_prompt",
    )(q, k, v)


def _sb_cached_body(q_ref, kn_ref, vn_ref, kc_ref, vc_ref, o_ref, run_ref, acc_ref, scan_ref, *, t, past):
    scan_ref[...] = _scan_matrix()
    earlier = (lax.broadcasted_iota(jnp.int32, (t, SB_BLOCK), 1) <
               lax.broadcasted_iota(jnp.int32, (t, SB_BLOCK), 0))
    zero = jnp.zeros((t, HEAD_DIM), F32)
    pad = jnp.zeros((SB_BLOCK - t, HEAD_DIM), BF16)
    heads = range(SB_HEADS)
    chain = [slice(h * t, (h + 1) * t) for h in heads]

    runs, pvs = _sb_steps([q_ref[h] for h in heads],
                          [jnp.concatenate([kn_ref[h], pad], axis=0) for h in heads],
                          [jnp.concatenate([vn_ref[h], pad], axis=0) for h in heads],
                          [zero] * SB_HEADS, earlier, scan_ref[...])
    for sl, run, pv in zip(chain, runs, pvs):
        run_ref[sl, :] = run
        acc_ref[sl, :] = pv

    def more(c):
        it, largest_run = c
        return jnp.logical_and(it < past // SB_BLOCK, largest_run > EXP_ZERO_BELOW)

    def step(c):
        it, _ = c
        keys = pl.ds(pl.multiple_of(past - (it + 1) * SB_BLOCK, SB_BLOCK), SB_BLOCK)
        runs, pvs = _sb_steps([q_ref[h] for h in heads], [kc_ref[h, keys, :].astype(BF16) for h in heads],
                              [vc_ref[h, keys, :].astype(BF16) for h in heads], [run_ref[sl, :] for sl in chain],
                              None, scan_ref[...])
        for sl, run, pv in zip(chain, runs, pvs):
            run_ref[sl, :] = run
            acc_ref[sl, :] += pv
        return it + 1, _largest(runs)

    lax.while_loop(more, step, (jnp.int32(0), _largest(runs)))
    for h, sl in zip(heads, chain):
        o_ref[:, h * HEAD_DIM:(h + 1) * HEAD_DIM] = acc_ref[sl, :].astype(o_ref.dtype)


def _sb_attn_cached(q, k_new, v_new, k_past, v_past):
    b, _, t, _ = q.shape
    past = k_past.shape[2]
    assert past % SB_BLOCK == 0 and t <= SB_BLOCK and t % 16 == 0, (past, t)
    new = pl.BlockSpec((None, SB_HEADS, t, HEAD_DIM), lambda bi: (bi, 0, 0, 0))
    old = pl.BlockSpec((None, SB_HEADS, past, HEAD_DIM), lambda bi: (bi, 0, 0, 0))
    return pl.pallas_call(
        functools.partial(_sb_cached_body, t=t, past=past),
        grid=(b,),
        in_specs=[new, new, new, old, old],
        out_specs=pl.BlockSpec((None, t, D_TOK), lambda bi: (bi, 0, 0)),
        out_shape=jax.ShapeDtypeStruct((b, t, D_TOK), BF16),
        scratch_shapes=[pltpu.VMEM((SB_HEADS * t, HEAD_DIM), F32),
                        pltpu.VMEM((SB_HEADS * t, HEAD_DIM), F32),
                        pltpu.VMEM((2 * SB_BLOCK, SB_BLOCK + HEAD_DIM), BF16)],
        compiler_params=_params("parallel"),
        name="sb_attn_cached",
    )(q, k_new, v_new, k_past, v_past)


def _mix_out_body(x_ref, mix_ref, qm_ref, mk_ref, mv_ref, qg_ref, w_ref, o_ref):
    bb = mk_ref.shape[0]
    tt = x_ref.shape[0] // bb
    probs = []
    for bi in range(bb):
        rows = slice(bi * tt, (bi + 1) * tt)
        for hd in range(MEM_HEADS):
            sl = slice(hd * HEAD_DIM, (hd + 1) * HEAD_DIM)
            qh = _rms(qm_ref[rows, sl], qg_ref[...]).astype(BF16)
            s = _dot_nt(qh, mk_ref[bi, :, sl].astype(BF16)) * INV_SQRT_HD
            p = jnp.exp(s - jnp.max(s, axis=-1, keepdims=True))
            probs.append((p / jnp.sum(p, axis=-1, keepdims=True)).astype(BF16))
    y = x_ref[...] + _dot(mix_ref[...], w_ref[:D_TOK, :])
    mo = jnp.concatenate(
        [jnp.concatenate([_dot(probs[bi * MEM_HEADS + hd],
                               mv_ref[bi, :, hd * HEAD_DIM:(hd + 1) * HEAD_DIM].astype(BF16))
                          for hd in range(MEM_HEADS)], axis=1) for bi in range(bb)], axis=0)
    o_ref[...] = y + _dot(mo.astype(BF16), w_ref[D_TOK:, :])


def _mix_out(x, mix, qm_src, qm_col, mem_k, mem_v, q_gain, w_out, layer, t):
    n, d = x.shape
    n_mem = mem_k.shape[2]
    tm = min(n, ROW_TILE)
    if t >= tm:
        per_seq = t // tm
        mem = pl.BlockSpec((None, 1, n_mem, D_MEM), lambda i: (layer, i // per_seq, 0, 0))
    else:
        mem = pl.BlockSpec((None, tm // t, n_mem, D_MEM), lambda i: (layer, i, 0, 0),
                           pipeline_mode=pl.Buffered(1))
    row = lambda c, col=0: pl.BlockSpec((tm, c), lambda i: (i, col))
    return pl.pallas_call(
        _mix_out_body,
        grid=(n // tm,),
        in_specs=[row(d), row(D_TOK), row(D_MEM, qm_col), mem, mem,
                  pl.BlockSpec((1, HEAD_DIM), lambda i: (0, 0)), _resident((d, d), layer)],
        out_specs=row(d),
        out_shape=jax.ShapeDtypeStruct((n, d), F32),
        compiler_params=_params("parallel"),
        name="mix_out",
    )(x, mix, qm_src, mem_k, mem_v, q_gain.reshape(1, HEAD_DIM), w_out)


def _ffn_body(x_ref, g_ref, wu_ref, wd_ref, o_ref, *rest):
    h_ref = rest[-1]

    @pl.when(pl.program_id(1) == 0)
    def _():
        x = x_ref[...]
        h_ref[...] = _rms(x, g_ref[...]).astype(BF16)
        o_ref[...] = x

    wu = wu_ref[...].astype(BF16)
    wd = wd_ref[...].astype(BF16)
    if len(rest) == 3:
        rest[0][...] = wu
        rest[1][...] = wd
    half = wu.shape[1] // 2
    for c in range(2):
        cols = slice(c * half, (c + 1) * half)
        up = jnp.maximum(_dot(h_ref[...], wu[:, cols]), 0.0)
        o_ref[...] += _dot((up * up).astype(BF16), wd[cols, :])


def _ffn(x, g, w_up, w_down):
    n, d = x.shape
    f = w_up.shape[1]
    tm = min(n, ROW_TILE)
    tk = 2048
    return pl.pallas_call(
        _ffn_body,
        grid=(n // tm, f // tk),
        in_specs=[
            pl.BlockSpec((tm, d), lambda i, k: (i, 0)),
            pl.BlockSpec((1, d), lambda i, k: (0, 0)),
            pl.BlockSpec((d, tk), lambda i, k: (0, k)),
            pl.BlockSpec((tk, d), lambda i, k: (k, 0)),
        ],
        out_specs=pl.BlockSpec((tm, d), lambda i, k: (i, 0)),
        out_shape=jax.ShapeDtypeStruct((n, d), F32),
        scratch_shapes=[pltpu.VMEM((tm, d), BF16)],
        compiler_params=pltpu.CompilerParams(dimension_semantics=("parallel", "arbitrary"),
                                             vmem_limit_bytes=FFN_VMEM_LIMIT),
        name="ffn",
    )(x, g.reshape(1, d), w_up, w_down)


def _ffn_casting(x, g, w_up, w_down, layer):
    n, d = x.shape
    f = w_up.shape[2]
    tk = 512
    assert n <= ROW_TILE, n
    return pl.pallas_call(
        _ffn_body,
        grid=(1, f // tk),
        in_specs=[
            pl.BlockSpec((n, d), lambda i, k: (0, 0)),
            pl.BlockSpec((1, d), lambda i, k: (0, 0)),
            pl.BlockSpec((None, d, tk), lambda i, k: (layer, 0, k)),
            pl.BlockSpec((None, tk, d), lambda i, k: (layer, k, 0)),
        ],
        out_specs=[pl.BlockSpec((n, d), lambda i, k: (0, 0)),
                   pl.BlockSpec((d, tk), lambda i, k: (0, k)),
                   pl.BlockSpec((tk, d), lambda i, k: (k, 0))],
        out_shape=[jax.ShapeDtypeStruct((n, d), F32), jax.ShapeDtypeStruct((d, f), BF16),
                   jax.ShapeDtypeStruct((f, d), BF16)],
        scratch_shapes=[pltpu.VMEM((n, d), BF16)],
        compiler_params=_params("arbitrary", "arbitrary"),
        name="ffn_casting",
    )(x, g.reshape(1, d), w_up, w_down)


def _run_trunk(x, past, pool_prev, sb_past_k, sb_past_v, mem_k, mem_v, p, ffn_bf16):
    b, t, d = x.shape
    n = b * t
    depth = p["w_out"].shape[0]
    pool_new, sbk_new, sbv_new, made = [], [], [], []
    x2 = x.reshape(n, d)
    for i in range(depth):
        j = i // 2
        if i % 2 == 0:
            proj = _norm_matmul(x2, p["norm_mix"][i], p["w_in_pool"], j)
            proj3 = proj.reshape(b, t, d)
            mix = _pool_mix(proj3, pool_prev[j], p["pool_w"][j], p["pool_scale"][j], past)
            pool_new.append(proj3[:, t - POOL_STATE:, :D_TOK])
            qm_src, qm_col = proj, D_TOK // D_MEM
        else:
            q, k, v, kb, vb, qm = _inproj_sb(x2, p["norm_mix"][i], p["w_in_sb"], j, b, t)
            if sb_past_k is None:
                mix = _sb_attn_prompt(q, kb, vb)
            else:
                mix = _sb_attn_cached(q, kb, vb, sb_past_k[j], sb_past_v[j])
            sbk_new.append(k)
            sbv_new.append(v)
            qm_src, qm_col = qm, 0
        x2 = _mix_out(x2, mix.reshape(n, D_TOK), qm_src, qm_col, mem_k, mem_v, p["q_norm"][i], p["w_out"], i, t)
        if ffn_bf16 is None:
            x2, wu, wd = _ffn_casting(x2, p["norm_ffn"][i], p["w_up"], p["w_down"], i)
            made.append((wu, wd))
        else:
            x2 = _ffn(x2, p["norm_ffn"][i], *ffn_bf16[i])
    return x2.reshape(b, t, d), jnp.stack(pool_new), jnp.stack(sbk_new), jnp.stack(sbv_new), made


def kernel(x_prompt, x_sample, mem_prompt, cache_pool, cache_sb_k, cache_sb_v, cache_mem_k, cache_mem_v,
           norm_mix, w_in_pool, w_in_sb, pool_w, pool_scale, norm_mem, w_mem_kv, q_norm, k_norm, w_out,
           norm_ffn, w_up, w_down):
    b, t, d = x_prompt.shape
    depth = w_out.shape[0]
    n_mem = mem_prompt.shape[1]
    p = dict(norm_mix=norm_mix, pool_scale=pool_scale, q_norm=q_norm, norm_ffn=norm_ffn,
             w_in_pool=w_in_pool.astype(BF16), w_in_sb=w_in_sb.astype(BF16), pool_w=pool_w.astype(BF16),
             w_out=w_out.astype(BF16), w_up=w_up, w_down=w_down)

    bs = x_sample.shape[0]
    past = cache_sb_k.shape[2]
    pool_prev = jnp.pad(cache_pool, ((0, 0), (0, 0), (POOL_HALO - POOL_STATE, 0), (0, 0)))
    head_major = lambda a: jnp.transpose(a, (0, 1, 3, 2, 4))
    y_sample, pool_sample, sb_k_sample, sb_v_sample, ffn_bf16 = _run_trunk(
        x_sample, past, pool_prev, head_major(cache_sb_k), head_major(cache_sb_v),
        cache_mem_k.reshape(depth, bs, n_mem, D_MEM), cache_mem_v.reshape(depth, bs, n_mem, D_MEM), p, None)

    mk, mv = _mem_kv(mem_prompt.reshape(b * n_mem, d), norm_mem, w_mem_kv.astype(BF16), k_norm)
    mk = mk.reshape(depth, b, n_mem, D_MEM)
    mv = mv.reshape(depth, b, n_mem, D_MEM)
    pool_zero = jnp.zeros((cache_pool.shape[0], b, POOL_HALO, D_TOK), F32)
    y_prompt, pool_prompt, sb_k_prompt, sb_v_prompt, _ = _run_trunk(
        x_prompt, 0, pool_zero, None, None, mk, mv, p, ffn_bf16)

    heads = lambda a: a.reshape(depth, b, n_mem, MEM_HEADS, HEAD_DIM)
    return (y_prompt, y_sample, pool_prompt, pool_sample, head_major(sb_k_prompt), head_major(sb_v_prompt),
            head_major(sb_k_sample), head_major(sb_v_sample), heads(mk), heads(mv))
```

```python
import functools
import math

import jax
import jax.numpy as jnp
from jax import lax
from jax.experimental import pallas as pl
from jax.experimental.pallas import tpu as pltpu

D_MODEL = 2048
HEAD_DIM = 128
D_TOK = 3 * D_MODEL // 4
D_MEM = D_MODEL // 4
SB_HEADS = D_TOK // HEAD_DIM
MEM_HEADS = D_MEM // HEAD_DIM
POOL_WINDOWS = (2, 4, 8, 16)
POOL_GROUP = D_TOK // len(POOL_WINDOWS)
POOL_STATE = max(POOL_WINDOWS) - 1
POOL_HALO = POOL_STATE + 1
D_FF = 4 * D_MODEL
EPS = 1e-6
INV_SQRT_HD = 1.0 / math.sqrt(HEAD_DIM)

SB_BLOCK = 128
SB_GROUP = 32
EXP_ZERO_BELOW = -104.0
RUN_FINISHED = -1e30
ROW_TILE = 512
VMEM_LIMIT = 48 * 1024 * 1024
BIG_VMEM_LIMIT = 60 * 1024 * 1024

BF16 = jnp.bfloat16
F32 = jnp.float32


def _params(*sem):
    return pltpu.CompilerParams(dimension_semantics=sem, vmem_limit_bytes=VMEM_LIMIT)


def _rms(xf, g):
    return xf * lax.rsqrt(jnp.mean(xf * xf, axis=-1, keepdims=True) + EPS) * g


def _dot(a, b):
    return jnp.dot(a, b, preferred_element_type=F32)


def _dot_nt(a, b):
    return lax.dot_general(a, b, (((1,), (1,)), ((), ())), preferred_element_type=F32)


def _resident(shape, layer):
    return pl.BlockSpec((None,) + shape, lambda *_: (layer,) + (0,) * len(shape), pipeline_mode=pl.Buffered(1))


def _norm_matmul_body(x_ref, g_ref, w_ref, o_ref):
    o_ref[...] = _dot(_rms(x_ref[...], g_ref[...]).astype(BF16), w_ref[...])


def _norm_matmul(x, g, w, layer):
    n, d = x.shape
    m = w.shape[2]
    tm = min(n, ROW_TILE)
    return pl.pallas_call(
        _norm_matmul_body,
        grid=(n // tm,),
        in_specs=[
            pl.BlockSpec((tm, d), lambda i: (i, 0)),
            pl.BlockSpec((1, d), lambda i: (0, 0)),
            _resident((d, m), layer),
        ],
        out_specs=pl.BlockSpec((tm, m), lambda i: (i, 0)),
        out_shape=jax.ShapeDtypeStruct((n, m), F32),
        compiler_params=_params("parallel"),
        name="norm_matmul",
    )(x, g.reshape(1, d), w)


def _inproj_sb_body(x_ref, g_ref, wq_ref, wk_ref, wv_ref, wm_ref,
                    q_ref, k_ref, v_ref, kb_ref, vb_ref, qm_ref, h_ref):
    @pl.when(pl.program_id(1) == 0)
    def _():
        h_ref[...] = _rms(x_ref[...], g_ref[...]).astype(BF16)
        qm_ref[...] = _dot(h_ref[...], wm_ref[...])

    def put(refs, val):
        bb, heads, tt, _ = refs[0].shape
        for bi in range(bb):
            for hd in range(heads):
                piece = val[bi * tt:(bi + 1) * tt, hd * HEAD_DIM:(hd + 1) * HEAD_DIM]
                for ref in refs:
                    ref[bi, hd] = piece.astype(ref.dtype)

    h = h_ref[...]
    put([q_ref], _dot(h, wq_ref[...]) * INV_SQRT_HD)
    put([k_ref, kb_ref], _dot(h, wk_ref[...]))
    put([v_ref, vb_ref], _dot(h, wv_ref[...]))


def _inproj_sb(x, g, w, layer, b, t):
    n, d = x.shape
    tm = min(n, 2 * ROW_TILE)
    tn = D_MEM
    nj = D_TOK // tn
    hj = tn // HEAD_DIM
    if t >= tm:
        per_seq = t // tm
        oblock, omap = (1, hj, tm, HEAD_DIM), lambda i, j: (i // per_seq, j, i % per_seq, 0)
    else:
        oblock, omap = (tm // t, hj, t, HEAD_DIM), lambda i, j: (i, j, 0, 0)
    tok = lambda dt: jax.ShapeDtypeStruct((b, SB_HEADS, t, HEAD_DIM), dt)
    wspec = lambda off: pl.BlockSpec((None, d, tn), lambda i, j: (layer, 0, off + j))
    ospec = pl.BlockSpec(oblock, omap)
    return pl.pallas_call(
        _inproj_sb_body,
        grid=(n // tm, nj),
        in_specs=[
            pl.BlockSpec((tm, d), lambda i, j: (i, 0)),
            pl.BlockSpec((1, d), lambda i, j: (0, 0)),
            wspec(0), wspec(nj), wspec(2 * nj),
            pl.BlockSpec((None, d, D_MEM), lambda i, j: (layer, 0, 3 * D_TOK // D_MEM),
                         pipeline_mode=pl.Buffered(1)),
        ],
        out_specs=[ospec, ospec, ospec, ospec, ospec,
                   pl.BlockSpec((tm, D_MEM), lambda i, j: (i, 0))],
        out_shape=[tok(BF16), tok(F32), tok(F32), tok(BF16), tok(BF16),
                   jax.ShapeDtypeStruct((n, D_MEM), F32)],
        scratch_shapes=[pltpu.VMEM((tm, d), BF16)],
        compiler_params=pltpu.CompilerParams(dimension_semantics=("parallel", "arbitrary"),
                                             vmem_limit_bytes=BIG_VMEM_LIMIT),
        name="inproj_sb",
    )(x, g.reshape(1, d), w, w, w, w)


def _mem_kv_body(x_ref, g_ref, w_ref, kg_ref, k_ref, v_ref):
    h = _rms(x_ref[...], g_ref[...]).astype(BF16)
    kv = _dot(h, w_ref[...])
    for hd in range(MEM_HEADS):
        sl = slice(hd * HEAD_DIM, (hd + 1) * HEAD_DIM)
        k_ref[:, sl] = _rms(kv[:, sl], kg_ref[...])
    v_ref[...] = kv[:, D_MEM:]


def _mem_kv(mem, g_mem, w_kv, k_gain):
    r, d = mem.shape
    depth = w_kv.shape[0]
    tm = min(r, 256)
    out = jax.ShapeDtypeStruct((depth, r, D_MEM), F32)
    ospec = pl.BlockSpec((None, tm, D_MEM), lambda l, i: (l, i, 0))
    return pl.pallas_call(
        _mem_kv_body,
        grid=(depth, r // tm),
        in_specs=[
            pl.BlockSpec((tm, d), lambda l, i: (i, 0)),
            pl.BlockSpec((None, 1, d), lambda l, i: (l, 0, 0)),
            pl.BlockSpec((None, d, 2 * D_MEM), lambda l, i: (l, 0, 0)),
            pl.BlockSpec((None, 1, HEAD_DIM), lambda l, i: (l, 0, 0)),
        ],
        out_specs=[ospec, ospec],
        out_shape=[out, out],
        compiler_params=_params("parallel", "parallel"),
        name="mem_kv",
    )(mem, g_mem.reshape(depth, 1, d), w_kv, k_gain.reshape(depth, 1, HEAD_DIM))


def _pool_body(u_ref, halo_ref, prev_ref, w_ref, s_ref, o_ref, ext_ref, a_ref, b_ref, *, tm, past):
    i = pl.program_id(1)
    base = 2 * POOL_HALO
    rows = tm + POOL_HALO
    ext_ref[0:POOL_HALO, :] = jnp.zeros((POOL_HALO, D_TOK), F32)
    a_ref[0:POOL_HALO, :] = jnp.zeros((POOL_HALO, POOL_GROUP), F32)
    b_ref[0:POOL_HALO, :] = jnp.zeros((POOL_HALO, POOL_GROUP), F32)

    @pl.when(i == 0)
    def _():
        ext_ref[POOL_HALO:base, :] = prev_ref[...]

    @pl.when(i > 0)
    def _():
        ext_ref[POOL_HALO:base, :] = halo_ref[...]

    ext_ref[base:, :] = u_ref[...]
    pos = (past + i * tm + lax.broadcasted_iota(jnp.int32, (tm, 1), 0)).astype(F32)
    for g, w in enumerate(POOL_WINDOWS):
        sl = slice(g * POOL_GROUP, (g + 1) * POOL_GROUP)
        src, cols, nxt = ext_ref, sl, (a_ref, b_ref)
        span = 1
        while span < w:
            part = src[POOL_HALO:POOL_HALO + rows, cols] + src[POOL_HALO - span:POOL_HALO - span + rows, cols]
            span *= 2
            if span < w:
                nxt[0][POOL_HALO:POOL_HALO + rows, :] = part
                src, cols, nxt = nxt[0], slice(None), nxt[::-1]
        win = part[POOL_HALO:, :]
        cur = ext_ref[base:, sl]
        cnt = jnp.minimum(pos + 1.0, float(w))
        dlt = (win / cnt - cur).astype(BF16)
        o_ref[:, sl] = (_dot(dlt, w_ref[g]) * s_ref[:, sl]).astype(BF16)


def _pool_mix(proj, prev, w_grp, scale, past):
    b, t, _ = proj.shape
    tm = min(t, 512)
    hb = tm // POOL_HALO
    return pl.pallas_call(
        functools.partial(_pool_body, tm=tm, past=past),
        grid=(b, t // tm),
        in_specs=[
            pl.BlockSpec((None, tm, D_TOK), lambda bi, i: (bi, i, 0)),
            pl.BlockSpec((None, POOL_HALO, D_TOK), lambda bi, i: (bi, jnp.maximum(i * hb - 1, 0), 0)),
            pl.BlockSpec((None, POOL_HALO, D_TOK), lambda bi, i: (bi, 0, 0)),
            pl.BlockSpec((len(POOL_WINDOWS), POOL_GROUP, POOL_GROUP), lambda bi, i: (0, 0, 0)),
            pl.BlockSpec((1, D_TOK), lambda bi, i: (0, 0)),
        ],
        out_specs=pl.BlockSpec((None, tm, D_TOK), lambda bi, i: (bi, i, 0)),
        out_shape=jax.ShapeDtypeStruct((b, t, D_TOK), BF16),
        scratch_shapes=[pltpu.VMEM((tm + 2 * POOL_HALO, D_TOK), F32),
                        pltpu.VMEM((tm + 2 * POOL_HALO, POOL_GROUP), F32),
                        pltpu.VMEM((tm + 2 * POOL_HALO, POOL_GROUP), F32)],
        compiler_params=_params("parallel", "arbitrary"),
        name="pool_mix",
    )(proj, proj, prev, w_grp, scale.reshape(1, D_TOK))


def _scan_matrix():
    shape = (2 * SB_BLOCK, SB_BLOCK + HEAD_DIM)
    r = lax.broadcasted_iota(jnp.int32, shape, 0) % SB_BLOCK
    c = lax.broadcasted_iota(jnp.int32, shape, 1)
    return jnp.where((c >= SB_BLOCK) | (r > c), 1.0, 0.0).astype(BF16)


def _sb_steps(qs, ks, vs, runs, mask, scan):
    tq = qs[0].shape[0]
    zs = [_dot_nt(q, k) for q, k in zip(qs, ks)]
    fails, parts = [], []
    for z in zs:
        log_fail = -(jnp.maximum(z, 0.0) + jnp.log(1.0 + jnp.exp(-jnp.abs(z))))
        if mask is not None:
            log_fail = jnp.where(mask, log_fail, 0.0)
        hi = log_fail.astype(BF16)
        lo = (log_fail - hi.astype(F32)).astype(BF16)
        fails.append(log_fail)
        parts.append(jnp.concatenate([hi, lo], axis=1))
    sums = _dot(jnp.concatenate(parts, axis=0), scan)
    weights, new_runs = [], []
    for g, (z, log_fail, run) in enumerate(zip(zs, fails, runs)):
        s = sums[g * tq:(g + 1) * tq]
        a = jnp.exp(z + log_fail + s[:, :SB_BLOCK] + run)
        if mask is not None:
            a = jnp.where(mask, a, 0.0)
        weights.append(a.astype(BF16))
        new_runs.append(run + s[:, SB_BLOCK:])
    return new_runs, [_dot(a, v) for a, v in zip(weights, vs)]


def _largest(values):
    out = values[0]
    for v in values[1:]:
        out = jnp.maximum(out, v)
    return jnp.max(out)


def _sb_prompt_body(q_ref, k_ref, v_ref, o_ref, run_ref, acc_ref, scan_ref, *, t, group):
    scan_ref[...] = _scan_matrix()
    below_diag = (lax.broadcasted_iota(jnp.int32, (SB_BLOCK, SB_BLOCK), 1) <
                  lax.broadcasted_iota(jnp.int32, (SB_BLOCK, SB_BLOCK), 0))
    zero = jnp.zeros((SB_BLOCK, HEAD_DIM), F32)
    chain = [slice(g * SB_BLOCK, (g + 1) * SB_BLOCK) for g in range(group)]

    def block(i):
        return pl.ds(pl.multiple_of(i * SB_BLOCK, SB_BLOCK), SB_BLOCK)

    def group_scan(s, _):
        first = s * group
        tiles = [block(first + g) for g in range(group)]
        runs, pvs = _sb_steps([q_ref[r, :] for r in tiles], [k_ref[r, :] for r in tiles],
                              [v_ref[r, :] for r in tiles], [zero] * group, below_diag, scan_ref[...])
        for sl, run, pv in zip(chain, runs, pvs):
            run_ref[sl, :] = run
            acc_ref[sl, :] = pv

        def more(c):
            it, largest_run = c
            return jnp.logical_and(it < first + group, largest_run > EXP_ZERO_BELOW)

        def step(c):
            it, _ = c
            keys = [block(jnp.maximum(first + g - it, 0)) for g in range(group)]
            runs = [jnp.where(first + g - it >= 0, run_ref[chain[g], :], RUN_FINISHED) for g in range(group)]
            runs, pvs = _sb_steps([q_ref[r, :] for r in tiles], [k_ref[r, :] for r in keys],
                                  [v_ref[r, :] for r in keys], runs, None, scan_ref[...])
            for sl, run, pv in zip(chain, runs, pvs):
                run_ref[sl, :] = run
                acc_ref[sl, :] += pv
            return it + 1, _largest(runs)

        lax.while_loop(more, step, (jnp.int32(1), _largest(runs)))
        out_rows = pl.ds(pl.multiple_of(first * SB_BLOCK, SB_BLOCK), group * SB_BLOCK)
        o_ref[out_rows, :] = acc_ref[...].astype(o_ref.dtype)
        return 0

    lax.fori_loop(0, t // (group * SB_BLOCK), group_scan, 0)


def _sb_attn_prompt(q, k, v):
    b, _, t, _ = q.shape
    group = min(SB_GROUP, t // SB_BLOCK)
    assert t % (group * SB_BLOCK) == 0, t
    spec = pl.BlockSpec((None, None, t, HEAD_DIM), lambda bi, h: (bi, h, 0, 0))
    return pl.pallas_call(
        functools.partial(_sb_prompt_body, t=t, group=group),
        grid=(b, SB_HEADS),
        in_specs=[spec, spec, spec],
        out_specs=pl.BlockSpec((None, t, HEAD_DIM), lambda bi, h: (bi, 0, h)),
        out_shape=jax.ShapeDtypeStruct((b, t, D_TOK), BF16),
        scratch_shapes=[pltpu.VMEM((group * SB_BLOCK, HEAD_DIM), F32),
                        pltpu.VMEM((group * SB_BLOCK, HEAD_DIM), F32),
                        pltpu.VMEM((2 * SB_BLOCK, SB_BLOCK + HEAD_DIM), BF16)],
        compiler_params=_params("parallel", "parallel"),
        name="sb_attn_prompt",
    )(q, k, v)


def _sb_cached_body(q_ref, kn_ref, vn_ref, kc_ref, vc_ref, o_ref, run_ref, acc_ref, scan_ref, *, t, past):
    scan_ref[...] = _scan_matrix()
    earlier = (lax.broadcasted_iota(jnp.int32, (t, SB_BLOCK), 1) <
               lax.broadcasted_iota(jnp.int32, (t, SB_BLOCK), 0))
    zero = jnp.zeros((t, HEAD_DIM), F32)
    pad = jnp.zeros((SB_BLOCK - t, HEAD_DIM), BF16)
    heads = range(SB_HEADS)
    chain = [slice(h * t, (h + 1) * t) for h in heads]

    runs, pvs = _sb_steps([q_ref[h] for h in heads],
                          [jnp.concatenate([kn_ref[h], pad], axis=0) for h in heads],
                          [jnp.concatenate([vn_ref[h], pad], axis=0) for h in heads],
                          [zero] * SB_HEADS, earlier, scan_ref[...])
    for sl, run, pv in zip(chain, runs, pvs):
        run_ref[sl, :] = run
        acc_ref[sl, :] = pv

    def more(c):
        it, largest_run = c
        return jnp.logical_and(it < past // SB_BLOCK, largest_run > EXP_ZERO_BELOW)

    def step(c):
        it, _ = c
        keys = pl.ds(pl.multiple_of(past - (it + 1) * SB_BLOCK, SB_BLOCK), SB_BLOCK)
        runs, pvs = _sb_steps([q_ref[h] for h in heads], [kc_ref[h, keys, :].astype(BF16) for h in heads],
                              [vc_ref[h, keys, :].astype(BF16) for h in heads], [run_ref[sl, :] for sl in chain],
                              None, scan_ref[...])
        for sl, run, pv in zip(chain, runs, pvs):
            run_ref[sl, :] = run
            acc_ref[sl, :] += pv
        return it + 1, _largest(runs)

    lax.while_loop(more, step, (jnp.int32(0), _largest(runs)))
    for h, sl in zip(heads, chain):
        o_ref[:, h * HEAD_DIM:(h + 1) * HEAD_DIM] = acc_ref[sl, :].astype(o_ref.dtype)


def _sb_attn_cached(q, k_new, v_new, k_past, v_past):
    b, _, t, _ = q.shape
    past = k_past.shape[2]
    assert past % SB_BLOCK == 0 and t <= SB_BLOCK and t % 16 == 0, (past, t)
    new = pl.BlockSpec((None, SB_HEADS, t, HEAD_DIM), lambda bi: (bi, 0, 0, 0))
    old = pl.BlockSpec((None, SB_HEADS, past, HEAD_DIM), lambda bi: (bi, 0, 0, 0))
    return pl.pallas_call(
        functools.partial(_sb_cached_body, t=t, past=past),
        grid=(b,),
        in_specs=[new, new, new, old, old],
        out_specs=pl.BlockSpec((None, t, D_TOK), lambda bi: (bi, 0, 0)),
        out_shape=jax.ShapeDtypeStruct((b, t, D_TOK), BF16),
        scratch_shapes=[pltpu.VMEM((SB_HEADS * t, HEAD_DIM), F32),
                        pltpu.VMEM((SB_HEADS * t, HEAD_DIM), F32),
                        pltpu.VMEM((2 * SB_BLOCK, SB_BLOCK + HEAD_DIM), BF16)],
        compiler_params=_params("parallel"),
        name="sb_attn_cached",
    )(q, k_new, v_new, k_past, v_past)


def _mix_out_body(x_ref, mix_ref, qm_ref, mk_ref, mv_ref, qg_ref, w_ref, o_ref):
    bb = mk_ref.shape[0]
    tt = x_ref.shape[0] // bb
    probs = []
    for bi in range(bb):
        rows = slice(bi * tt, (bi + 1) * tt)
        for hd in range(MEM_HEADS):
            sl = slice(hd * HEAD_DIM, (hd + 1) * HEAD_DIM)
            qh = _rms(qm_ref[rows, sl], qg_ref[...]).astype(BF16)
            s = _dot_nt(qh, mk_ref[bi, :, sl].astype(BF16)) * INV_SQRT_HD
            p = jnp.exp(s - jnp.max(s, axis=-1, keepdims=True))
            probs.append((p / jnp.sum(p, axis=-1, keepdims=True)).astype(BF16))
    y = x_ref[...] + _dot(mix_ref[...], w_ref[:D_TOK, :])
    mo = jnp.concatenate(
        [jnp.concatenate([_dot(probs[bi * MEM_HEADS + hd],
                               mv_ref[bi, :, hd * HEAD_DIM:(hd + 1) * HEAD_DIM].astype(BF16))
                          for hd in range(MEM_HEADS)], axis=1) for bi in range(bb)], axis=0)
    o_ref[...] = y + _dot(mo.astype(BF16), w_ref[D_TOK:, :])


def _mix_out(x, mix, qm_src, qm_col, mem_k, mem_v, q_gain, w_out, layer, t):
    n, d = x.shape
    n_mem = mem_k.shape[2]
    tm = min(n, ROW_TILE)
    if t >= tm:
        per_seq = t // tm
        mem = pl.BlockSpec((None, 1, n_mem, D_MEM), lambda i: (layer, i // per_seq, 0, 0))
    else:
        mem = pl.BlockSpec((None, tm // t, n_mem, D_MEM), lambda i: (layer, i, 0, 0),
                           pipeline_mode=pl.Buffered(1))
    row = lambda c, col=0: pl.BlockSpec((tm, c), lambda i: (i, col))
    return pl.pallas_call(
        _mix_out_body,
        grid=(n // tm,),
        in_specs=[row(d), row(D_TOK), row(D_MEM, qm_col), mem, mem,
                  pl.BlockSpec((1, HEAD_DIM), lambda i: (0, 0)), _resident((d, d), layer)],
        out_specs=row(d),
        out_shape=jax.ShapeDtypeStruct((n, d), F32),
        compiler_params=_params("parallel"),
        name="mix_out",
    )(x, mix, qm_src, mem_k, mem_v, q_gain.reshape(1, HEAD_DIM), w_out)


def _ffn_body(x_ref, g_ref, wu_ref, wd_ref, o_ref, *rest):
    h_ref = rest[-1]

    @pl.when(pl.program_id(1) == 0)
    def _():
        x = x_ref[...]
        h_ref[...] = _rms(x, g_ref[...]).astype(BF16)
        o_ref[...] = x

    wu = wu_ref[...].astype(BF16)
    wd = wd_ref[...].astype(BF16)
    if len(rest) == 3:
        rest[0][...] = wu
        rest[1][...] = wd
    pieces = max(1, wu.shape[1] // 1024)
    half = wu.shape[1] // pieces
    for c in range(pieces):
        cols = slice(c * half, (c + 1) * half)
        up = jnp.maximum(_dot(h_ref[...], wu[:, cols]), 0.0)
        o_ref[...] += _dot((up * up).astype(BF16), wd[cols, :])


def _ffn(x, g, w_up, w_down):
    n, d = x.shape
    f = w_up.shape[1]
    tm = min(n, ROW_TILE)
    tk = 2048
    return pl.pallas_call(
        _ffn_body,
        grid=(n // tm, f // tk),
        in_specs=[
            pl.BlockSpec((tm, d), lambda i, k: (i, 0)),
            pl.BlockSpec((1, d), lambda i, k: (0, 0)),
            pl.BlockSpec((d, tk), lambda i, k: (0, k)),
            pl.BlockSpec((tk, d), lambda i, k: (k, 0)),
        ],
        out_specs=pl.BlockSpec((tm, d), lambda i, k: (i, 0)),
        out_shape=jax.ShapeDtypeStruct((n, d), F32),
        scratch_shapes=[pltpu.VMEM((tm, d), BF16)],
        compiler_params=pltpu.CompilerParams(dimension_semantics=("parallel", "arbitrary"),
                                             vmem_limit_bytes=BIG_VMEM_LIMIT),
        name="ffn",
    )(x, g.reshape(1, d), w_up, w_down)


def _ffn_casting(x, g, w_up, w_down, layer):
    n, d = x.shape
    f = w_up.shape[2]
    tk = 512
    assert n <= ROW_TILE, n
    return pl.pallas_call(
        _ffn_body,
        grid=(1, f // tk),
        in_specs=[
            pl.BlockSpec((n, d), lambda i, k: (0, 0)),
            pl.BlockSpec((1, d), lambda i, k: (0, 0)),
            pl.BlockSpec((None, d, tk), lambda i, k: (layer, 0, k)),
            pl.BlockSpec((None, tk, d), lambda i, k: (layer, k, 0)),
        ],
        out_specs=[pl.BlockSpec((n, d), lambda i, k: (0, 0)),
                   pl.BlockSpec((d, tk), lambda i, k: (0, k)),
                   pl.BlockSpec((tk, d), lambda i, k: (k, 0))],
        out_shape=[jax.ShapeDtypeStruct((n, d), F32), jax.ShapeDtypeStruct((d, f), BF16),
                   jax.ShapeDtypeStruct((f, d), BF16)],
        scratch_shapes=[pltpu.VMEM((n, d), BF16)],
        compiler_params=_params("arbitrary", "arbitrary"),
        name="ffn_casting",
    )(x, g.reshape(1, d), w_up, w_down)


def _run_trunk(x, past, pool_prev, sb_past_k, sb_past_v, mem_k, mem_v, p, ffn_bf16):
    b, t, d = x.shape
    n = b * t
    depth = p["w_out"].shape[0]
    pool_new, sbk_new, sbv_new, made = [], [], [], []
    x2 = x.reshape(n, d)
    for i in range(depth):
        j = i // 2
        if i % 2 == 0:
            proj = _norm_matmul(x2, p["norm_mix"][i], p["w_in_pool"], j)
            proj3 = proj.reshape(b, t, d)
            mix = _pool_mix(proj3, pool_prev[j], p["pool_w"][j], p["pool_scale"][j], past)
            pool_new.append(proj3[:, t - POOL_STATE:, :D_TOK])
            qm_src, qm_col = proj, D_TOK // D_MEM
        else:
            q, k, v, kb, vb, qm = _inproj_sb(x2, p["norm_mix"][i], p["w_in_sb"], j, b, t)
            if sb_past_k is None:
                mix = _sb_attn_prompt(q, kb, vb)
            else:
                mix = _sb_attn_cached(q, kb, vb, sb_past_k[j], sb_past_v[j])
            sbk_new.append(k)
            sbv_new.append(v)
            qm_src, qm_col = qm, 0
        x2 = _mix_out(x2, mix.reshape(n, D_TOK), qm_src, qm_col, mem_k, mem_v, p["q_norm"][i], p["w_out"], i, t)
        if ffn_bf16 is None:
            x2, wu, wd = _ffn_casting(x2, p["norm_ffn"][i], p["w_up"], p["w_down"], i)
            made.append((wu, wd))
        else:
            x2 = _ffn(x2, p["norm_ffn"][i], *ffn_bf16[i])
    return x2.reshape(b, t, d), jnp.stack(pool_new), jnp.stack(sbk_new), jnp.stack(sbv_new), made


def kernel(x_prompt, x_sample, mem_prompt, cache_pool, cache_sb_k, cache_sb_v, cache_mem_k, cache_mem_v,
           norm_mix, w_in_pool, w_in_sb, pool_w, pool_scale, norm_mem, w_mem_kv, q_norm, k_norm, w_out,
           norm_ffn, w_up, w_down):
    b, t, d = x_prompt.shape
    depth = w_out.shape[0]
    n_mem = mem_prompt.shape[1]
    p = dict(norm_mix=norm_mix, pool_scale=pool_scale, q_norm=q_norm, norm_ffn=norm_ffn,
             w_in_pool=w_in_pool.astype(BF16), w_in_sb=w_in_sb.astype(BF16), pool_w=pool_w.astype(BF16),
             w_out=w_out.astype(BF16), w_up=w_up, w_down=w_down)

    bs = x_sample.shape[0]
    past = cache_sb_k.shape[2]
    pool_prev = jnp.pad(cache_pool, ((0, 0), (0, 0), (POOL_HALO - POOL_STATE, 0), (0, 0)))
    head_major = lambda a: jnp.transpose(a, (0, 1, 3, 2, 4))
    y_sample, pool_sample, sb_k_sample, sb_v_sample, ffn_bf16 = _run_trunk(
        x_sample, past, pool_prev, head_major(cache_sb_k), head_major(cache_sb_v),
        cache_mem_k.reshape(depth, bs, n_mem, D_MEM), cache_mem_v.reshape(depth, bs, n_mem, D_MEM), p, None)

    mk, mv = _mem_kv(mem_prompt.reshape(b * n_mem, d), norm_mem, w_mem_kv.astype(BF16), k_norm)
    mk = mk.reshape(depth, b, n_mem, D_MEM)
    mv = mv.reshape(depth, b, n_mem, D_MEM)
    pool_zero = jnp.zeros((cache_pool.shape[0], b, POOL_HALO, D_TOK), F32)
    y_prompt, pool_prompt, sb_k_prompt, sb_v_prompt, _ = _run_trunk(
        x_prompt, 0, pool_zero, None, None, mk, mv, p, ffn_bf16)

    heads = lambda a: a.reshape(depth, b, n_mem, MEM_HEADS, HEAD_DIM)
    return (y_prompt, y_sample, pool_prompt, pool_sample, head_major(sb_k_prompt), head_major(sb_v_prompt),
            head_major(sb_k_sample), head_major(sb_v_sample), heads(mk), heads(mv))
```

```python
import functools
import math

import jax
import jax.numpy as jnp
from jax import lax
from jax.experimental import pallas as pl
from jax.experimental.pallas import tpu as pltpu

D_MODEL = 2048
HEAD_DIM = 128
D_TOK = 3 * D_MODEL // 4
D_MEM = D_MODEL // 4
SB_HEADS = D_TOK // HEAD_DIM
MEM_HEADS = D_MEM // HEAD_DIM
POOL_WINDOWS = (2, 4, 8, 16)
POOL_GROUP = D_TOK // len(POOL_WINDOWS)
POOL_STATE = max(POOL_WINDOWS) - 1
POOL_HALO = POOL_STATE + 1
D_FF = 4 * D_MODEL
EPS = 1e-6
INV_SQRT_HD = 1.0 / math.sqrt(HEAD_DIM)

SB_BLOCK = 128
SB_LATE_ROWS = 64
SB_GROUP = 32
EXP_ZERO_BELOW = -104.0
RUN_FINISHED = -1e30
ROW_TILE = 512
VMEM_LIMIT = 48 * 1024 * 1024
BIG_VMEM_LIMIT = 60 * 1024 * 1024

BF16 = jnp.bfloat16
F32 = jnp.float32


def _params(*sem):
    return pltpu.CompilerParams(dimension_semantics=sem, vmem_limit_bytes=VMEM_LIMIT)


def _rms(xf, g):
    return xf * lax.rsqrt(jnp.mean(xf * xf, axis=-1, keepdims=True) + EPS) * g


def _dot(a, b):
    return jnp.dot(a, b, preferred_element_type=F32)


def _dot_nt(a, b):
    return lax.dot_general(a, b, (((1,), (1,)), ((), ())), preferred_element_type=F32)


def _resident(shape, layer):
    return pl.BlockSpec((None,) + shape, lambda *_: (layer,) + (0,) * len(shape), pipeline_mode=pl.Buffered(1))


def _norm_matmul_body(x_ref, g_ref, w_ref, o_ref):
    o_ref[...] = _dot(_rms(x_ref[...], g_ref[...]).astype(BF16), w_ref[...])


def _norm_matmul(x, g, w, layer):
    n, d = x.shape
    m = w.shape[2]
    tm = min(n, ROW_TILE)
    return pl.pallas_call(
        _norm_matmul_body,
        grid=(n // tm,),
        in_specs=[
            pl.BlockSpec((tm, d), lambda i: (i, 0)),
            pl.BlockSpec((1, d), lambda i: (0, 0)),
            _resident((d, m), layer),
        ],
        out_specs=pl.BlockSpec((tm, m), lambda i: (i, 0)),
        out_shape=jax.ShapeDtypeStruct((n, m), F32),
        compiler_params=_params("parallel"),
        name="norm_matmul",
    )(x, g.reshape(1, d), w)


def _inproj_sb_body(x_ref, g_ref, wq_ref, wk_ref, wv_ref, wm_ref,
                    q_ref, k_ref, v_ref, kb_ref, vb_ref, qm_ref, h_ref):
    @pl.when(pl.program_id(1) == 0)
    def _():
        h_ref[...] = _rms(x_ref[...], g_ref[...]).astype(BF16)
        qm_ref[...] = _dot(h_ref[...], wm_ref[...])

    def put(refs, val):
        bb, heads, tt, _ = refs[0].shape
        for bi in range(bb):
            for hd in range(heads):
                piece = val[bi * tt:(bi + 1) * tt, hd * HEAD_DIM:(hd + 1) * HEAD_DIM]
                for ref in refs:
                    ref[bi, hd] = piece.astype(ref.dtype)

    h = h_ref[...]
    put([q_ref], _dot(h, wq_ref[...]) * INV_SQRT_HD)
    put([k_ref, kb_ref], _dot(h, wk_ref[...]))
    put([v_ref, vb_ref], _dot(h, wv_ref[...]))


def _inproj_sb(x, g, w, layer, b, t):
    n, d = x.shape
    tm = min(n, 2 * ROW_TILE)
    tn = D_MEM
    nj = D_TOK // tn
    hj = tn // HEAD_DIM
    if t >= tm:
        per_seq = t // tm
        oblock, omap = (1, hj, tm, HEAD_DIM), lambda i, j: (i // per_seq, j, i % per_seq, 0)
    else:
        oblock, omap = (tm // t, hj, t, HEAD_DIM), lambda i, j: (i, j, 0, 0)
    tok = lambda dt: jax.ShapeDtypeStruct((b, SB_HEADS, t, HEAD_DIM), dt)
    wspec = lambda off: pl.BlockSpec((None, d, tn), lambda i, j: (layer, 0, off + j))
    ospec = pl.BlockSpec(oblock, omap)
    return pl.pallas_call(
        _inproj_sb_body,
        grid=(n // tm, nj),
        in_specs=[
            pl.BlockSpec((tm, d), lambda i, j: (i, 0)),
            pl.BlockSpec((1, d), lambda i, j: (0, 0)),
            wspec(0), wspec(nj), wspec(2 * nj),
            pl.BlockSpec((None, d, D_MEM), lambda i, j: (layer, 0, 3 * D_TOK // D_MEM),
                         pipeline_mode=pl.Buffered(1)),
        ],
        out_specs=[ospec, ospec, ospec, ospec, ospec,
                   pl.BlockSpec((tm, D_MEM), lambda i, j: (i, 0))],
        out_shape=[tok(BF16), tok(F32), tok(F32), tok(BF16), tok(BF16),
                   jax.ShapeDtypeStruct((n, D_MEM), F32)],
        scratch_shapes=[pltpu.VMEM((tm, d), BF16)],
        compiler_params=pltpu.CompilerParams(dimension_semantics=("parallel", "arbitrary"),
                                             vmem_limit_bytes=BIG_VMEM_LIMIT),
        name="inproj_sb",
    )(x, g.reshape(1, d), w, w, w, w)


def _head_rows(hd, n):
    return pl.ds(hd, n, stride=MEM_HEADS)


def _mem_kv_body(x_ref, g_ref, w_ref, kg_ref, k_ref, v_ref):
    tm = x_ref.shape[0]
    h = _rms(x_ref[...], g_ref[...]).astype(BF16)
    kv = _dot(h, w_ref[...].astype(BF16))
    for hd in range(MEM_HEADS):
        sl = slice(hd * HEAD_DIM, (hd + 1) * HEAD_DIM)
        k_ref[_head_rows(hd, tm), :] = _rms(kv[:, sl], kg_ref[...])
        v_ref[_head_rows(hd, tm), :] = kv[:, D_MEM + hd * HEAD_DIM:D_MEM + (hd + 1) * HEAD_DIM]


def _mem_kv(mem, g_mem, w_kv, k_gain):
    r, d = mem.shape
    depth = w_kv.shape[0]
    tm = min(r, 256)
    out = jax.ShapeDtypeStruct((depth, r * MEM_HEADS, HEAD_DIM), F32)
    ospec = pl.BlockSpec((None, tm * MEM_HEADS, HEAD_DIM), lambda l, i: (l, i, 0))
    return pl.pallas_call(
        _mem_kv_body,
        grid=(depth, r // tm),
        in_specs=[
            pl.BlockSpec((tm, d), lambda l, i: (i, 0)),
            pl.BlockSpec((None, 1, d), lambda l, i: (l, 0, 0)),
            pl.BlockSpec((None, d, 2 * D_MEM), lambda l, i: (l, 0, 0)),
            pl.BlockSpec((None, 1, HEAD_DIM), lambda l, i: (l, 0, 0)),
        ],
        out_specs=[ospec, ospec],
        out_shape=[out, out],
        compiler_params=_params("parallel", "parallel"),
        name="mem_kv",
    )(mem, g_mem.reshape(depth, 1, d), w_kv, k_gain.reshape(depth, 1, HEAD_DIM))


def _pool_body(u_ref, halo_ref, prev_ref, w_ref, s_ref, o_ref, ext_ref, a_ref, b_ref, *, tm, past):
    i = pl.program_id(1)
    base = 2 * POOL_HALO
    rows = tm + POOL_HALO
    ext_ref[0:POOL_HALO, :] = jnp.zeros((POOL_HALO, D_TOK), F32)
    a_ref[0:POOL_HALO, :] = jnp.zeros((POOL_HALO, POOL_GROUP), F32)
    b_ref[0:POOL_HALO, :] = jnp.zeros((POOL_HALO, POOL_GROUP), F32)

    @pl.when(i == 0)
    def _():
        ext_ref[POOL_HALO:base, :] = prev_ref[...]

    @pl.when(i > 0)
    def _():
        ext_ref[POOL_HALO:base, :] = halo_ref[...]

    ext_ref[base:, :] = u_ref[...]
    pos = (past + i * tm + lax.broadcasted_iota(jnp.int32, (tm, 1), 0)).astype(F32)
    for g, w in enumerate(POOL_WINDOWS):
        sl = slice(g * POOL_GROUP, (g + 1) * POOL_GROUP)
        src, cols, nxt = ext_ref, sl, (a_ref, b_ref)
        span = 1
        while span < w:
            part = src[POOL_HALO:POOL_HALO + rows, cols] + src[POOL_HALO - span:POOL_HALO - span + rows, cols]
            span *= 2
            if span < w:
                nxt[0][POOL_HALO:POOL_HALO + rows, :] = part
                src, cols, nxt = nxt[0], slice(None), nxt[::-1]
        win = part[POOL_HALO:, :]
        cur = ext_ref[base:, sl]
        cnt = jnp.minimum(pos + 1.0, float(w))
        dlt = (win / cnt - cur).astype(BF16)
        o_ref[:, sl] = (_dot(dlt, w_ref[g]) * s_ref[:, sl]).astype(BF16)


def _pool_mix(proj, prev, w_grp, scale, past):
    b, t, _ = proj.shape
    tm = min(t, 512)
    hb = tm // POOL_HALO
    return pl.pallas_call(
        functools.partial(_pool_body, tm=tm, past=past),
        grid=(b, t // tm),
        in_specs=[
            pl.BlockSpec((None, tm, D_TOK), lambda bi, i: (bi, i, 0)),
            pl.BlockSpec((None, POOL_HALO, D_TOK), lambda bi, i: (bi, jnp.maximum(i * hb - 1, 0), 0)),
            pl.BlockSpec((None, POOL_HALO, D_TOK), lambda bi, i: (bi, 0, 0)),
            pl.BlockSpec((len(POOL_WINDOWS), POOL_GROUP, POOL_GROUP), lambda bi, i: (0, 0, 0)),
            pl.BlockSpec((1, D_TOK), lambda bi, i: (0, 0)),
        ],
        out_specs=pl.BlockSpec((None, tm, D_TOK), lambda bi, i: (bi, i, 0)),
        out_shape=jax.ShapeDtypeStruct((b, t, D_TOK), BF16),
        scratch_shapes=[pltpu.VMEM((tm + 2 * POOL_HALO, D_TOK), F32),
                        pltpu.VMEM((tm + 2 * POOL_HALO, POOL_GROUP), F32),
                        pltpu.VMEM((tm + 2 * POOL_HALO, POOL_GROUP), F32)],
        compiler_params=_params("parallel", "arbitrary"),
        name="pool_mix",
    )(proj, proj, prev, w_grp, scale.reshape(1, D_TOK))


def _scan_matrix():
    shape = (2 * SB_BLOCK, SB_BLOCK + HEAD_DIM)
    r = lax.broadcasted_iota(jnp.int32, shape, 0) % SB_BLOCK
    c = lax.broadcasted_iota(jnp.int32, shape, 1)
    return jnp.where((c >= SB_BLOCK) | (r > c), 1.0, 0.0).astype(BF16)


def _sb_steps(qs, ks, vs, runs, mask, scan):
    tq = qs[0].shape[0]
    zs = [_dot_nt(q, k) for q, k in zip(qs, ks)]
    fails, parts = [], []
    for z in zs:
        log_fail = -(jnp.maximum(z, 0.0) + jnp.log(1.0 + jnp.exp(-jnp.abs(z))))
        if mask is not None:
            log_fail = jnp.where(mask, log_fail, 0.0)
        hi = log_fail.astype(BF16)
        lo = (log_fail - hi.astype(F32)).astype(BF16)
        fails.append(log_fail)
        parts.append(jnp.concatenate([hi, lo], axis=1))
    sums = _dot(jnp.concatenate(parts, axis=0), scan)
    weights, new_runs = [], []
    for g, (z, log_fail, run) in enumerate(zip(zs, fails, runs)):
        s = sums[g * tq:(g + 1) * tq]
        a = jnp.exp(z + log_fail + s[:, :SB_BLOCK] + run)
        if mask is not None:
            a = jnp.where(mask, a, 0.0)
        weights.append(a.astype(BF16))
        new_runs.append(run + s[:, SB_BLOCK:])
    return new_runs, [_dot(a, v) for a, v in zip(weights, vs)]


def _largest(values):
    out = values[0]
    for v in values[1:]:
        out = jnp.maximum(out, v)
    return jnp.max(out)


def _sb_prompt_body(q_ref, k_ref, v_ref, o_ref, run_ref, acc_ref, scan_ref, *, t, group):
    scan_ref[...] = _scan_matrix()
    below_diag = (lax.broadcasted_iota(jnp.int32, (SB_BLOCK, SB_BLOCK), 1) <
                  lax.broadcasted_iota(jnp.int32, (SB_BLOCK, SB_BLOCK), 0))
    zero = jnp.zeros((SB_BLOCK, HEAD_DIM), F32)
    chain = [slice(g * SB_BLOCK, (g + 1) * SB_BLOCK) for g in range(group)]

    def block(i):
        return pl.ds(pl.multiple_of(i * SB_BLOCK, SB_BLOCK), SB_BLOCK)

    def group_scan(s, _):
        first = s * group
        tiles = [block(first + g) for g in range(group)]
        runs, pvs = _sb_steps([q_ref[r, :] for r in tiles], [k_ref[r, :] for r in tiles],
                              [v_ref[r, :] for r in tiles], [zero] * group, below_diag, scan_ref[...])
        for sl, run, pv in zip(chain, runs, pvs):
            run_ref[sl, :] = run
            acc_ref[sl, :] = pv

        def split(runs):
            return (_largest([r[:SB_LATE_ROWS] for r in runs]), _largest([r[SB_LATE_ROWS:] for r in runs]))

        def visit(it, n):
            keys = [block(jnp.maximum(first + g - it, 0)) for g in range(group)]
            rows = [slice(g * SB_BLOCK, g * SB_BLOCK + n) for g in range(group)]
            runs = [jnp.where(first + g - it >= 0, run_ref[r, :], RUN_FINISHED) for g, r in enumerate(rows)]
            queries = [q_ref[pl.ds(pl.multiple_of((first + g) * SB_BLOCK, SB_BLOCK), n), :] for g in range(group)]
            runs, pvs = _sb_steps(queries, [k_ref[r, :] for r in keys], [v_ref[r, :] for r in keys], runs, None,
                                  scan_ref[...])
            for r, run, pv in zip(rows, runs, pvs):
                run_ref[r, :] = run
                acc_ref[r, :] += pv
            return runs

        def more(c):
            it, late, rest = c
            return jnp.logical_and(it < first + group, jnp.maximum(late, rest) > EXP_ZERO_BELOW)

        def step(c):
            it, _, rest = c
            late, rest = lax.cond(rest > EXP_ZERO_BELOW,
                                  lambda: split(visit(it, SB_BLOCK)),
                                  lambda: (_largest(visit(it, SB_LATE_ROWS)), rest))
            return it + 1, late, rest

        lax.while_loop(more, step, (jnp.int32(1),) + split(runs))
        out_rows = pl.ds(pl.multiple_of(first * SB_BLOCK, SB_BLOCK), group * SB_BLOCK)
        o_ref[out_rows, :] = acc_ref[...].astype(o_ref.dtype)
        return 0

    lax.fori_loop(0, t // (group * SB_BLOCK), group_scan, 0)


def _sb_attn_prompt(q, k, v):
    b, _, t, _ = q.shape
    group = min(SB_GROUP, t // SB_BLOCK)
    assert t % (group * SB_BLOCK) == 0, t
    spec = pl.BlockSpec((None, None, t, HEAD_DIM), lambda bi, h: (bi, h, 0, 0))
    return pl.pallas_call(
        functools.partial(_sb_prompt_body, t=t, group=group),
        grid=(b, SB_HEADS),
        in_specs=[spec, spec, spec],
        out_specs=pl.BlockSpec((None, t, HEAD_DIM), lambda bi, h: (bi, 0, h)),
        out_shape=jax.ShapeDtypeStruct((b, t, D_TOK), BF16),
        scratch_shapes=[pltpu.VMEM((group * SB_BLOCK, HEAD_DIM), F32),
                        pltpu.VMEM((group * SB_BLOCK, HEAD_DIM), F32),
                        pltpu.VMEM((2 * SB_BLOCK, SB_BLOCK + HEAD_DIM), BF16)],
        compiler_params=_params("parallel", "parallel"),
        name="sb_attn_prompt",
    )(q, k, v)


def _sb_cached_body(q_ref, kn_ref, vn_ref, kc_ref, vc_ref, o_ref, run_ref, acc_ref, scan_ref, *, t, past):
    scan_ref[...] = _scan_matrix()
    earlier = (lax.broadcasted_iota(jnp.int32, (t, SB_BLOCK), 1) <
               lax.broadcasted_iota(jnp.int32, (t, SB_BLOCK), 0))
    zero = jnp.zeros((t, HEAD_DIM), F32)
    pad = jnp.zeros((SB_BLOCK - t, HEAD_DIM), BF16)
    heads = range(SB_HEADS)
    chain = [slice(h * t, (h + 1) * t) for h in heads]

    runs, pvs = _sb_steps([q_ref[h] for h in heads],
                          [jnp.concatenate([kn_ref[h], pad], axis=0) for h in heads],
                          [jnp.concatenate([vn_ref[h], pad], axis=0) for h in heads],
                          [zero] * SB_HEADS, earlier, scan_ref[...])
    for sl, run, pv in zip(chain, runs, pvs):
        run_ref[sl, :] = run
        acc_ref[sl, :] = pv

    def more(c):
        it, largest_run = c
        return jnp.logical_and(it < past // SB_BLOCK, largest_run > EXP_ZERO_BELOW)

    def step(c):
        it, _ = c
        keys = pl.ds(pl.multiple_of(past - (it + 1) * SB_BLOCK, SB_BLOCK), SB_BLOCK)
        runs, pvs = _sb_steps([q_ref[h] for h in heads], [kc_ref[h, keys, :].astype(BF16) for h in heads],
                              [vc_ref[h, keys, :].astype(BF16) for h in heads], [run_ref[sl, :] for sl in chain],
                              None, scan_ref[...])
        for sl, run, pv in zip(chain, runs, pvs):
            run_ref[sl, :] = run
            acc_ref[sl, :] += pv
        return it + 1, _largest(runs)

    lax.while_loop(more, step, (jnp.int32(0), _largest(runs)))
    for h, sl in zip(heads, chain):
        o_ref[:, h * HEAD_DIM:(h + 1) * HEAD_DIM] = acc_ref[sl, :].astype(o_ref.dtype)


def _sb_attn_cached(q, k_new, v_new, k_past, v_past):
    b, _, t, _ = q.shape
    past = k_past.shape[2]
    assert past % SB_BLOCK == 0 and t <= SB_BLOCK and t % 16 == 0, (past, t)
    new = pl.BlockSpec((None, SB_HEADS, t, HEAD_DIM), lambda bi: (bi, 0, 0, 0))
    old = pl.BlockSpec((None, SB_HEADS, past, HEAD_DIM), lambda bi: (bi, 0, 0, 0))
    return pl.pallas_call(
        functools.partial(_sb_cached_body, t=t, past=past),
        grid=(b,),
        in_specs=[new, new, new, old, old],
        out_specs=pl.BlockSpec((None, t, D_TOK), lambda bi: (bi, 0, 0)),
        out_shape=jax.ShapeDtypeStruct((b, t, D_TOK), BF16),
        scratch_shapes=[pltpu.VMEM((SB_HEADS * t, HEAD_DIM), F32),
                        pltpu.VMEM((SB_HEADS * t, HEAD_DIM), F32),
                        pltpu.VMEM((2 * SB_BLOCK, SB_BLOCK + HEAD_DIM), BF16)],
        compiler_params=_params("parallel"),
        name="sb_attn_cached",
    )(q, k_new, v_new, k_past, v_past)


def _mix_out_body(x_ref, mix_ref, qm_ref, mk_ref, mv_ref, qg_ref, w_ref, o_ref):
    bb = mk_ref.shape[0]
    n_mem = mk_ref.shape[1] // MEM_HEADS
    tt = x_ref.shape[0] // bb
    probs = []
    for bi in range(bb):
        rows = slice(bi * tt, (bi + 1) * tt)
        for hd in range(MEM_HEADS):
            sl = slice(hd * HEAD_DIM, (hd + 1) * HEAD_DIM)
            qh = _rms(qm_ref[rows, sl], qg_ref[...]).astype(BF16)
            s = _dot_nt(qh, mk_ref[bi, _head_rows(hd, n_mem), :].astype(BF16)) * INV_SQRT_HD
            p = jnp.exp(s - jnp.max(s, axis=-1, keepdims=True))
            probs.append((p / jnp.sum(p, axis=-1, keepdims=True)).astype(BF16))
    y = x_ref[...] + _dot(mix_ref[...], w_ref[:D_TOK, :])
    mo = jnp.concatenate(
        [jnp.concatenate([_dot(probs[bi * MEM_HEADS + hd], mv_ref[bi, _head_rows(hd, n_mem), :].astype(BF16))
                          for hd in range(MEM_HEADS)], axis=1) for bi in range(bb)], axis=0)
    o_ref[...] = y + _dot(mo.astype(BF16), w_ref[D_TOK:, :])


def _mix_out(x, mix, qm_src, qm_col, mem_k, mem_v, q_gain, w_out, layer, t):
    n, d = x.shape
    mem_rows = mem_k.shape[2]
    tm = min(n, ROW_TILE)
    if t >= tm:
        per_seq = t // tm
        mem = pl.BlockSpec((None, 1, mem_rows, HEAD_DIM), lambda i: (layer, i // per_seq, 0, 0))
    else:
        mem = pl.BlockSpec((None, tm // t, mem_rows, HEAD_DIM), lambda i: (layer, i, 0, 0),
                           pipeline_mode=pl.Buffered(1))
    row = lambda c, col=0: pl.BlockSpec((tm, c), lambda i: (i, col))
    return pl.pallas_call(
        _mix_out_body,
        grid=(n // tm,),
        in_specs=[row(d), row(D_TOK), row(D_MEM, qm_col), mem, mem,
                  pl.BlockSpec((1, HEAD_DIM), lambda i: (0, 0)), _resident((d, d), layer)],
        out_specs=row(d),
        out_shape=jax.ShapeDtypeStruct((n, d), F32),
        compiler_params=_params("parallel"),
        name="mix_out",
    )(x, mix, qm_src, mem_k, mem_v, q_gain.reshape(1, HEAD_DIM), w_out)


def _ffn_body(x_ref, g_ref, wu_ref, wd_ref, o_ref, *rest):
    h_ref = rest[-1]

    @pl.when(pl.program_id(1) == 0)
    def _():
        x = x_ref[...]
        h_ref[...] = _rms(x, g_ref[...]).astype(BF16)
        o_ref[...] = x

    wu = wu_ref[...].astype(BF16)
    wd = wd_ref[...].astype(BF16)
    if len(rest) == 3:
        rest[0][...] = wu
        rest[1][...] = wd
    pieces = max(1, wu.shape[1] // 1024)
    half = wu.shape[1] // pieces
    for c in range(pieces):
        cols = slice(c * half, (c + 1) * half)
        up = jnp.maximum(_dot(h_ref[...], wu[:, cols]), 0.0)
        o_ref[...] += _dot((up * up).astype(BF16), wd[cols, :])


def _ffn(x, g, w_up, w_down):
    n, d = x.shape
    f = w_up.shape[1]
    tm = min(n, ROW_TILE)
    tk = 2048
    return pl.pallas_call(
        _ffn_body,
        grid=(n // tm, f // tk),
        in_specs=[
            pl.BlockSpec((tm, d), lambda i, k: (i, 0)),
            pl.BlockSpec((1, d), lambda i, k: (0, 0)),
            pl.BlockSpec((d, tk), lambda i, k: (0, k)),
            pl.BlockSpec((tk, d), lambda i, k: (k, 0)),
        ],
        out_specs=pl.BlockSpec((tm, d), lambda i, k: (i, 0)),
        out_shape=jax.ShapeDtypeStruct((n, d), F32),
        scratch_shapes=[pltpu.VMEM((tm, d), BF16)],
        compiler_params=pltpu.CompilerParams(dimension_semantics=("parallel", "arbitrary"),
                                             vmem_limit_bytes=BIG_VMEM_LIMIT),
        name="ffn",
    )(x, g.reshape(1, d), w_up, w_down)


def _ffn_casting(x, g, w_up, w_down, layer):
    n, d = x.shape
    f = w_up.shape[2]
    tk = 512
    assert n <= ROW_TILE, n
    return pl.pallas_call(
        _ffn_body,
        grid=(1, f // tk),
        in_specs=[
            pl.BlockSpec((n, d), lambda i, k: (0, 0)),
            pl.BlockSpec((1, d), lambda i, k: (0, 0)),
            pl.BlockSpec((None, d, tk), lambda i, k: (layer, 0, k)),
            pl.BlockSpec((None, tk, d), lambda i, k: (layer, k, 0)),
        ],
        out_specs=[pl.BlockSpec((n, d), lambda i, k: (0, 0)),
                   pl.BlockSpec((d, tk), lambda i, k: (0, k)),
                   pl.BlockSpec((tk, d), lambda i, k: (k, 0))],
        out_shape=[jax.ShapeDtypeStruct((n, d), F32), jax.ShapeDtypeStruct((d, f), BF16),
                   jax.ShapeDtypeStruct((f, d), BF16)],
        scratch_shapes=[pltpu.VMEM((n, d), BF16)],
        compiler_params=_params("arbitrary", "arbitrary"),
        name="ffn_casting",
    )(x, g.reshape(1, d), w_up, w_down)


def _run_trunk(x, past, pool_prev, sb_past_k, sb_past_v, mem_k, mem_v, p, ffn_bf16):
    b, t, d = x.shape
    n = b * t
    depth = p["w_out"].shape[0]
    pool_new, sbk_new, sbv_new, made = [], [], [], []
    x2 = x.reshape(n, d)
    for i in range(depth):
        j = i // 2
        if i % 2 == 0:
            proj = _norm_matmul(x2, p["norm_mix"][i], p["w_in_pool"], j)
            proj3 = proj.reshape(b, t, d)
            mix = _pool_mix(proj3, pool_prev[j], p["pool_w"][j], p["pool_scale"][j], past)
            pool_new.append(proj3[:, t - POOL_STATE:, :D_TOK])
            qm_src, qm_col = proj, D_TOK // D_MEM
        else:
            q, k, v, kb, vb, qm = _inproj_sb(x2, p["norm_mix"][i], p["w_in_sb"], j, b, t)
            if sb_past_k is None:
                mix = _sb_attn_prompt(q, kb, vb)
            else:
                mix = _sb_attn_cached(q, kb, vb, sb_past_k[j], sb_past_v[j])
            sbk_new.append(k)
            sbv_new.append(v)
            qm_src, qm_col = qm, 0
        x2 = _mix_out(x2, mix.reshape(n, D_TOK), qm_src, qm_col, mem_k, mem_v, p["q_norm"][i], p["w_out"], i, t)
        if ffn_bf16 is None:
            x2, wu, wd = _ffn_casting(x2, p["norm_ffn"][i], p["w_up"], p["w_down"], i)
            made.append((wu, wd))
        else:
            x2 = _ffn(x2, p["norm_ffn"][i], *ffn_bf16[i])
    return x2.reshape(b, t, d), jnp.stack(pool_new), jnp.stack(sbk_new), jnp.stack(sbv_new), made


def kernel(x_prompt, x_sample, mem_prompt, cache_pool, cache_sb_k, cache_sb_v, cache_mem_k, cache_mem_v,
           norm_mix, w_in_pool, w_in_sb, pool_w, pool_scale, norm_mem, w_mem_kv, q_norm, k_norm, w_out,
           norm_ffn, w_up, w_down):
    b, t, d = x_prompt.shape
    depth = w_out.shape[0]
    n_mem = mem_prompt.shape[1]
    p = dict(norm_mix=norm_mix, pool_scale=pool_scale, q_norm=q_norm, norm_ffn=norm_ffn,
             w_in_pool=w_in_pool.astype(BF16), w_in_sb=w_in_sb.astype(BF16), pool_w=pool_w.astype(BF16),
             w_out=w_out.astype(BF16), w_up=w_up, w_down=w_down)

    bs = x_sample.shape[0]
    past = cache_sb_k.shape[2]
    pool_prev = jnp.pad(cache_pool, ((0, 0), (0, 0), (POOL_HALO - POOL_STATE, 0), (0, 0)))
    head_major = lambda a: jnp.transpose(a, (0, 1, 3, 2, 4))
    mem_rows = lambda a, nb: a.reshape(depth, nb, n_mem * MEM_HEADS, HEAD_DIM)
    y_sample, pool_sample, sb_k_sample, sb_v_sample, ffn_bf16 = _run_trunk(
        x_sample, past, pool_prev, head_major(cache_sb_k), head_major(cache_sb_v),
        mem_rows(cache_mem_k, bs), mem_rows(cache_mem_v, bs), p, None)

    mk, mv = _mem_kv(mem_prompt.reshape(b * n_mem, d), norm_mem, w_mem_kv, k_norm)
    mk = mem_rows(mk, b)
    mv = mem_rows(mv, b)
    pool_zero = jnp.zeros((cache_pool.shape[0], b, POOL_HALO, D_TOK), F32)
    y_prompt, pool_prompt, sb_k_prompt, sb_v_prompt, _ = _run_trunk(
        x_prompt, 0, pool_zero, None, None, mk, mv, p, ffn_bf16)

    heads = lambda a: a.reshape(depth, b, n_mem, MEM_HEADS, HEAD_DIM)
    return (y_prompt, y_sample, pool_prompt, pool_sample, head_major(sb_k_prompt), head_major(sb_v_prompt),
            head_major(sb_k_sample), head_major(sb_v_sample), heads(mk), heads(mv))
```

```python
import functools
import math

import jax
import jax.numpy as jnp
from jax import lax
from jax.experimental import pallas as pl
from jax.experimental.pallas import tpu as pltpu

D_MODEL = 2048
HEAD_DIM = 128
D_TOK = 3 * D_MODEL // 4
D_MEM = D_MODEL // 4
SB_HEADS = D_TOK // HEAD_DIM
MEM_HEADS = D_MEM // HEAD_DIM
POOL_WINDOWS = (2, 4, 8, 16)
POOL_GROUP = D_TOK // len(POOL_WINDOWS)
POOL_STATE = max(POOL_WINDOWS) - 1
POOL_HALO = POOL_STATE + 1
D_FF = 4 * D_MODEL
EPS = 1e-6
INV_SQRT_HD = 1.0 / math.sqrt(HEAD_DIM)

SB_BLOCK = 128
SB_LATE_ROWS = 64
SB_GROUP = 32
EXP_ZERO_BELOW = -104.0
RUN_FINISHED = -1e30
ROW_TILE = 512
VMEM_LIMIT = 48 * 1024 * 1024
BIG_VMEM_LIMIT = 60 * 1024 * 1024

BF16 = jnp.bfloat16
F32 = jnp.float32


def _params(*sem):
    return pltpu.CompilerParams(dimension_semantics=sem, vmem_limit_bytes=VMEM_LIMIT)


def _rms(xf, g):
    return xf * lax.rsqrt(jnp.mean(xf * xf, axis=-1, keepdims=True) + EPS) * g


def _dot(a, b):
    return jnp.dot(a, b, preferred_element_type=F32)


def _dot_nt(a, b):
    return lax.dot_general(a, b, (((1,), (1,)), ((), ())), preferred_element_type=F32)


def _resident(shape, layer):
    return pl.BlockSpec((None,) + shape, lambda *_: (layer,) + (0,) * len(shape), pipeline_mode=pl.Buffered(1))


def _norm_matmul_body(x_ref, g_ref, w_ref, o_ref):
    o_ref[...] = _dot(_rms(x_ref[...], g_ref[...]).astype(BF16), w_ref[...])


def _norm_matmul(x, g, w, layer):
    n, d = x.shape
    m = w.shape[2]
    tm = min(n, ROW_TILE)
    return pl.pallas_call(
        _norm_matmul_body,
        grid=(n // tm,),
        in_specs=[
            pl.BlockSpec((tm, d), lambda i: (i, 0)),
            pl.BlockSpec((1, d), lambda i: (0, 0)),
            _resident((d, m), layer),
        ],
        out_specs=pl.BlockSpec((tm, m), lambda i: (i, 0)),
        out_shape=jax.ShapeDtypeStruct((n, m), F32),
        compiler_params=_params("parallel"),
        name="norm_matmul",
    )(x, g.reshape(1, d), w)


def _inproj_sb_body(x_ref, g_ref, wq_ref, wk_ref, wv_ref, wm_ref,
                    q_ref, k_ref, v_ref, kb_ref, vb_ref, qm_ref, h_ref):
    @pl.when(pl.program_id(1) == 0)
    def _():
        h_ref[...] = _rms(x_ref[...], g_ref[...]).astype(BF16)
        qm_ref[...] = _dot(h_ref[...], wm_ref[...])

    def put(refs, val):
        bb, heads, tt, _ = refs[0].shape
        for bi in range(bb):
            for hd in range(heads):
                piece = val[bi * tt:(bi + 1) * tt, hd * HEAD_DIM:(hd + 1) * HEAD_DIM]
                for ref in refs:
                    ref[bi, hd] = piece.astype(ref.dtype)

    h = h_ref[...]
    put([q_ref], _dot(h, wq_ref[...]) * INV_SQRT_HD)
    put([k_ref, kb_ref], _dot(h, wk_ref[...]))
    put([v_ref, vb_ref], _dot(h, wv_ref[...]))


def _inproj_sb(x, g, w, layer, b, t):
    n, d = x.shape
    tm = min(n, 2 * ROW_TILE)
    tn = D_MEM
    nj = D_TOK // tn
    hj = tn // HEAD_DIM
    if t >= tm:
        per_seq = t // tm
        oblock, omap = (1, hj, tm, HEAD_DIM), lambda i, j: (i // per_seq, j, i % per_seq, 0)
    else:
        oblock, omap = (tm // t, hj, t, HEAD_DIM), lambda i, j: (i, j, 0, 0)
    tok = lambda dt: jax.ShapeDtypeStruct((b, SB_HEADS, t, HEAD_DIM), dt)
    wspec = lambda off: pl.BlockSpec((None, d, tn), lambda i, j: (layer, 0, off + j))
    ospec = pl.BlockSpec(oblock, omap)
    return pl.pallas_call(
        _inproj_sb_body,
        grid=(n // tm, nj),
        in_specs=[
            pl.BlockSpec((tm, d), lambda i, j: (i, 0)),
            pl.BlockSpec((1, d), lambda i, j: (0, 0)),
            wspec(0), wspec(nj), wspec(2 * nj),
            pl.BlockSpec((None, d, D_MEM), lambda i, j: (layer, 0, 3 * D_TOK // D_MEM),
                         pipeline_mode=pl.Buffered(1)),
        ],
        out_specs=[ospec, ospec, ospec, ospec, ospec,
                   pl.BlockSpec((tm, D_MEM), lambda i, j: (i, 0))],
        out_shape=[tok(BF16), tok(F32), tok(F32), tok(BF16), tok(BF16),
                   jax.ShapeDtypeStruct((n, D_MEM), F32)],
        scratch_shapes=[pltpu.VMEM((tm, d), BF16)],
        compiler_params=pltpu.CompilerParams(dimension_semantics=("parallel", "arbitrary"),
                                             vmem_limit_bytes=BIG_VMEM_LIMIT),
        name="inproj_sb",
    )(x, g.reshape(1, d), w, w, w, w)


def _head_rows(hd, n):
    return pl.ds(hd, n, stride=MEM_HEADS)


def _mem_kv_body(x_ref, g_ref, w_ref, kg_ref, k_ref, v_ref):
    tm = x_ref.shape[0]
    h = _rms(x_ref[...], g_ref[...]).astype(BF16)
    kv = _dot(h, w_ref[...].astype(BF16))
    for hd in range(MEM_HEADS):
        sl = slice(hd * HEAD_DIM, (hd + 1) * HEAD_DIM)
        k_ref[_head_rows(hd, tm), :] = _rms(kv[:, sl], kg_ref[...])
        v_ref[_head_rows(hd, tm), :] = kv[:, D_MEM + hd * HEAD_DIM:D_MEM + (hd + 1) * HEAD_DIM]


def _mem_kv(mem, g_mem, w_kv, k_gain):
    r, d = mem.shape
    depth = w_kv.shape[0]
    tm = min(r, 256)
    out = jax.ShapeDtypeStruct((depth, r * MEM_HEADS, HEAD_DIM), F32)
    ospec = pl.BlockSpec((None, tm * MEM_HEADS, HEAD_DIM), lambda l, i: (l, i, 0))
    return pl.pallas_call(
        _mem_kv_body,
        grid=(depth, r // tm),
        in_specs=[
            pl.BlockSpec((tm, d), lambda l, i: (i, 0)),
            pl.BlockSpec((None, 1, d), lambda l, i: (l, 0, 0)),
            pl.BlockSpec((None, d, 2 * D_MEM), lambda l, i: (l, 0, 0)),
            pl.BlockSpec((None, 1, HEAD_DIM), lambda l, i: (l, 0, 0)),
        ],
        out_specs=[ospec, ospec],
        out_shape=[out, out],
        compiler_params=_params("parallel", "parallel"),
        name="mem_kv",
    )(mem, g_mem.reshape(depth, 1, d), w_kv, k_gain.reshape(depth, 1, HEAD_DIM))


def _pool_body(u_ref, halo_ref, prev_ref, w_ref, s_ref, o_ref, ext_ref, a_ref, b_ref, *, tm, past):
    i = pl.program_id(1)
    base = 2 * POOL_HALO
    rows = tm + POOL_HALO
    ext_ref[0:POOL_HALO, :] = jnp.zeros((POOL_HALO, D_TOK), F32)
    a_ref[0:POOL_HALO, :] = jnp.zeros((POOL_HALO, POOL_GROUP), F32)
    b_ref[0:POOL_HALO, :] = jnp.zeros((POOL_HALO, POOL_GROUP), F32)

    @pl.when(i == 0)
    def _():
        ext_ref[POOL_HALO:base, :] = prev_ref[...]

    @pl.when(i > 0)
    def _():
        ext_ref[POOL_HALO:base, :] = halo_ref[...]

    ext_ref[base:, :] = u_ref[...]
    pos = (past + i * tm + lax.broadcasted_iota(jnp.int32, (tm, 1), 0)).astype(F32)
    for g, w in enumerate(POOL_WINDOWS):
        sl = slice(g * POOL_GROUP, (g + 1) * POOL_GROUP)
        src, cols, nxt = ext_ref, sl, (a_ref, b_ref)
        span = 1
        while span < w:
            part = src[POOL_HALO:POOL_HALO + rows, cols] + src[POOL_HALO - span:POOL_HALO - span + rows, cols]
            span *= 2
            if span < w:
                nxt[0][POOL_HALO:POOL_HALO + rows, :] = part
                src, cols, nxt = nxt[0], slice(None), nxt[::-1]
        win = part[POOL_HALO:, :]
        cur = ext_ref[base:, sl]
        cnt = jnp.minimum(pos + 1.0, float(w))
        dlt = (win / cnt - cur).astype(BF16)
        o_ref[:, sl] = (_dot(dlt, w_ref[g]) * s_ref[:, sl]).astype(BF16)


def _pool_mix(proj, prev, w_grp, scale, past):
    b, t, _ = proj.shape
    tm = min(t, 512)
    hb = tm // POOL_HALO
    return pl.pallas_call(
        functools.partial(_pool_body, tm=tm, past=past),
        grid=(b, t // tm),
        in_specs=[
            pl.BlockSpec((None, tm, D_TOK), lambda bi, i: (bi, i, 0)),
            pl.BlockSpec((None, POOL_HALO, D_TOK), lambda bi, i: (bi, jnp.maximum(i * hb - 1, 0), 0)),
            pl.BlockSpec((None, POOL_HALO, D_TOK), lambda bi, i: (bi, 0, 0)),
            pl.BlockSpec((len(POOL_WINDOWS), POOL_GROUP, POOL_GROUP), lambda bi, i: (0, 0, 0)),
            pl.BlockSpec((1, D_TOK), lambda bi, i: (0, 0)),
        ],
        out_specs=pl.BlockSpec((None, tm, D_TOK), lambda bi, i: (bi, i, 0)),
        out_shape=jax.ShapeDtypeStruct((b, t, D_TOK), BF16),
        scratch_shapes=[pltpu.VMEM((tm + 2 * POOL_HALO, D_TOK), F32),
                        pltpu.VMEM((tm + 2 * POOL_HALO, POOL_GROUP), F32),
                        pltpu.VMEM((tm + 2 * POOL_HALO, POOL_GROUP), F32)],
        compiler_params=_params("parallel", "arbitrary"),
        name="pool_mix",
    )(proj, proj, prev, w_grp, scale.reshape(1, D_TOK))


def _scan_matrix():
    shape = (2 * SB_BLOCK, SB_BLOCK + HEAD_DIM)
    r = lax.broadcasted_iota(jnp.int32, shape, 0) % SB_BLOCK
    c = lax.broadcasted_iota(jnp.int32, shape, 1)
    return jnp.where((c >= SB_BLOCK) | (r > c), 1.0, 0.0).astype(BF16)


def _sb_steps(qs, ks, vs, runs, mask, scan):
    tq = qs[0].shape[0]
    zs = [_dot_nt(q, k) for q, k in zip(qs, ks)]
    fails, parts = [], []
    for z in zs:
        log_fail = -(jnp.maximum(z, 0.0) + jnp.log(1.0 + jnp.exp(-jnp.abs(z))))
        if mask is not None:
            log_fail = jnp.where(mask, log_fail, 0.0)
        hi = log_fail.astype(BF16)
        lo = (log_fail - hi.astype(F32)).astype(BF16)
        fails.append(log_fail)
        parts.append(jnp.concatenate([hi, lo], axis=1))
    sums = _dot(jnp.concatenate(parts, axis=0), scan)
    weights, new_runs = [], []
    for g, (z, log_fail, run) in enumerate(zip(zs, fails, runs)):
        s = sums[g * tq:(g + 1) * tq]
        a = jnp.exp(z + log_fail + s[:, :SB_BLOCK] + run)
        if mask is not None:
            a = jnp.where(mask, a, 0.0)
        weights.append(a.astype(BF16))
        new_runs.append(run + s[:, SB_BLOCK:])
    return new_runs, [_dot(a, v) for a, v in zip(weights, vs)]


def _largest(values):
    out = values[0]
    for v in values[1:]:
        out = jnp.maximum(out, v)
    return jnp.max(out)


def _sb_prompt_body(q_ref, k_ref, v_ref, o_ref, run_ref, acc_ref, scan_ref, *, t, group):
    scan_ref[...] = _scan_matrix()
    below_diag = (lax.broadcasted_iota(jnp.int32, (SB_BLOCK, SB_BLOCK), 1) <
                  lax.broadcasted_iota(jnp.int32, (SB_BLOCK, SB_BLOCK), 0))
    zero = jnp.zeros((SB_BLOCK, HEAD_DIM), F32)
    chain = [slice(g * SB_BLOCK, (g + 1) * SB_BLOCK) for g in range(group)]

    def block(i):
        return pl.ds(pl.multiple_of(i * SB_BLOCK, SB_BLOCK), SB_BLOCK)

    def group_scan(s, _):
        first = s * group
        tiles = [block(first + g) for g in range(group)]
        runs, pvs = _sb_steps([q_ref[r, :] for r in tiles], [k_ref[r, :] for r in tiles],
                              [v_ref[r, :] for r in tiles], [zero] * group, below_diag, scan_ref[...])
        for sl, run, pv in zip(chain, runs, pvs):
            run_ref[sl, :] = run
            acc_ref[sl, :] = pv

        def split(runs):
            return (_largest([r[:SB_LATE_ROWS] for r in runs]), _largest([r[SB_LATE_ROWS:] for r in runs]))

        def visit(it, n):
            keys = [block(jnp.maximum(first + g - it, 0)) for g in range(group)]
            rows = [slice(g * SB_BLOCK, g * SB_BLOCK + n) for g in range(group)]
            runs = [jnp.where(first + g - it >= 0, run_ref[r, :], RUN_FINISHED) for g, r in enumerate(rows)]
            queries = [q_ref[pl.ds(pl.multiple_of((first + g) * SB_BLOCK, SB_BLOCK), n), :] for g in range(group)]
            runs, pvs = _sb_steps(queries, [k_ref[r, :] for r in keys], [v_ref[r, :] for r in keys], runs, None,
                                  scan_ref[...])
            for r, run, pv in zip(rows, runs, pvs):
                run_ref[r, :] = run
                acc_ref[r, :] += pv
            return runs

        def more(c):
            it, late, rest = c
            return jnp.logical_and(it < first + group, jnp.maximum(late, rest) > EXP_ZERO_BELOW)

        def step(c):
            it, _, rest = c
            late, rest = lax.cond(rest > EXP_ZERO_BELOW,
                                  lambda: split(visit(it, SB_BLOCK)),
                                  lambda: (_largest(visit(it, SB_LATE_ROWS)), rest))
            return it + 1, late, rest

        lax.while_loop(more, step, (jnp.int32(1),) + split(runs))
        out_rows = pl.ds(pl.multiple_of(first * SB_BLOCK, SB_BLOCK), group * SB_BLOCK)
        o_ref[out_rows, :] = acc_ref[...].astype(o_ref.dtype)
        return 0

    lax.fori_loop(0, t // (group * SB_BLOCK), group_scan, 0)


def _sb_attn_prompt(q, k, v):
    b, _, t, _ = q.shape
    group = min(SB_GROUP, t // SB_BLOCK)
    assert t % (group * SB_BLOCK) == 0, t
    spec = pl.BlockSpec((None, None, t, HEAD_DIM), lambda bi, h: (bi, h, 0, 0))
    return pl.pallas_call(
        functools.partial(_sb_prompt_body, t=t, group=group),
        grid=(b, SB_HEADS),
        in_specs=[spec, spec, spec],
        out_specs=spec,
        out_shape=jax.ShapeDtypeStruct((b, SB_HEADS, t, HEAD_DIM), BF16),
        scratch_shapes=[pltpu.VMEM((group * SB_BLOCK, HEAD_DIM), F32),
                        pltpu.VMEM((group * SB_BLOCK, HEAD_DIM), F32),
                        pltpu.VMEM((2 * SB_BLOCK, SB_BLOCK + HEAD_DIM), BF16)],
        compiler_params=_params("parallel", "parallel"),
        name="sb_attn_prompt",
    )(q, k, v)


def _sb_cached_body(q_ref, kn_ref, vn_ref, kc_ref, vc_ref, o_ref, run_ref, acc_ref, scan_ref, *, t, past):
    scan_ref[...] = _scan_matrix()
    earlier = (lax.broadcasted_iota(jnp.int32, (t, SB_BLOCK), 1) <
               lax.broadcasted_iota(jnp.int32, (t, SB_BLOCK), 0))
    zero = jnp.zeros((t, HEAD_DIM), F32)
    pad = jnp.zeros((SB_BLOCK - t, HEAD_DIM), BF16)
    heads = range(SB_HEADS)
    chain = [slice(h * t, (h + 1) * t) for h in heads]

    runs, pvs = _sb_steps([q_ref[h] for h in heads],
                          [jnp.concatenate([kn_ref[h], pad], axis=0) for h in heads],
                          [jnp.concatenate([vn_ref[h], pad], axis=0) for h in heads],
                          [zero] * SB_HEADS, earlier, scan_ref[...])
    for sl, run, pv in zip(chain, runs, pvs):
        run_ref[sl, :] = run
        acc_ref[sl, :] = pv

    def more(c):
        it, largest_run = c
        return jnp.logical_and(it < past // SB_BLOCK, largest_run > EXP_ZERO_BELOW)

    def step(c):
        it, _ = c
        keys = pl.ds(pl.multiple_of(past - (it + 1) * SB_BLOCK, SB_BLOCK), SB_BLOCK)
        runs, pvs = _sb_steps([q_ref[h] for h in heads], [kc_ref[h, keys, :].astype(BF16) for h in heads],
                              [vc_ref[h, keys, :].astype(BF16) for h in heads], [run_ref[sl, :] for sl in chain],
                              None, scan_ref[...])
        for sl, run, pv in zip(chain, runs, pvs):
            run_ref[sl, :] = run
            acc_ref[sl, :] += pv
        return it + 1, _largest(runs)

    lax.while_loop(more, step, (jnp.int32(0), _largest(runs)))
    for h, sl in zip(heads, chain):
        o_ref[h] = acc_ref[sl, :].astype(o_ref.dtype)


def _sb_attn_cached(q, k_new, v_new, k_past, v_past):
    b, _, t, _ = q.shape
    past = k_past.shape[2]
    assert past % SB_BLOCK == 0 and t <= SB_BLOCK and t % 16 == 0, (past, t)
    new = pl.BlockSpec((None, SB_HEADS, t, HEAD_DIM), lambda bi: (bi, 0, 0, 0))
    old = pl.BlockSpec((None, SB_HEADS, past, HEAD_DIM), lambda bi: (bi, 0, 0, 0))
    return pl.pallas_call(
        functools.partial(_sb_cached_body, t=t, past=past),
        grid=(b,),
        in_specs=[new, new, new, old, old],
        out_specs=new,
        out_shape=jax.ShapeDtypeStruct((b, SB_HEADS, t, HEAD_DIM), BF16),
        scratch_shapes=[pltpu.VMEM((SB_HEADS * t, HEAD_DIM), F32),
                        pltpu.VMEM((SB_HEADS * t, HEAD_DIM), F32),
                        pltpu.VMEM((2 * SB_BLOCK, SB_BLOCK + HEAD_DIM), BF16)],
        compiler_params=_params("parallel"),
        name="sb_attn_cached",
    )(q, k_new, v_new, k_past, v_past)


def _mix_out_body(x_ref, mix_ref, qm_ref, mk_ref, mv_ref, qg_ref, w_ref, o_ref):
    bb = mk_ref.shape[0]
    n_mem = mk_ref.shape[1] // MEM_HEADS
    tt = x_ref.shape[0] // bb
    probs = []
    for bi in range(bb):
        rows = slice(bi * tt, (bi + 1) * tt)
        for hd in range(MEM_HEADS):
            sl = slice(hd * HEAD_DIM, (hd + 1) * HEAD_DIM)
            qh = _rms(qm_ref[rows, sl], qg_ref[...]).astype(BF16)
            s = _dot_nt(qh, mk_ref[bi, _head_rows(hd, n_mem), :].astype(BF16)) * INV_SQRT_HD
            p = jnp.exp(s - jnp.max(s, axis=-1, keepdims=True))
            probs.append((p / jnp.sum(p, axis=-1, keepdims=True)).astype(BF16))
    if len(mix_ref.shape) == 4:
        mix = jnp.concatenate([jnp.concatenate([mix_ref[bi, hd] for hd in range(SB_HEADS)], axis=1)
                               for bi in range(mix_ref.shape[0])], axis=0)
    else:
        mix = mix_ref[...]
    y = x_ref[...] + _dot(mix, w_ref[:D_TOK, :])
    mo = jnp.concatenate(
        [jnp.concatenate([_dot(probs[bi * MEM_HEADS + hd], mv_ref[bi, _head_rows(hd, n_mem), :].astype(BF16))
                          for hd in range(MEM_HEADS)], axis=1) for bi in range(bb)], axis=0)
    o_ref[...] = y + _dot(mo.astype(BF16), w_ref[D_TOK:, :])


def _mix_out(x, mix, qm_src, qm_col, mem_k, mem_v, q_gain, w_out, layer, t):
    n, d = x.shape
    mem_rows = mem_k.shape[2]
    tm = min(n, ROW_TILE)
    row = lambda c, col=0: pl.BlockSpec((tm, c), lambda i: (i, col))
    if t >= tm:
        per_seq = t // tm
        mem = pl.BlockSpec((None, 1, mem_rows, HEAD_DIM), lambda i: (layer, i // per_seq, 0, 0))
        heads = pl.BlockSpec((1, SB_HEADS, tm, HEAD_DIM), lambda i: (i // per_seq, 0, i % per_seq, 0))
    else:
        mem = pl.BlockSpec((None, tm // t, mem_rows, HEAD_DIM), lambda i: (layer, i, 0, 0),
                           pipeline_mode=pl.Buffered(1))
        heads = pl.BlockSpec((tm // t, SB_HEADS, t, HEAD_DIM), lambda i: (i, 0, 0, 0))
    return pl.pallas_call(
        _mix_out_body,
        grid=(n // tm,),
        in_specs=[row(d), heads if mix.ndim == 4 else row(D_TOK), row(D_MEM, qm_col), mem, mem,
                  pl.BlockSpec((1, HEAD_DIM), lambda i: (0, 0)), _resident((d, d), layer)],
        out_specs=row(d),
        out_shape=jax.ShapeDtypeStruct((n, d), F32),
        compiler_params=_params("parallel"),
        name="mix_out",
    )(x, mix, qm_src, mem_k, mem_v, q_gain.reshape(1, HEAD_DIM), w_out)


def _ffn_body(x_ref, g_ref, wu_ref, wd_ref, o_ref, *rest):
    h_ref = rest[-1]

    @pl.when(pl.program_id(1) == 0)
    def _():
        x = x_ref[...]
        h_ref[...] = _rms(x, g_ref[...]).astype(BF16)
        o_ref[...] = x

    wu = wu_ref[...].astype(BF16)
    wd = wd_ref[...].astype(BF16)
    if len(rest) == 3:
        rest[0][...] = wu
        rest[1][...] = wd
    pieces = max(1, wu.shape[1] // 1024)
    half = wu.shape[1] // pieces
    for c in range(pieces):
        cols = slice(c * half, (c + 1) * half)
        up = jnp.maximum(_dot(h_ref[...], wu[:, cols]), 0.0)
        o_ref[...] += _dot((up * up).astype(BF16), wd[cols, :])


def _ffn(x, g, w_up, w_down):
    n, d = x.shape
    f = w_up.shape[1]
    tm = min(n, ROW_TILE)
    tk = 2048
    return pl.pallas_call(
        _ffn_body,
        grid=(n // tm, f // tk),
        in_specs=[
            pl.BlockSpec((tm, d), lambda i, k: (i, 0)),
            pl.BlockSpec((1, d), lambda i, k: (0, 0)),
            pl.BlockSpec((d, tk), lambda i, k: (0, k)),
            pl.BlockSpec((tk, d), lambda i, k: (k, 0)),
        ],
        out_specs=pl.BlockSpec((tm, d), lambda i, k: (i, 0)),
        out_shape=jax.ShapeDtypeStruct((n, d), F32),
        scratch_shapes=[pltpu.VMEM((tm, d), BF16)],
        compiler_params=pltpu.CompilerParams(dimension_semantics=("parallel", "arbitrary"),
                                             vmem_limit_bytes=BIG_VMEM_LIMIT),
        name="ffn",
    )(x, g.reshape(1, d), w_up, w_down)


def _ffn_casting(x, g, w_up, w_down, layer):
    n, d = x.shape
    f = w_up.shape[2]
    tk = 512
    assert n <= ROW_TILE, n
    return pl.pallas_call(
        _ffn_body,
        grid=(1, f // tk),
        in_specs=[
            pl.BlockSpec((n, d), lambda i, k: (0, 0)),
            pl.BlockSpec((1, d), lambda i, k: (0, 0)),
            pl.BlockSpec((None, d, tk), lambda i, k: (layer, 0, k)),
            pl.BlockSpec((None, tk, d), lambda i, k: (layer, k, 0)),
        ],
        out_specs=[pl.BlockSpec((n, d), lambda i, k: (0, 0)),
                   pl.BlockSpec((d, tk), lambda i, k: (0, k)),
                   pl.BlockSpec((tk, d), lambda i, k: (k, 0))],
        out_shape=[jax.ShapeDtypeStruct((n, d), F32), jax.ShapeDtypeStruct((d, f), BF16),
                   jax.ShapeDtypeStruct((f, d), BF16)],
        scratch_shapes=[pltpu.VMEM((n, d), BF16)],
        compiler_params=_params("arbitrary", "arbitrary"),
        name="ffn_casting",
    )(x, g.reshape(1, d), w_up, w_down)


def _run_trunk(x, past, pool_prev, sb_past_k, sb_past_v, mem_k, mem_v, p, ffn_bf16):
    b, t, d = x.shape
    n = b * t
    depth = p["w_out"].shape[0]
    pool_new, sbk_new, sbv_new, made = [], [], [], []
    x2 = x.reshape(n, d)
    for i in range(depth):
        j = i // 2
        if i % 2 == 0:
            proj = _norm_matmul(x2, p["norm_mix"][i], p["w_in_pool"], j)
            proj3 = proj.reshape(b, t, d)
            mix = _pool_mix(proj3, pool_prev[j], p["pool_w"][j], p["pool_scale"][j], past).reshape(n, D_TOK)
            pool_new.append(proj3[:, t - POOL_STATE:, :D_TOK])
            qm_src, qm_col = proj, D_TOK // D_MEM
        else:
            q, k, v, kb, vb, qm = _inproj_sb(x2, p["norm_mix"][i], p["w_in_sb"], j, b, t)
            if sb_past_k is None:
                mix = _sb_attn_prompt(q, kb, vb)
            else:
                mix = _sb_attn_cached(q, kb, vb, sb_past_k[j], sb_past_v[j])
            sbk_new.append(k)
            sbv_new.append(v)
            qm_src, qm_col = qm, 0
        x2 = _mix_out(x2, mix, qm_src, qm_col, mem_k, mem_v, p["q_norm"][i], p["w_out"], i, t)
        if ffn_bf16 is None:
            x2, wu, wd = _ffn_casting(x2, p["norm_ffn"][i], p["w_up"], p["w_down"], i)
            made.append((wu, wd))
        else:
            x2 = _ffn(x2, p["norm_ffn"][i], *ffn_bf16[i])
    return x2.reshape(b, t, d), jnp.stack(pool_new), jnp.stack(sbk_new), jnp.stack(sbv_new), made


def kernel(x_prompt, x_sample, mem_prompt, cache_pool, cache_sb_k, cache_sb_v, cache_mem_k, cache_mem_v,
           norm_mix, w_in_pool, w_in_sb, pool_w, pool_scale, norm_mem, w_mem_kv, q_norm, k_norm, w_out,
           norm_ffn, w_up, w_down):
    b, t, d = x_prompt.shape
    depth = w_out.shape[0]
    n_mem = mem_prompt.shape[1]
    p = dict(norm_mix=norm_mix, pool_scale=pool_scale, q_norm=q_norm, norm_ffn=norm_ffn,
             w_in_pool=w_in_pool.astype(BF16), w_in_sb=w_in_sb.astype(BF16), pool_w=pool_w.astype(BF16),
             w_out=w_out.astype(BF16), w_up=w_up, w_down=w_down)

    bs = x_sample.shape[0]
    past = cache_sb_k.shape[2]
    pool_prev = jnp.pad(cache_pool, ((0, 0), (0, 0), (POOL_HALO - POOL_STATE, 0), (0, 0)))
    head_major = lambda a: jnp.transpose(a, (0, 1, 3, 2, 4))
    mem_rows = lambda a, nb: a.reshape(depth, nb, n_mem * MEM_HEADS, HEAD_DIM)
    y_sample, pool_sample, sb_k_sample, sb_v_sample, ffn_bf16 = _run_trunk(
        x_sample, past, pool_prev, head_major(cache_sb_k), head_major(cache_sb_v),
        mem_rows(cache_mem_k, bs), mem_rows(cache_mem_v, bs), p, None)

    mk, mv = _mem_kv(mem_prompt.reshape(b * n_mem, d), norm_mem, w_mem_kv, k_norm)
    mk = mem_rows(mk, b)
    mv = mem_rows(mv, b)
    pool_zero = jnp.zeros((cache_pool.shape[0], b, POOL_HALO, D_TOK), F32)
    y_prompt, pool_prompt, sb_k_prompt, sb_v_prompt, _ = _run_trunk(
        x_prompt, 0, pool_zero, None, None, mk, mv, p, ffn_bf16)

    heads = lambda a: a.reshape(depth, b, n_mem, MEM_HEADS, HEAD_DIM)
    return (y_prompt, y_sample, pool_prompt, pool_sample, head_major(sb_k_prompt), head_major(sb_v_prompt),
            head_major(sb_k_sample), head_major(sb_v_sample), heads(mk), heads(mv))
```

```python
import functools
import math

import jax
import jax.numpy as jnp
from jax import lax
from jax.experimental import pallas as pl
from jax.experimental.pallas import tpu as pltpu

D_MODEL = 2048
HEAD_DIM = 128
D_TOK = 3 * D_MODEL // 4
D_MEM = D_MODEL // 4
SB_HEADS = D_TOK // HEAD_DIM
MEM_HEADS = D_MEM // HEAD_DIM
POOL_WINDOWS = (2, 4, 8, 16)
POOL_GROUP = D_TOK // len(POOL_WINDOWS)
POOL_STATE = max(POOL_WINDOWS) - 1
POOL_HALO = POOL_STATE + 1
D_FF = 4 * D_MODEL
EPS = 1e-6
INV_SQRT_HD = 1.0 / math.sqrt(HEAD_DIM)

SB_BLOCK = 128
SB_LATE_ROWS = 64
SB_GROUP = 32
EXP_ZERO_BELOW = -104.0
RUN_FINISHED = -1e30
ROW_TILE = 512
VMEM_LIMIT = 48 * 1024 * 1024
BIG_VMEM_LIMIT = 60 * 1024 * 1024

BF16 = jnp.bfloat16
F32 = jnp.float32


def _params(*sem):
    return pltpu.CompilerParams(dimension_semantics=sem, vmem_limit_bytes=VMEM_LIMIT)


def _rms(xf, g):
    return xf * lax.rsqrt(jnp.mean(xf * xf, axis=-1, keepdims=True) + EPS) * g


def _dot(a, b):
    return jnp.dot(a, b, preferred_element_type=F32)


def _dot_nt(a, b):
    return lax.dot_general(a, b, (((1,), (1,)), ((), ())), preferred_element_type=F32)


def _resident(shape, layer):
    return pl.BlockSpec((None,) + shape, lambda *_: (layer,) + (0,) * len(shape), pipeline_mode=pl.Buffered(1))


def _norm_matmul_body(x_ref, g_ref, w_ref, o_ref):
    o_ref[...] = _dot(_rms(x_ref[...], g_ref[...]).astype(BF16), w_ref[...])


def _norm_matmul(x, g, w, layer):
    n, d = x.shape
    m = w.shape[2]
    tm = min(n, ROW_TILE)
    return pl.pallas_call(
        _norm_matmul_body,
        grid=(n // tm,),
        in_specs=[
            pl.BlockSpec((tm, d), lambda i: (i, 0)),
            pl.BlockSpec((1, d), lambda i: (0, 0)),
            _resident((d, m), layer),
        ],
        out_specs=pl.BlockSpec((tm, m), lambda i: (i, 0)),
        out_shape=jax.ShapeDtypeStruct((n, m), F32),
        compiler_params=_params("parallel"),
        name="norm_matmul",
    )(x, g.reshape(1, d), w)


def _inproj_sb_body(x_ref, g_ref, wq_ref, wk_ref, wv_ref, wm_ref,
                    q_ref, k_ref, v_ref, kb_ref, vb_ref, qm_ref, h_ref):
    @pl.when(pl.program_id(1) == 0)
    def _():
        h_ref[...] = _rms(x_ref[...], g_ref[...]).astype(BF16)
        qm_ref[...] = _dot(h_ref[...], wm_ref[...])

    def put(refs, val):
        bb, heads, tt, _ = refs[0].shape
        for bi in range(bb):
            for hd in range(heads):
                piece = val[bi * tt:(bi + 1) * tt, hd * HEAD_DIM:(hd + 1) * HEAD_DIM]
                for ref in refs:
                    ref[bi, hd] = piece.astype(ref.dtype)

    h = h_ref[...]
    put([q_ref], _dot(h, wq_ref[...]) * INV_SQRT_HD)
    put([k_ref, kb_ref], _dot(h, wk_ref[...]))
    put([v_ref, vb_ref], _dot(h, wv_ref[...]))


def _inproj_sb(x, g, w, layer, b, t):
    n, d = x.shape
    tm = min(n, 2 * ROW_TILE)
    tn = D_MEM
    nj = D_TOK // tn
    hj = tn // HEAD_DIM
    if t >= tm:
        per_seq = t // tm
        oblock, omap = (1, hj, tm, HEAD_DIM), lambda i, j: (i // per_seq, j, i % per_seq, 0)
    else:
        oblock, omap = (tm // t, hj, t, HEAD_DIM), lambda i, j: (i, j, 0, 0)
    tok = lambda dt: jax.ShapeDtypeStruct((b, SB_HEADS, t, HEAD_DIM), dt)
    wspec = lambda off: pl.BlockSpec((None, d, tn), lambda i, j: (layer, 0, off + j))
    ospec = pl.BlockSpec(oblock, omap)
    return pl.pallas_call(
        _inproj_sb_body,
        grid=(n // tm, nj),
        in_specs=[
            pl.BlockSpec((tm, d), lambda i, j: (i, 0)),
            pl.BlockSpec((1, d), lambda i, j: (0, 0)),
            wspec(0), wspec(nj), wspec(2 * nj),
            pl.BlockSpec((None, d, D_MEM), lambda i, j: (layer, 0, 3 * D_TOK // D_MEM),
                         pipeline_mode=pl.Buffered(1)),
        ],
        out_specs=[ospec, ospec, ospec, ospec, ospec,
                   pl.BlockSpec((tm, D_MEM), lambda i, j: (i, 0))],
        out_shape=[tok(BF16), tok(F32), tok(F32), tok(BF16), tok(BF16),
                   jax.ShapeDtypeStruct((n, D_MEM), F32)],
        scratch_shapes=[pltpu.VMEM((tm, d), BF16)],
        compiler_params=pltpu.CompilerParams(dimension_semantics=("parallel", "arbitrary"),
                                             vmem_limit_bytes=BIG_VMEM_LIMIT),
        name="inproj_sb",
    )(x, g.reshape(1, d), w, w, w, w)


def _head_rows(hd, n):
    return pl.ds(hd, n, stride=MEM_HEADS)


def _mem_kv_body(x_ref, g_ref, w_ref, kg_ref, k_ref, v_ref):
    tm = x_ref.shape[0]
    h = _rms(x_ref[...], g_ref[...]).astype(BF16)
    kv = _dot(h, w_ref[...].astype(BF16))
    for hd in range(MEM_HEADS):
        sl = slice(hd * HEAD_DIM, (hd + 1) * HEAD_DIM)
        k_ref[_head_rows(hd, tm), :] = _rms(kv[:, sl], kg_ref[...])
        v_ref[_head_rows(hd, tm), :] = kv[:, D_MEM + hd * HEAD_DIM:D_MEM + (hd + 1) * HEAD_DIM]


def _mem_kv(mem, g_mem, w_kv, k_gain):
    r, d = mem.shape
    depth = w_kv.shape[0]
    tm = min(r, 256)
    out = jax.ShapeDtypeStruct((depth, r * MEM_HEADS, HEAD_DIM), F32)
    ospec = pl.BlockSpec((None, tm * MEM_HEADS, HEAD_DIM), lambda l, i: (l, i, 0))
    return pl.pallas_call(
        _mem_kv_body,
        grid=(depth, r // tm),
        in_specs=[
            pl.BlockSpec((tm, d), lambda l, i: (i, 0)),
            pl.BlockSpec((None, 1, d), lambda l, i: (l, 0, 0)),
            pl.BlockSpec((None, d, 2 * D_MEM), lambda l, i: (l, 0, 0)),
            pl.BlockSpec((None, 1, HEAD_DIM), lambda l, i: (l, 0, 0)),
        ],
        out_specs=[ospec, ospec],
        out_shape=[out, out],
        compiler_params=_params("parallel", "parallel"),
        name="mem_kv",
    )(mem, g_mem.reshape(depth, 1, d), w_kv, k_gain.reshape(depth, 1, HEAD_DIM))


def _pool_body(u_ref, halo_ref, prev_ref, w_ref, s_ref, o_ref, ext_ref, a_ref, b_ref, *, tm, past):
    i = pl.program_id(1)
    base = 2 * POOL_HALO
    rows = tm + POOL_HALO
    ext_ref[0:POOL_HALO, :] = jnp.zeros((POOL_HALO, D_TOK), F32)
    a_ref[0:POOL_HALO, :] = jnp.zeros((POOL_HALO, POOL_GROUP), F32)
    b_ref[0:POOL_HALO, :] = jnp.zeros((POOL_HALO, POOL_GROUP), F32)

    @pl.when(i == 0)
    def _():
        ext_ref[POOL_HALO:base, :] = prev_ref[...]

    @pl.when(i > 0)
    def _():
        ext_ref[POOL_HALO:base, :] = halo_ref[...]

    ext_ref[base:, :] = u_ref[...]
    pos = (past + i * tm + lax.broadcasted_iota(jnp.int32, (tm, 1), 0)).astype(F32)
    for g, w in enumerate(POOL_WINDOWS):
        sl = slice(g * POOL_GROUP, (g + 1) * POOL_GROUP)
        src, cols, nxt = ext_ref, sl, (a_ref, b_ref)
        span = 1
        while span < w:
            part = src[POOL_HALO:POOL_HALO + rows, cols] + src[POOL_HALO - span:POOL_HALO - span + rows, cols]
            span *= 2
            if span < w:
                nxt[0][POOL_HALO:POOL_HALO + rows, :] = part
                src, cols, nxt = nxt[0], slice(None), nxt[::-1]
        win = part[POOL_HALO:, :]
        cur = ext_ref[base:, sl]
        cnt = jnp.minimum(pos + 1.0, float(w))
        dlt = (win / cnt - cur).astype(BF16)
        o_ref[:, sl] = (_dot(dlt, w_ref[g]) * s_ref[:, sl]).astype(BF16)


def _pool_mix(proj, prev, w_grp, scale, past):
    b, t, _ = proj.shape
    tm = min(t, 512)
    hb = tm // POOL_HALO
    return pl.pallas_call(
        functools.partial(_pool_body, tm=tm, past=past),
        grid=(b, t // tm),
        in_specs=[
            pl.BlockSpec((None, tm, D_TOK), lambda bi, i: (bi, i, 0)),
            pl.BlockSpec((None, POOL_HALO, D_TOK), lambda bi, i: (bi, jnp.maximum(i * hb - 1, 0), 0)),
            pl.BlockSpec((None, POOL_HALO, D_TOK), lambda bi, i: (bi, 0, 0)),
            pl.BlockSpec((len(POOL_WINDOWS), POOL_GROUP, POOL_GROUP), lambda bi, i: (0, 0, 0)),
            pl.BlockSpec((1, D_TOK), lambda bi, i: (0, 0)),
        ],
        out_specs=pl.BlockSpec((None, tm, D_TOK), lambda bi, i: (bi, i, 0)),
        out_shape=jax.ShapeDtypeStruct((b, t, D_TOK), BF16),
        scratch_shapes=[pltpu.VMEM((tm + 2 * POOL_HALO, D_TOK), F32),
                        pltpu.VMEM((tm + 2 * POOL_HALO, POOL_GROUP), F32),
                        pltpu.VMEM((tm + 2 * POOL_HALO, POOL_GROUP), F32)],
        compiler_params=_params("parallel", "arbitrary"),
        name="pool_mix",
    )(proj, proj, prev, w_grp, scale.reshape(1, D_TOK))


def _scan_matrix():
    shape = (2 * SB_BLOCK, SB_BLOCK + HEAD_DIM)
    r = lax.broadcasted_iota(jnp.int32, shape, 0) % SB_BLOCK
    c = lax.broadcasted_iota(jnp.int32, shape, 1)
    return jnp.where((c >= SB_BLOCK) | (r > c), 1.0, 0.0).astype(BF16)


def _sb_steps(qs, ks, vs, runs, mask, scan):
    tq = qs[0].shape[0]
    zs = [_dot_nt(q, k) for q, k in zip(qs, ks)]
    fails, parts = [], []
    for z in zs:
        log_fail = -(jnp.maximum(z, 0.0) + jnp.log(1.0 + jnp.exp(-jnp.abs(z))))
        if mask is not None:
            log_fail = jnp.where(mask, log_fail, 0.0)
        hi = log_fail.astype(BF16)
        lo = (log_fail - hi.astype(F32)).astype(BF16)
        fails.append(log_fail)
        parts.append(jnp.concatenate([hi, lo], axis=1))
    sums = _dot(jnp.concatenate(parts, axis=0), scan)
    weights, new_runs = [], []
    for g, (z, log_fail, run) in enumerate(zip(zs, fails, runs)):
        s = sums[g * tq:(g + 1) * tq]
        a = jnp.exp(z + log_fail + s[:, :SB_BLOCK] + run)
        if mask is not None:
            a = jnp.where(mask, a, 0.0)
        weights.append(a.astype(BF16))
        new_runs.append(run + s[:, SB_BLOCK:])
    return new_runs, [_dot(a, v) for a, v in zip(weights, vs)]


def _largest(values):
    out = values[0]
    for v in values[1:]:
        out = jnp.maximum(out, v)
    return jnp.max(out)


def _sb_prompt_body(q_ref, k_ref, v_ref, o_ref, run_ref, acc_ref, scan_ref, *, t, group):
    scan_ref[...] = _scan_matrix()
    below_diag = (lax.broadcasted_iota(jnp.int32, (SB_BLOCK, SB_BLOCK), 1) <
                  lax.broadcasted_iota(jnp.int32, (SB_BLOCK, SB_BLOCK), 0))
    zero = jnp.zeros((SB_BLOCK, HEAD_DIM), F32)
    chain = [slice(g * SB_BLOCK, (g + 1) * SB_BLOCK) for g in range(group)]

    def block(i):
        return pl.ds(pl.multiple_of(i * SB_BLOCK, SB_BLOCK), SB_BLOCK)

    def group_scan(s, _):
        first = s * group
        tiles = [block(first + g) for g in range(group)]
        runs, pvs = _sb_steps([q_ref[r, :] for r in tiles], [k_ref[r, :] for r in tiles],
                              [v_ref[r, :] for r in tiles], [zero] * group, below_diag, scan_ref[...])
        for sl, run, pv in zip(chain, runs, pvs):
            run_ref[sl, :] = run
            acc_ref[sl, :] = pv

        def split(runs):
            return (_largest([r[:SB_LATE_ROWS] for r in runs]), _largest([r[SB_LATE_ROWS:] for r in runs]))

        def visit(it, n):
            keys = [block(jnp.maximum(first + g - it, 0)) for g in range(group)]
            rows = [slice(g * SB_BLOCK, g * SB_BLOCK + n) for g in range(group)]
            runs = [jnp.where(first + g - it >= 0, run_ref[r, :], RUN_FINISHED) for g, r in enumerate(rows)]
            queries = [q_ref[pl.ds(pl.multiple_of((first + g) * SB_BLOCK, SB_BLOCK), n), :] for g in range(group)]
            runs, pvs = _sb_steps(queries, [k_ref[r, :] for r in keys], [v_ref[r, :] for r in keys], runs, None,
                                  scan_ref[...])
            for r, run, pv in zip(rows, runs, pvs):
                run_ref[r, :] = run
                acc_ref[r, :] += pv
            return runs

        def more(c):
            it, late, rest = c
            return jnp.logical_and(it < first + group, jnp.maximum(late, rest) > EXP_ZERO_BELOW)

        def step(c):
            it, _, rest = c
            late, rest = lax.cond(rest > EXP_ZERO_BELOW,
                                  lambda: split(visit(it, SB_BLOCK)),
                                  lambda: (_largest(visit(it, SB_LATE_ROWS)), rest))
            return it + 1, late, rest

        lax.while_loop(more, step, (jnp.int32(1),) + split(runs))
        out_rows = pl.ds(pl.multiple_of(first * SB_BLOCK, SB_BLOCK), group * SB_BLOCK)
        o_ref[out_rows, :] = acc_ref[...].astype(o_ref.dtype)
        return 0

    lax.fori_loop(0, t // (group * SB_BLOCK), group_scan, 0)


def _sb_attn_prompt(q, k, v):
    b, _, t, _ = q.shape
    group = min(SB_GROUP, t // SB_BLOCK)
    assert t % (group * SB_BLOCK) == 0, t
    spec = pl.BlockSpec((None, None, t, HEAD_DIM), lambda bi, h: (bi, h, 0, 0))
    return pl.pallas_call(
        functools.partial(_sb_prompt_body, t=t, group=group),
        grid=(b, SB_HEADS),
        in_specs=[spec, spec, spec],
        out_specs=spec,
        out_shape=jax.ShapeDtypeStruct((b, SB_HEADS, t, HEAD_DIM), BF16),
        scratch_shapes=[pltpu.VMEM((group * SB_BLOCK, HEAD_DIM), F32),
                        pltpu.VMEM((group * SB_BLOCK, HEAD_DIM), F32),
                        pltpu.VMEM((2 * SB_BLOCK, SB_BLOCK + HEAD_DIM), BF16)],
        compiler_params=_params("parallel", "parallel"),
        name="sb_attn_prompt",
    )(q, k, v)


def _sb_cached_body(q_ref, kn_ref, vn_ref, kr_ref, vr_ref, k_hbm, v_hbm, o_ref,
                    run_ref, acc_ref, scan_ref, kbuf_ref, vbuf_ref, sem, *, t, past, recent):
    scan_ref[...] = _scan_matrix()
    earlier = (lax.broadcasted_iota(jnp.int32, (t, SB_BLOCK), 1) <
               lax.broadcasted_iota(jnp.int32, (t, SB_BLOCK), 0))
    zero = jnp.zeros((t, HEAD_DIM), F32)
    pad = jnp.zeros((SB_BLOCK - t, HEAD_DIM), BF16)
    heads = range(SB_HEADS)
    chain = [slice(h * t, (h + 1) * t) for h in heads]

    runs, pvs = _sb_steps([q_ref[h] for h in heads],
                          [jnp.concatenate([kn_ref[h], pad], axis=0) for h in heads],
                          [jnp.concatenate([vn_ref[h], pad], axis=0) for h in heads],
                          [zero] * SB_HEADS, earlier, scan_ref[...])
    for sl, run, pv in zip(chain, runs, pvs):
        run_ref[sl, :] = run
        acc_ref[sl, :] = pv

    def visit(ks, vs):
        runs, pvs = _sb_steps([q_ref[h] for h in heads], ks, vs, [run_ref[sl, :] for sl in chain], None,
                              scan_ref[...])
        for sl, run, pv in zip(chain, runs, pvs):
            run_ref[sl, :] = run
            acc_ref[sl, :] += pv
        return _largest(runs)

    def more(blocks):
        return lambda c: jnp.logical_and(c[0] < blocks, c[1] > EXP_ZERO_BELOW)

    def recent_step(c):
        it, _ = c
        keys = pl.ds(pl.multiple_of(recent - (it + 1) * SB_BLOCK, SB_BLOCK), SB_BLOCK)
        return it + 1, visit([kr_ref[h, keys, :].astype(BF16) for h in heads],
                             [vr_ref[h, keys, :].astype(BF16) for h in heads])

    def older_step(c):
        it, _ = c
        bi = pl.program_id(0)
        keys = pl.ds(pl.multiple_of(past - (it + 1) * SB_BLOCK, SB_BLOCK), SB_BLOCK)
        copies = [pltpu.make_async_copy(src.at[bi, :, keys, :], dst, sem.at[n])
                  for n, (src, dst) in enumerate(((k_hbm, kbuf_ref), (v_hbm, vbuf_ref)))]
        for cp in copies:
            cp.start()
        for cp in copies:
            cp.wait()
        return it + 1, visit([kbuf_ref[h].astype(BF16) for h in heads], [vbuf_ref[h].astype(BF16) for h in heads])

    state = lax.while_loop(more(recent // SB_BLOCK), recent_step, (jnp.int32(0), _largest(runs)))
    lax.while_loop(more(past // SB_BLOCK), older_step, state)
    for h, sl in zip(heads, chain):
        o_ref[h] = acc_ref[sl, :].astype(o_ref.dtype)


def _sb_attn_cached(q, k_new, v_new, k_past, v_past):
    b, _, t, _ = q.shape
    past = k_past.shape[2]
    assert past % SB_BLOCK == 0 and t <= SB_BLOCK and t % 16 == 0, (past, t)
    recent = min(past, 2 * SB_BLOCK)
    assert past % recent == 0, past
    new = pl.BlockSpec((None, SB_HEADS, t, HEAD_DIM), lambda bi: (bi, 0, 0, 0))
    last = pl.BlockSpec((None, SB_HEADS, recent, HEAD_DIM), lambda bi: (bi, 0, past // recent - 1, 0))
    whole = pl.BlockSpec(memory_space=pl.ANY)
    return pl.pallas_call(
        functools.partial(_sb_cached_body, t=t, past=past, recent=recent),
        grid=(b,),
        in_specs=[new, new, new, last, last, whole, whole],
        out_specs=new,
        out_shape=jax.ShapeDtypeStruct((b, SB_HEADS, t, HEAD_DIM), BF16),
        scratch_shapes=[pltpu.VMEM((SB_HEADS * t, HEAD_DIM), F32),
                        pltpu.VMEM((SB_HEADS * t, HEAD_DIM), F32),
                        pltpu.VMEM((2 * SB_BLOCK, SB_BLOCK + HEAD_DIM), BF16),
                        pltpu.VMEM((SB_HEADS, SB_BLOCK, HEAD_DIM), F32),
                        pltpu.VMEM((SB_HEADS, SB_BLOCK, HEAD_DIM), F32),
                        pltpu.SemaphoreType.DMA((2,))],
        compiler_params=_params("parallel"),
        name="sb_attn_cached",
    )(q, k_new, v_new, k_past, v_past, k_past, v_past)


def _mix_out_body(x_ref, mix_ref, qm_ref, mk_ref, mv_ref, qg_ref, w_ref, o_ref):
    bb = mk_ref.shape[0]
    n_mem = mk_ref.shape[1] // MEM_HEADS
    tt = x_ref.shape[0] // bb
    probs = []
    for bi in range(bb):
        rows = slice(bi * tt, (bi + 1) * tt)
        for hd in range(MEM_HEADS):
            sl = slice(hd * HEAD_DIM, (hd + 1) * HEAD_DIM)
            qh = _rms(qm_ref[rows, sl], qg_ref[...]).astype(BF16)
            s = _dot_nt(qh, mk_ref[bi, _head_rows(hd, n_mem), :].astype(BF16)) * INV_SQRT_HD
            p = jnp.exp(s - jnp.max(s, axis=-1, keepdims=True))
            probs.append((p / jnp.sum(p, axis=-1, keepdims=True)).astype(BF16))
    if len(mix_ref.shape) == 4:
        mix = jnp.concatenate([jnp.concatenate([mix_ref[bi, hd] for hd in range(SB_HEADS)], axis=1)
                               for bi in range(mix_ref.shape[0])], axis=0)
    else:
        mix = mix_ref[...]
    y = x_ref[...] + _dot(mix, w_ref[:D_TOK, :])
    mo = jnp.concatenate(
        [jnp.concatenate([_dot(probs[bi * MEM_HEADS + hd], mv_ref[bi, _head_rows(hd, n_mem), :].astype(BF16))
                          for hd in range(MEM_HEADS)], axis=1) for bi in range(bb)], axis=0)
    o_ref[...] = y + _dot(mo.astype(BF16), w_ref[D_TOK:, :])


def _mix_out(x, mix, qm_src, qm_col, mem_k, mem_v, q_gain, w_out, layer, t):
    n, d = x.shape
    mem_rows = mem_k.shape[2]
    tm = min(n, ROW_TILE)
    row = lambda c, col=0: pl.BlockSpec((tm, c), lambda i: (i, col))
    if t >= tm:
        per_seq = t // tm
        mem = pl.BlockSpec((None, 1, mem_rows, HEAD_DIM), lambda i: (layer, i // per_seq, 0, 0))
        heads = pl.BlockSpec((1, SB_HEADS, tm, HEAD_DIM), lambda i: (i // per_seq, 0, i % per_seq, 0))
    else:
        mem = pl.BlockSpec((None, tm // t, mem_rows, HEAD_DIM), lambda i: (layer, i, 0, 0),
                           pipeline_mode=pl.Buffered(1))
        heads = pl.BlockSpec((tm // t, SB_HEADS, t, HEAD_DIM), lambda i: (i, 0, 0, 0))
    return pl.pallas_call(
        _mix_out_body,
        grid=(n // tm,),
        in_specs=[row(d), heads if mix.ndim == 4 else row(D_TOK), row(D_MEM, qm_col), mem, mem,
                  pl.BlockSpec((1, HEAD_DIM), lambda i: (0, 0)), _resident((d, d), layer)],
        out_specs=row(d),
        out_shape=jax.ShapeDtypeStruct((n, d), F32),
        compiler_params=_params("parallel"),
        name="mix_out",
    )(x, mix, qm_src, mem_k, mem_v, q_gain.reshape(1, HEAD_DIM), w_out)


def _ffn_body(x_ref, g_ref, wu_ref, wd_ref, o_ref, *rest):
    h_ref = rest[-1]

    @pl.when(pl.program_id(1) == 0)
    def _():
        x = x_ref[...]
        h_ref[...] = _rms(x, g_ref[...]).astype(BF16)
        o_ref[...] = x

    wu = wu_ref[...].astype(BF16)
    wd = wd_ref[...].astype(BF16)
    if len(rest) == 3:
        rest[0][...] = wu
        rest[1][...] = wd
    pieces = max(1, wu.shape[1] // 1024)
    half = wu.shape[1] // pieces
    for c in range(pieces):
        cols = slice(c * half, (c + 1) * half)
        up = jnp.maximum(_dot(h_ref[...], wu[:, cols]), 0.0)
        o_ref[...] += _dot((up * up).astype(BF16), wd[cols, :])


def _ffn(x, g, w_up, w_down):
    n, d = x.shape
    f = w_up.shape[1]
    tm = min(n, ROW_TILE)
    tk = 2048
    return pl.pallas_call(
        _ffn_body,
        grid=(n // tm, f // tk),
        in_specs=[
            pl.BlockSpec((tm, d), lambda i, k: (i, 0)),
            pl.BlockSpec((1, d), lambda i, k: (0, 0)),
            pl.BlockSpec((d, tk), lambda i, k: (0, k)),
            pl.BlockSpec((tk, d), lambda i, k: (k, 0)),
        ],
        out_specs=pl.BlockSpec((tm, d), lambda i, k: (i, 0)),
        out_shape=jax.ShapeDtypeStruct((n, d), F32),
        scratch_shapes=[pltpu.VMEM((tm, d), BF16)],
        compiler_params=pltpu.CompilerParams(dimension_semantics=("parallel", "arbitrary"),
                                             vmem_limit_bytes=BIG_VMEM_LIMIT),
        name="ffn",
    )(x, g.reshape(1, d), w_up, w_down)


def _ffn_casting(x, g, w_up, w_down, layer):
    n, d = x.shape
    f = w_up.shape[2]
    tk = 512
    assert n <= ROW_TILE, n
    return pl.pallas_call(
        _ffn_body,
        grid=(1, f // tk),
        in_specs=[
            pl.BlockSpec((n, d), lambda i, k: (0, 0)),
            pl.BlockSpec((1, d), lambda i, k: (0, 0)),
            pl.BlockSpec((None, d, tk), lambda i, k: (layer, 0, k)),
            pl.BlockSpec((None, tk, d), lambda i, k: (layer, k, 0)),
        ],
        out_specs=[pl.BlockSpec((n, d), lambda i, k: (0, 0)),
                   pl.BlockSpec((d, tk), lambda i, k: (0, k)),
                   pl.BlockSpec((tk, d), lambda i, k: (k, 0))],
        out_shape=[jax.ShapeDtypeStruct((n, d), F32), jax.ShapeDtypeStruct((d, f), BF16),
                   jax.ShapeDtypeStruct((f, d), BF16)],
        scratch_shapes=[pltpu.VMEM((n, d), BF16)],
        compiler_params=_params("arbitrary", "arbitrary"),
        name="ffn_casting",
    )(x, g.reshape(1, d), w_up, w_down)


def _run_trunk(x, past, pool_prev, sb_past_k, sb_past_v, mem_k, mem_v, p, ffn_bf16):
    b, t, d = x.shape
    n = b * t
    depth = p["w_out"].shape[0]
    pool_new, sbk_new, sbv_new, made = [], [], [], []
    x2 = x.reshape(n, d)
    for i in range(depth):
        j = i // 2
        if i % 2 == 0:
            proj = _norm_matmul(x2, p["norm_mix"][i], p["w_in_pool"], j)
            proj3 = proj.reshape(b, t, d)
            mix = _pool_mix(proj3, pool_prev[j], p["pool_w"][j], p["pool_scale"][j], past).reshape(n, D_TOK)
            pool_new.append(proj3[:, t - POOL_STATE:, :D_TOK])
            qm_src, qm_col = proj, D_TOK // D_MEM
        else:
            q, k, v, kb, vb, qm = _inproj_sb(x2, p["norm_mix"][i], p["w_in_sb"], j, b, t)
            if sb_past_k is None:
                mix = _sb_attn_prompt(q, kb, vb)
            else:
                mix = _sb_attn_cached(q, kb, vb, sb_past_k[j], sb_past_v[j])
            sbk_new.append(k)
            sbv_new.append(v)
            qm_src, qm_col = qm, 0
        x2 = _mix_out(x2, mix, qm_src, qm_col, mem_k, mem_v, p["q_norm"][i], p["w_out"], i, t)
        if ffn_bf16 is None:
            x2, wu, wd = _ffn_casting(x2, p["norm_ffn"][i], p["w_up"], p["w_down"], i)
            made.append((wu, wd))
        else:
            x2 = _ffn(x2, p["norm_ffn"][i], *ffn_bf16[i])
    return x2.reshape(b, t, d), jnp.stack(pool_new), jnp.stack(sbk_new), jnp.stack(sbv_new), made


def kernel(x_prompt, x_sample, mem_prompt, cache_pool, cache_sb_k, cache_sb_v, cache_mem_k, cache_mem_v,
           norm_mix, w_in_pool, w_in_sb, pool_w, pool_scale, norm_mem, w_mem_kv, q_norm, k_norm, w_out,
           norm_ffn, w_up, w_down):
    b, t, d = x_prompt.shape
    depth = w_out.shape[0]
    n_mem = mem_prompt.shape[1]
    p = dict(norm_mix=norm_mix, pool_scale=pool_scale, q_norm=q_norm, norm_ffn=norm_ffn,
             w_in_pool=w_in_pool.astype(BF16), w_in_sb=w_in_sb.astype(BF16), pool_w=pool_w.astype(BF16),
             w_out=w_out.astype(BF16), w_up=w_up, w_down=w_down)

    bs = x_sample.shape[0]
    past = cache_sb_k.shape[2]
    pool_prev = jnp.pad(cache_pool, ((0, 0), (0, 0), (POOL_HALO - POOL_STATE, 0), (0, 0)))
    head_major = lambda a: jnp.transpose(a, (0, 1, 3, 2, 4))
    mem_rows = lambda a, nb: a.reshape(depth, nb, n_mem * MEM_HEADS, HEAD_DIM)
    y_sample, pool_sample, sb_k_sample, sb_v_sample, ffn_bf16 = _run_trunk(
        x_sample, past, pool_prev, head_major(cache_sb_k), head_major(cache_sb_v),
        mem_rows(cache_mem_k, bs), mem_rows(cache_mem_v, bs), p, None)

    mk, mv = _mem_kv(mem_prompt.reshape(b * n_mem, d), norm_mem, w_mem_kv, k_norm)
    mk = mem_rows(mk, b)
    mv = mem_rows(mv, b)
    pool_zero = jnp.zeros((cache_pool.shape[0], b, POOL_HALO, D_TOK), F32)
    y_prompt, pool_prompt, sb_k_prompt, sb_v_prompt, _ = _run_trunk(
        x_prompt, 0, pool_zero, None, None, mk, mv, p, ffn_bf16)

    heads = lambda a: a.reshape(depth, b, n_mem, MEM_HEADS, HEAD_DIM)
    return (y_prompt, y_sample, pool_prompt, pool_sample, head_major(sb_k_prompt), head_major(sb_v_prompt),
            head_major(sb_k_sample), head_major(sb_v_sample), heads(mk), heads(mv))
```

```python
import functools
import math

import jax
import jax.numpy as jnp
from jax import lax
from jax.experimental import pallas as pl
from jax.experimental.pallas import tpu as pltpu

D_MODEL = 2048
HEAD_DIM = 128
D_TOK = 3 * D_MODEL // 4
D_MEM = D_MODEL // 4
SB_HEADS = D_TOK // HEAD_DIM
MEM_HEADS = D_MEM // HEAD_DIM
POOL_WINDOWS = (2, 4, 8, 16)
POOL_GROUP = D_TOK // len(POOL_WINDOWS)
POOL_STATE = max(POOL_WINDOWS) - 1
POOL_HALO = POOL_STATE + 1
D_FF = 4 * D_MODEL
EPS = 1e-6
INV_SQRT_HD = 1.0 / math.sqrt(HEAD_DIM)

SB_BLOCK = 128
SB_LATE_ROWS = 64
SB_GROUP = 32
EXP_ZERO_BELOW = -104.0
RUN_FINISHED = -1e30
ROW_TILE = 512
VMEM_LIMIT = 48 * 1024 * 1024
BIG_VMEM_LIMIT = 60 * 1024 * 1024

BF16 = jnp.bfloat16
F32 = jnp.float32


def _params(*sem):
    return pltpu.CompilerParams(dimension_semantics=sem, vmem_limit_bytes=VMEM_LIMIT)


def _rms(xf, g):
    return xf * lax.rsqrt(jnp.mean(xf * xf, axis=-1, keepdims=True) + EPS) * g


def _dot(a, b):
    return jnp.dot(a, b, preferred_element_type=F32)


def _dot_nt(a, b):
    return lax.dot_general(a, b, (((1,), (1,)), ((), ())), preferred_element_type=F32)


def _resident(shape, layer):
    return pl.BlockSpec((None,) + shape, lambda *_: (layer,) + (0,) * len(shape), pipeline_mode=pl.Buffered(1))


def _norm_matmul_body(x_ref, g_ref, w_ref, o_ref):
    o_ref[...] = _dot(_rms(x_ref[...], g_ref[...]).astype(BF16), w_ref[...])


def _norm_matmul(x, g, w, layer):
    n, d = x.shape
    m = w.shape[2]
    tm = min(n, 2 * ROW_TILE)
    return pl.pallas_call(
        _norm_matmul_body,
        grid=(n // tm,),
        in_specs=[
            pl.BlockSpec((tm, d), lambda i: (i, 0)),
            pl.BlockSpec((1, d), lambda i: (0, 0)),
            _resident((d, m), layer),
        ],
        out_specs=pl.BlockSpec((tm, m), lambda i: (i, 0)),
        out_shape=jax.ShapeDtypeStruct((n, m), F32),
        compiler_params=pltpu.CompilerParams(dimension_semantics=("parallel",), vmem_limit_bytes=BIG_VMEM_LIMIT),
        name="norm_matmul",
    )(x, g.reshape(1, d), w)


def _inproj_sb_body(x_ref, g_ref, wq_ref, wk_ref, wv_ref, wm_ref,
                    q_ref, k_ref, v_ref, kb_ref, vb_ref, qm_ref, h_ref):
    @pl.when(pl.program_id(1) == 0)
    def _():
        h_ref[...] = _rms(x_ref[...], g_ref[...]).astype(BF16)
        qm_ref[...] = _dot(h_ref[...], wm_ref[...])

    def put(refs, val):
        bb, heads, tt, _ = refs[0].shape
        for bi in range(bb):
            for hd in range(heads):
                piece = val[bi * tt:(bi + 1) * tt, hd * HEAD_DIM:(hd + 1) * HEAD_DIM]
                for ref in refs:
                    ref[bi, hd] = piece.astype(ref.dtype)

    h = h_ref[...]
    put([q_ref], _dot(h, wq_ref[...]) * INV_SQRT_HD)
    put([k_ref, kb_ref], _dot(h, wk_ref[...]))
    put([v_ref, vb_ref], _dot(h, wv_ref[...]))


def _inproj_sb(x, g, w, layer, b, t):
    n, d = x.shape
    tm = min(n, 2 * ROW_TILE)
    tn = D_MEM
    nj = D_TOK // tn
    hj = tn // HEAD_DIM
    if t >= tm:
        per_seq = t // tm
        oblock, omap = (1, hj, tm, HEAD_DIM), lambda i, j: (i // per_seq, j, i % per_seq, 0)
    else:
        oblock, omap = (tm // t, hj, t, HEAD_DIM), lambda i, j: (i, j, 0, 0)
    tok = lambda dt: jax.ShapeDtypeStruct((b, SB_HEADS, t, HEAD_DIM), dt)
    wspec = lambda off: pl.BlockSpec((None, d, tn), lambda i, j: (layer, 0, off + j))
    ospec = pl.BlockSpec(oblock, omap)
    return pl.pallas_call(
        _inproj_sb_body,
        grid=(n // tm, nj),
        in_specs=[
            pl.BlockSpec((tm, d), lambda i, j: (i, 0)),
            pl.BlockSpec((1, d), lambda i, j: (0, 0)),
            wspec(0), wspec(nj), wspec(2 * nj),
            pl.BlockSpec((None, d, D_MEM), lambda i, j: (layer, 0, 3 * D_TOK // D_MEM),
                         pipeline_mode=pl.Buffered(1)),
        ],
        out_specs=[ospec, ospec, ospec, ospec, ospec,
                   pl.BlockSpec((tm, D_MEM), lambda i, j: (i, 0))],
        out_shape=[tok(BF16), tok(F32), tok(F32), tok(BF16), tok(BF16),
                   jax.ShapeDtypeStruct((n, D_MEM), F32)],
        scratch_shapes=[pltpu.VMEM((tm, d), BF16)],
        compiler_params=pltpu.CompilerParams(dimension_semantics=("parallel", "arbitrary"),
                                             vmem_limit_bytes=BIG_VMEM_LIMIT),
        name="inproj_sb",
    )(x, g.reshape(1, d), w, w, w, w)


def _head_rows(hd, n):
    return pl.ds(hd, n, stride=MEM_HEADS)


def _mem_kv_body(x_ref, g_ref, w_ref, kg_ref, k_ref, v_ref):
    tm = x_ref.shape[0]
    h = _rms(x_ref[...], g_ref[...]).astype(BF16)
    kv = _dot(h, w_ref[...].astype(BF16))
    for hd in range(MEM_HEADS):
        sl = slice(hd * HEAD_DIM, (hd + 1) * HEAD_DIM)
        k_ref[_head_rows(hd, tm), :] = _rms(kv[:, sl], kg_ref[...])
        v_ref[_head_rows(hd, tm), :] = kv[:, D_MEM + hd * HEAD_DIM:D_MEM + (hd + 1) * HEAD_DIM]


def _mem_kv(mem, g_mem, w_kv, k_gain):
    r, d = mem.shape
    depth = w_kv.shape[0]
    tm = min(r, 2 * ROW_TILE)
    out = jax.ShapeDtypeStruct((depth, r * MEM_HEADS, HEAD_DIM), F32)
    ospec = pl.BlockSpec((None, tm * MEM_HEADS, HEAD_DIM), lambda l, i: (l, i, 0))
    return pl.pallas_call(
        _mem_kv_body,
        grid=(depth, r // tm),
        in_specs=[
            pl.BlockSpec((tm, d), lambda l, i: (i, 0)),
            pl.BlockSpec((None, 1, d), lambda l, i: (l, 0, 0)),
            pl.BlockSpec((None, d, 2 * D_MEM), lambda l, i: (l, 0, 0)),
            pl.BlockSpec((None, 1, HEAD_DIM), lambda l, i: (l, 0, 0)),
        ],
        out_specs=[ospec, ospec],
        out_shape=[out, out],
        compiler_params=_params("parallel", "parallel"),
        name="mem_kv",
    )(mem, g_mem.reshape(depth, 1, d), w_kv, k_gain.reshape(depth, 1, HEAD_DIM))


def _pool_body(u_ref, halo_ref, prev_ref, w_ref, s_ref, o_ref, ext_ref, a_ref, b_ref, *, tm, past):
    i = pl.program_id(1)
    base = 2 * POOL_HALO
    rows = tm + POOL_HALO
    ext_ref[0:POOL_HALO, :] = jnp.zeros((POOL_HALO, D_TOK), F32)
    a_ref[0:POOL_HALO, :] = jnp.zeros((POOL_HALO, POOL_GROUP), F32)
    b_ref[0:POOL_HALO, :] = jnp.zeros((POOL_HALO, POOL_GROUP), F32)

    @pl.when(i == 0)
    def _():
        ext_ref[POOL_HALO:base, :] = prev_ref[...]

    @pl.when(i > 0)
    def _():
        ext_ref[POOL_HALO:base, :] = halo_ref[...]

    ext_ref[base:, :] = u_ref[...]
    pos = (past + i * tm + lax.broadcasted_iota(jnp.int32, (tm, 1), 0)).astype(F32)
    for g, w in enumerate(POOL_WINDOWS):
        sl = slice(g * POOL_GROUP, (g + 1) * POOL_GROUP)
        src, cols, nxt = ext_ref, sl, (a_ref, b_ref)
        span = 1
        while span < w:
            part = src[POOL_HALO:POOL_HALO + rows, cols] + src[POOL_HALO - span:POOL_HALO - span + rows, cols]
            span *= 2
            if span < w:
                nxt[0][POOL_HALO:POOL_HALO + rows, :] = part
                src, cols, nxt = nxt[0], slice(None), nxt[::-1]
        win = part[POOL_HALO:, :]
        cur = ext_ref[base:, sl]
        cnt = jnp.minimum(pos + 1.0, float(w))
        dlt = (win / cnt - cur).astype(BF16)
        o_ref[:, sl] = (_dot(dlt, w_ref[g]) * s_ref[:, sl]).astype(BF16)


def _pool_mix(proj, prev, w_grp, scale, past):
    b, t, _ = proj.shape
    tm = min(t, 2 * ROW_TILE)
    hb = tm // POOL_HALO
    return pl.pallas_call(
        functools.partial(_pool_body, tm=tm, past=past),
        grid=(b, t // tm),
        in_specs=[
            pl.BlockSpec((None, tm, D_TOK), lambda bi, i: (bi, i, 0)),
            pl.BlockSpec((None, POOL_HALO, D_TOK), lambda bi, i: (bi, jnp.maximum(i * hb - 1, 0), 0)),
            pl.BlockSpec((None, POOL_HALO, D_TOK), lambda bi, i: (bi, 0, 0)),
            pl.BlockSpec((len(POOL_WINDOWS), POOL_GROUP, POOL_GROUP), lambda bi, i: (0, 0, 0)),
            pl.BlockSpec((1, D_TOK), lambda bi, i: (0, 0)),
        ],
        out_specs=pl.BlockSpec((None, tm, D_TOK), lambda bi, i: (bi, i, 0)),
        out_shape=jax.ShapeDtypeStruct((b, t, D_TOK), BF16),
        scratch_shapes=[pltpu.VMEM((tm + 2 * POOL_HALO, D_TOK), F32),
                        pltpu.VMEM((tm + 2 * POOL_HALO, POOL_GROUP), F32),
                        pltpu.VMEM((tm + 2 * POOL_HALO, POOL_GROUP), F32)],
        compiler_params=_params("parallel", "arbitrary"),
        name="pool_mix",
    )(proj, proj, prev, w_grp, scale.reshape(1, D_TOK))


def _scan_matrix():
    shape = (2 * SB_BLOCK, SB_BLOCK + HEAD_DIM)
    r = lax.broadcasted_iota(jnp.int32, shape, 0) % SB_BLOCK
    c = lax.broadcasted_iota(jnp.int32, shape, 1)
    return jnp.where((c >= SB_BLOCK) | (r > c), 1.0, 0.0).astype(BF16)


def _sb_steps(qs, ks, vs, runs, mask, scan):
    tq = qs[0].shape[0]
    zs = [_dot_nt(q, k) for q, k in zip(qs, ks)]
    fails, parts = [], []
    for z in zs:
        log_fail = -(jnp.maximum(z, 0.0) + jnp.log(1.0 + jnp.exp(-jnp.abs(z))))
        if mask is not None:
            log_fail = jnp.where(mask, log_fail, 0.0)
        hi = log_fail.astype(BF16)
        lo = (log_fail - hi.astype(F32)).astype(BF16)
        fails.append(log_fail)
        parts.append(jnp.concatenate([hi, lo], axis=1))
    sums = _dot(jnp.concatenate(parts, axis=0), scan)
    weights, new_runs = [], []
    for g, (z, log_fail, run) in enumerate(zip(zs, fails, runs)):
        s = sums[g * tq:(g + 1) * tq]
        a = jnp.exp(z + log_fail + s[:, :SB_BLOCK] + run)
        if mask is not None:
            a = jnp.where(mask, a, 0.0)
        weights.append(a.astype(BF16))
        new_runs.append(run + s[:, SB_BLOCK:])
    return new_runs, [_dot(a, v) for a, v in zip(weights, vs)]


def _largest(values):
    out = values[0]
    for v in values[1:]:
        out = jnp.maximum(out, v)
    return jnp.max(out)


def _sb_prompt_body(q_ref, k_ref, v_ref, o_ref, run_ref, acc_ref, scan_ref, *, t, group):
    scan_ref[...] = _scan_matrix()
    below_diag = (lax.broadcasted_iota(jnp.int32, (SB_BLOCK, SB_BLOCK), 1) <
                  lax.broadcasted_iota(jnp.int32, (SB_BLOCK, SB_BLOCK), 0))
    zero = jnp.zeros((SB_BLOCK, HEAD_DIM), F32)
    chain = [slice(g * SB_BLOCK, (g + 1) * SB_BLOCK) for g in range(group)]

    def block(i):
        return pl.ds(pl.multiple_of(i * SB_BLOCK, SB_BLOCK), SB_BLOCK)

    def group_scan(s, _):
        first = s * group
        tiles = [block(first + g) for g in range(group)]
        runs, pvs = _sb_steps([q_ref[r, :] for r in tiles], [k_ref[r, :] for r in tiles],
                              [v_ref[r, :] for r in tiles], [zero] * group, below_diag, scan_ref[...])
        for sl, run, pv in zip(chain, runs, pvs):
            run_ref[sl, :] = run
            acc_ref[sl, :] = pv

        def split(runs):
            return (_largest([r[:SB_LATE_ROWS] for r in runs]), _largest([r[SB_LATE_ROWS:] for r in runs]))

        def visit(it, n):
            keys = [block(jnp.maximum(first + g - it, 0)) for g in range(group)]
            rows = [slice(g * SB_BLOCK, g * SB_BLOCK + n) for g in range(group)]
            runs = [jnp.where(first + g - it >= 0, run_ref[r, :], RUN_FINISHED) for g, r in enumerate(rows)]
            queries = [q_ref[pl.ds(pl.multiple_of((first + g) * SB_BLOCK, SB_BLOCK), n), :] for g in range(group)]
            runs, pvs = _sb_steps(queries, [k_ref[r, :] for r in keys], [v_ref[r, :] for r in keys], runs, None,
                                  scan_ref[...])
            for r, run, pv in zip(rows, runs, pvs):
                run_ref[r, :] = run
                acc_ref[r, :] += pv
            return runs

        def more(c):
            it, late, rest = c
            return jnp.logical_and(it < first + group, jnp.maximum(late, rest) > EXP_ZERO_BELOW)

        def step(c):
            it, _, rest = c
            late, rest = lax.cond(rest > EXP_ZERO_BELOW,
                                  lambda: split(visit(it, SB_BLOCK)),
                                  lambda: (_largest(visit(it, SB_LATE_ROWS)), rest))
            return it + 1, late, rest

        lax.while_loop(more, step, (jnp.int32(1),) + split(runs))
        out_rows = pl.ds(pl.multiple_of(first * SB_BLOCK, SB_BLOCK), group * SB_BLOCK)
        o_ref[out_rows, :] = acc_ref[...].astype(o_ref.dtype)
        return 0

    lax.fori_loop(0, t // (group * SB_BLOCK), group_scan, 0)


def _sb_attn_prompt(q, k, v):
    b, _, t, _ = q.shape
    group = min(SB_GROUP, t // SB_BLOCK)
    assert t % (group * SB_BLOCK) == 0, t
    spec = pl.BlockSpec((None, None, t, HEAD_DIM), lambda bi, h: (bi, h, 0, 0))
    return pl.pallas_call(
        functools.partial(_sb_prompt_body, t=t, group=group),
        grid=(b, SB_HEADS),
        in_specs=[spec, spec, spec],
        out_specs=spec,
        out_shape=jax.ShapeDtypeStruct((b, SB_HEADS, t, HEAD_DIM), BF16),
        scratch_shapes=[pltpu.VMEM((group * SB_BLOCK, HEAD_DIM), F32),
                        pltpu.VMEM((group * SB_BLOCK, HEAD_DIM), F32),
                        pltpu.VMEM((2 * SB_BLOCK, SB_BLOCK + HEAD_DIM), BF16)],
        compiler_params=_params("parallel", "parallel"),
        name="sb_attn_prompt",
    )(q, k, v)


def _sb_cached_body(q_ref, kn_ref, vn_ref, kr_ref, vr_ref, k_hbm, v_hbm, o_ref,
                    run_ref, acc_ref, scan_ref, kbuf_ref, vbuf_ref, sem, *, t, past, recent):
    scan_ref[...] = _scan_matrix()
    earlier = (lax.broadcasted_iota(jnp.int32, (t, SB_BLOCK), 1) <
               lax.broadcasted_iota(jnp.int32, (t, SB_BLOCK), 0))
    zero = jnp.zeros((t, HEAD_DIM), F32)
    pad = jnp.zeros((SB_BLOCK - t, HEAD_DIM), BF16)
    heads = range(SB_HEADS)
    chain = [slice(h * t, (h + 1) * t) for h in heads]

    runs, pvs = _sb_steps([q_ref[h] for h in heads],
                          [jnp.concatenate([kn_ref[h], pad], axis=0) for h in heads],
                          [jnp.concatenate([vn_ref[h], pad], axis=0) for h in heads],
                          [zero] * SB_HEADS, earlier, scan_ref[...])
    for sl, run, pv in zip(chain, runs, pvs):
        run_ref[sl, :] = run
        acc_ref[sl, :] = pv

    def visit(ks, vs):
        runs, pvs = _sb_steps([q_ref[h] for h in heads], ks, vs, [run_ref[sl, :] for sl in chain], None,
                              scan_ref[...])
        for sl, run, pv in zip(chain, runs, pvs):
            run_ref[sl, :] = run
            acc_ref[sl, :] += pv
        return _largest(runs)

    def more(blocks):
        return lambda c: jnp.logical_and(c[0] < blocks, c[1] > EXP_ZERO_BELOW)

    def recent_step(c):
        it, _ = c
        keys = pl.ds(pl.multiple_of(recent - (it + 1) * SB_BLOCK, SB_BLOCK), SB_BLOCK)
        return it + 1, visit([kr_ref[h, keys, :].astype(BF16) for h in heads],
                             [vr_ref[h, keys, :].astype(BF16) for h in heads])

    def older_step(c):
        it, _ = c
        bi = pl.program_id(0)
        keys = pl.ds(pl.multiple_of(past - (it + 1) * SB_BLOCK, SB_BLOCK), SB_BLOCK)
        copies = [pltpu.make_async_copy(src.at[bi, :, keys, :], dst, sem.at[n])
                  for n, (src, dst) in enumerate(((k_hbm, kbuf_ref), (v_hbm, vbuf_ref)))]
        for cp in copies:
            cp.start()
        for cp in copies:
            cp.wait()
        return it + 1, visit([kbuf_ref[h].astype(BF16) for h in heads], [vbuf_ref[h].astype(BF16) for h in heads])

    state = lax.while_loop(more(recent // SB_BLOCK), recent_step, (jnp.int32(0), _largest(runs)))
    lax.while_loop(more(past // SB_BLOCK), older_step, state)
    for h, sl in zip(heads, chain):
        o_ref[h] = acc_ref[sl, :].astype(o_ref.dtype)


def _sb_attn_cached(q, k_new, v_new, k_past, v_past):
    b, _, t, _ = q.shape
    past = k_past.shape[2]
    assert past % SB_BLOCK == 0 and t <= SB_BLOCK and t % 16 == 0, (past, t)
    recent = min(past, 2 * SB_BLOCK)
    assert past % recent == 0, past
    new = pl.BlockSpec((None, SB_HEADS, t, HEAD_DIM), lambda bi: (bi, 0, 0, 0))
    last = pl.BlockSpec((None, SB_HEADS, recent, HEAD_DIM), lambda bi: (bi, 0, past // recent - 1, 0))
    whole = pl.BlockSpec(memory_space=pl.ANY)
    return pl.pallas_call(
        functools.partial(_sb_cached_body, t=t, past=past, recent=recent),
        grid=(b,),
        in_specs=[new, new, new, last, last, whole, whole],
        out_specs=new,
        out_shape=jax.ShapeDtypeStruct((b, SB_HEADS, t, HEAD_DIM), BF16),
        scratch_shapes=[pltpu.VMEM((SB_HEADS * t, HEAD_DIM), F32),
                        pltpu.VMEM((SB_HEADS * t, HEAD_DIM), F32),
                        pltpu.VMEM((2 * SB_BLOCK, SB_BLOCK + HEAD_DIM), BF16),
                        pltpu.VMEM((SB_HEADS, SB_BLOCK, HEAD_DIM), F32),
                        pltpu.VMEM((SB_HEADS, SB_BLOCK, HEAD_DIM), F32),
                        pltpu.SemaphoreType.DMA((2,))],
        compiler_params=_params("parallel"),
        name="sb_attn_cached",
    )(q, k_new, v_new, k_past, v_past, k_past, v_past)


def _mix_out_body(x_ref, mix_ref, qm_ref, mk_ref, mv_ref, qg_ref, w_ref, o_ref):
    bb = mk_ref.shape[0]
    n_mem = mk_ref.shape[1] // MEM_HEADS
    tt = x_ref.shape[0] // bb
    probs = []
    for bi in range(bb):
        rows = slice(bi * tt, (bi + 1) * tt)
        for hd in range(MEM_HEADS):
            sl = slice(hd * HEAD_DIM, (hd + 1) * HEAD_DIM)
            qh = _rms(qm_ref[rows, sl], qg_ref[...]).astype(BF16)
            s = _dot_nt(qh, mk_ref[bi, _head_rows(hd, n_mem), :].astype(BF16)) * INV_SQRT_HD
            p = jnp.exp(s - jnp.max(s, axis=-1, keepdims=True))
            probs.append((p / jnp.sum(p, axis=-1, keepdims=True)).astype(BF16))
    if len(mix_ref.shape) == 4:
        mix = jnp.concatenate([jnp.concatenate([mix_ref[bi, hd] for hd in range(SB_HEADS)], axis=1)
                               for bi in range(mix_ref.shape[0])], axis=0)
    else:
        mix = mix_ref[...]
    y = x_ref[...] + _dot(mix, w_ref[:D_TOK, :])
    mo = jnp.concatenate(
        [jnp.concatenate([_dot(probs[bi * MEM_HEADS + hd], mv_ref[bi, _head_rows(hd, n_mem), :].astype(BF16))
                          for hd in range(MEM_HEADS)], axis=1) for bi in range(bb)], axis=0)
    o_ref[...] = y + _dot(mo.astype(BF16), w_ref[D_TOK:, :])


def _mix_out(x, mix, qm_src, qm_col, mem_k, mem_v, q_gain, w_out, layer, t):
    n, d = x.shape
    mem_rows = mem_k.shape[2]
    tm = min(n, ROW_TILE)
    row = lambda c, col=0: pl.BlockSpec((tm, c), lambda i: (i, col))
    if t >= tm:
        per_seq = t // tm
        mem = pl.BlockSpec((None, 1, mem_rows, HEAD_DIM), lambda i: (layer, i // per_seq, 0, 0))
        heads = pl.BlockSpec((1, SB_HEADS, tm, HEAD_DIM), lambda i: (i // per_seq, 0, i % per_seq, 0))
    else:
        mem = pl.BlockSpec((None, tm // t, mem_rows, HEAD_DIM), lambda i: (layer, i, 0, 0),
                           pipeline_mode=pl.Buffered(1))
        heads = pl.BlockSpec((tm // t, SB_HEADS, t, HEAD_DIM), lambda i: (i, 0, 0, 0))
    return pl.pallas_call(
        _mix_out_body,
        grid=(n // tm,),
        in_specs=[row(d), heads if mix.ndim == 4 else row(D_TOK), row(D_MEM, qm_col), mem, mem,
                  pl.BlockSpec((1, HEAD_DIM), lambda i: (0, 0)), _resident((d, d), layer)],
        out_specs=row(d),
        out_shape=jax.ShapeDtypeStruct((n, d), F32),
        compiler_params=_params("parallel"),
        name="mix_out",
    )(x, mix, qm_src, mem_k, mem_v, q_gain.reshape(1, HEAD_DIM), w_out)


def _ffn_body(x_ref, g_ref, wu_ref, wd_ref, o_ref, *rest):
    h_ref = rest[-1]

    @pl.when(pl.program_id(1) == 0)
    def _():
        x = x_ref[...]
        h_ref[...] = _rms(x, g_ref[...]).astype(BF16)
        o_ref[...] = x

    wu = wu_ref[...].astype(BF16)
    wd = wd_ref[...].astype(BF16)
    if len(rest) == 3:
        rest[0][...] = wu
        rest[1][...] = wd
    pieces = max(1, wu.shape[1] // 1024)
    half = wu.shape[1] // pieces
    for c in range(pieces):
        cols = slice(c * half, (c + 1) * half)
        up = jnp.maximum(_dot(h_ref[...], wu[:, cols]), 0.0)
        o_ref[...] += _dot((up * up).astype(BF16), wd[cols, :])


def _ffn(x, g, w_up, w_down):
    n, d = x.shape
    f = w_up.shape[1]
    tm = min(n, ROW_TILE)
    tk = 2048
    return pl.pallas_call(
        _ffn_body,
        grid=(n // tm, f // tk),
        in_specs=[
            pl.BlockSpec((tm, d), lambda i, k: (i, 0)),
            pl.BlockSpec((1, d), lambda i, k: (0, 0)),
            pl.BlockSpec((d, tk), lambda i, k: (0, k)),
            pl.BlockSpec((tk, d), lambda i, k: (k, 0)),
        ],
        out_specs=pl.BlockSpec((tm, d), lambda i, k: (i, 0)),
        out_shape=jax.ShapeDtypeStruct((n, d), F32),
        scratch_shapes=[pltpu.VMEM((tm, d), BF16)],
        compiler_params=pltpu.CompilerParams(dimension_semantics=("parallel", "arbitrary"),
                                             vmem_limit_bytes=BIG_VMEM_LIMIT),
        name="ffn",
    )(x, g.reshape(1, d), w_up, w_down)


def _ffn_casting(x, g, w_up, w_down, layer):
    n, d = x.shape
    f = w_up.shape[2]
    tk = 512
    assert n <= ROW_TILE, n
    return pl.pallas_call(
        _ffn_body,
        grid=(1, f // tk),
        in_specs=[
            pl.BlockSpec((n, d), lambda i, k: (0, 0)),
            pl.BlockSpec((1, d), lambda i, k: (0, 0)),
            pl.BlockSpec((None, d, tk), lambda i, k: (layer, 0, k)),
            pl.BlockSpec((None, tk, d), lambda i, k: (layer, k, 0)),
        ],
        out_specs=[pl.BlockSpec((n, d), lambda i, k: (0, 0)),
                   pl.BlockSpec((d, tk), lambda i, k: (0, k)),
                   pl.BlockSpec((tk, d), lambda i, k: (k, 0))],
        out_shape=[jax.ShapeDtypeStruct((n, d), F32), jax.ShapeDtypeStruct((d, f), BF16),
                   jax.ShapeDtypeStruct((f, d), BF16)],
        scratch_shapes=[pltpu.VMEM((n, d), BF16)],
        compiler_params=_params("arbitrary", "arbitrary"),
        name="ffn_casting",
    )(x, g.reshape(1, d), w_up, w_down)


def _run_trunk(x, past, pool_prev, sb_past_k, sb_past_v, mem_k, mem_v, p, ffn_bf16):
    b, t, d = x.shape
    n = b * t
    depth = p["w_out"].shape[0]
    pool_new, sbk_new, sbv_new, made = [], [], [], []
    x2 = x.reshape(n, d)
    for i in range(depth):
        j = i // 2
        if i % 2 == 0:
            proj = _norm_matmul(x2, p["norm_mix"][i], p["w_in_pool"], j)
            proj3 = proj.reshape(b, t, d)
            mix = _pool_mix(proj3, pool_prev[j], p["pool_w"][j], p["pool_scale"][j], past).reshape(n, D_TOK)
            pool_new.append(proj3[:, t - POOL_STATE:, :D_TOK])
            qm_src, qm_col = proj, D_TOK // D_MEM
        else:
            q, k, v, kb, vb, qm = _inproj_sb(x2, p["norm_mix"][i], p["w_in_sb"], j, b, t)
            if sb_past_k is None:
                mix = _sb_attn_prompt(q, kb, vb)
            else:
                mix = _sb_attn_cached(q, kb, vb, sb_past_k[j], sb_past_v[j])
            sbk_new.append(k)
            sbv_new.append(v)
            qm_src, qm_col = qm, 0
        x2 = _mix_out(x2, mix, qm_src, qm_col, mem_k, mem_v, p["q_norm"][i], p["w_out"], i, t)
        if ffn_bf16 is None:
            x2, wu, wd = _ffn_casting(x2, p["norm_ffn"][i], p["w_up"], p["w_down"], i)
            made.append((wu, wd))
        else:
            x2 = _ffn(x2, p["norm_ffn"][i], *ffn_bf16[i])
    return x2.reshape(b, t, d), jnp.stack(pool_new), jnp.stack(sbk_new), jnp.stack(sbv_new), made


def kernel(x_prompt, x_sample, mem_prompt, cache_pool, cache_sb_k, cache_sb_v, cache_mem_k, cache_mem_v,
           norm_mix, w_in_pool, w_in_sb, pool_w, pool_scale, norm_mem, w_mem_kv, q_norm, k_norm, w_out,
           norm_ffn, w_up, w_down):
    b, t, d = x_prompt.shape
    depth = w_out.shape[0]
    n_mem = mem_prompt.shape[1]
    p = dict(norm_mix=norm_mix, pool_scale=pool_scale, q_norm=q_norm, norm_ffn=norm_ffn,
             w_in_pool=w_in_pool.astype(BF16), w_in_sb=w_in_sb.astype(BF16), pool_w=pool_w.astype(BF16),
             w_out=w_out.astype(BF16), w_up=w_up, w_down=w_down)

    bs = x_sample.shape[0]
    past = cache_sb_k.shape[2]
    pool_prev = jnp.pad(cache_pool, ((0, 0), (0, 0), (POOL_HALO - POOL_STATE, 0), (0, 0)))
    head_major = lambda a: jnp.transpose(a, (0, 1, 3, 2, 4))
    mem_rows = lambda a, nb: a.reshape(depth, nb, n_mem * MEM_HEADS, HEAD_DIM)
    y_sample, pool_sample, sb_k_sample, sb_v_sample, ffn_bf16 = _run_trunk(
        x_sample, past, pool_prev, head_major(cache_sb_k), head_major(cache_sb_v),
        mem_rows(cache_mem_k, bs), mem_rows(cache_mem_v, bs), p, None)

    mk, mv = _mem_kv(mem_prompt.reshape(b * n_mem, d), norm_mem, w_mem_kv, k_norm)
    mk = mem_rows(mk, b)
    mv = mem_rows(mv, b)
    pool_zero = jnp.zeros((cache_pool.shape[0], b, POOL_HALO, D_TOK), F32)
    y_prompt, pool_prompt, sb_k_prompt, sb_v_prompt, _ = _run_trunk(
        x_prompt, 0, pool_zero, None, None, mk, mv, p, ffn_bf16)

    heads = lambda a: a.reshape(depth, b, n_mem, MEM_HEADS, HEAD_DIM)
    return (y_prompt, y_sample, pool_prompt, pool_sample, head_major(sb_k_prompt), head_major(sb_v_prompt),
            head_major(sb_k_sample), head_major(sb_v_sample), heads(mk), heads(mv))
```

```python
import functools
import math

import jax
import jax.numpy as jnp
from jax import lax
from jax.experimental import pallas as pl
from jax.experimental.pallas import tpu as pltpu

D_MODEL = 2048
HEAD_DIM = 128
D_TOK = 3 * D_MODEL // 4
D_MEM = D_MODEL // 4
SB_HEADS = D_TOK // HEAD_DIM
MEM_HEADS = D_MEM // HEAD_DIM
POOL_WINDOWS = (2, 4, 8, 16)
POOL_GROUP = D_TOK // len(POOL_WINDOWS)
POOL_STATE = max(POOL_WINDOWS) - 1
POOL_HALO = POOL_STATE + 1
D_FF = 4 * D_MODEL
EPS = 1e-6
INV_SQRT_HD = 1.0 / math.sqrt(HEAD_DIM)

SB_BLOCK = 128
SB_LATE_ROWS = 64
SB_GROUP = 32
EXP_ZERO_BELOW = -104.0
RUN_FINISHED = -1e30
ROW_TILE = 512
FFN_BLOCK = 2048
FFN_PIECE = 1024
FFN_CAST_BLOCK = 512
VMEM_LIMIT = 48 * 1024 * 1024
BIG_VMEM_LIMIT = 60 * 1024 * 1024

BF16 = jnp.bfloat16
F32 = jnp.float32


def _params(*sem, vmem=VMEM_LIMIT):
    return pltpu.CompilerParams(dimension_semantics=sem, vmem_limit_bytes=vmem)


def _rms(xf, g):
    return xf * lax.rsqrt(jnp.mean(xf * xf, axis=-1, keepdims=True) + EPS) * g


def _dot(a, b):
    return jnp.dot(a, b, preferred_element_type=F32)


def _dot_nt(a, b):
    return lax.dot_general(a, b, (((1,), (1,)), ((), ())), preferred_element_type=F32)


def _resident(shape, layer):
    return pl.BlockSpec((None,) + shape, lambda *_: (layer,) + (0,) * len(shape), pipeline_mode=pl.Buffered(1))


def _norm_matmul_body(x_ref, g_ref, w_ref, o_ref):
    o_ref[...] = _dot(_rms(x_ref[...], g_ref[...]).astype(BF16), w_ref[...])


def _norm_matmul(x, g, w, layer):
    n, d = x.shape
    m = w.shape[2]
    tm = min(n, 2 * ROW_TILE)
    return pl.pallas_call(
        _norm_matmul_body,
        grid=(n // tm,),
        in_specs=[
            pl.BlockSpec((tm, d), lambda i: (i, 0)),
            pl.BlockSpec((1, d), lambda i: (0, 0)),
            _resident((d, m), layer),
        ],
        out_specs=pl.BlockSpec((tm, m), lambda i: (i, 0)),
        out_shape=jax.ShapeDtypeStruct((n, m), F32),
        compiler_params=_params("parallel", vmem=BIG_VMEM_LIMIT),
        name="norm_matmul",
    )(x, g.reshape(1, d), w)


def _inproj_sb_body(x_ref, g_ref, wq_ref, wk_ref, wv_ref, wm_ref,
                    q_ref, k_ref, v_ref, kb_ref, vb_ref, qm_ref, *rest):
    h_ref = rest[-1]

    def weight(w_ref, n):
        w = w_ref[...].astype(BF16)
        if len(rest) == 5:
            rest[n][...] = w
        return w

    @pl.when(pl.program_id(1) == 0)
    def _():
        h_ref[...] = _rms(x_ref[...], g_ref[...]).astype(BF16)
        qm_ref[...] = _dot(h_ref[...], weight(wm_ref, 3))

    def put(refs, val):
        bb, heads, tt, _ = refs[0].shape
        for bi in range(bb):
            for hd in range(heads):
                piece = val[bi * tt:(bi + 1) * tt, hd * HEAD_DIM:(hd + 1) * HEAD_DIM]
                for ref in refs:
                    ref[bi, hd] = piece.astype(ref.dtype)

    h = h_ref[...]
    put([q_ref], _dot(h, weight(wq_ref, 0)) * INV_SQRT_HD)
    put([k_ref, kb_ref], _dot(h, weight(wk_ref, 1)))
    put([v_ref, vb_ref], _dot(h, weight(wv_ref, 2)))


def _inproj_sb(x, g, w, b, t, layer=None):
    n, d = x.shape
    casting = layer is not None
    tm = min(n, 2 * ROW_TILE)
    tn = 2 * HEAD_DIM if casting else D_MEM
    nj = D_TOK // tn
    hj = tn // HEAD_DIM
    if t >= tm:
        per_seq = t // tm
        oblock, omap = (1, hj, tm, HEAD_DIM), lambda i, j: (i // per_seq, j, i % per_seq, 0)
    else:
        oblock, omap = (tm // t, hj, t, HEAD_DIM), lambda i, j: (i, j, 0, 0)
    tok = lambda dt: jax.ShapeDtypeStruct((b, SB_HEADS, t, HEAD_DIM), dt)
    ospec = pl.BlockSpec(oblock, omap)
    col_block = pl.BlockSpec((d, tn), lambda i, j: (0, j))
    whole_wm = lambda **kw: pl.BlockSpec((d, D_MEM), lambda i, j: (0, 0), **kw)
    out_specs = [ospec, ospec, ospec, ospec, ospec, pl.BlockSpec((tm, D_MEM), lambda i, j: (i, 0))]
    out_shape = [tok(BF16), tok(F32), tok(F32), tok(BF16), tok(BF16), jax.ShapeDtypeStruct((n, D_MEM), F32)]
    if casting:
        assert n == tm, (n, tm)
        wspec = lambda off: pl.BlockSpec((None, d, tn), lambda i, j: (layer, 0, off + j))
        w_specs = [wspec(0), wspec(nj), wspec(2 * nj),
                   pl.BlockSpec((None, d, D_MEM), lambda i, j: (layer, 0, 3 * D_TOK // D_MEM),
                                pipeline_mode=pl.Buffered(1))]
        w_args = (w, w, w, w)
        out_specs += [col_block, col_block, col_block, whole_wm()]
        out_shape += [jax.ShapeDtypeStruct((d, D_TOK), BF16)] * 3 + [jax.ShapeDtypeStruct((d, D_MEM), BF16)]
    else:
        w_specs = [col_block, col_block, col_block, whole_wm(pipeline_mode=pl.Buffered(1))]
        w_args = tuple(w)
    out = pl.pallas_call(
        _inproj_sb_body,
        grid=(n // tm, nj),
        in_specs=[pl.BlockSpec((tm, d), lambda i, j: (i, 0)), pl.BlockSpec((1, d), lambda i, j: (0, 0))] + w_specs,
        out_specs=out_specs,
        out_shape=out_shape,
        scratch_shapes=[pltpu.VMEM((tm, d), BF16)],
        compiler_params=_params("arbitrary" if casting else "parallel", "arbitrary", vmem=BIG_VMEM_LIMIT),
        name="inproj_sb",
    )(x, g.reshape(1, d), *w_args)
    return (*out[:6], tuple(out[6:])) if casting else out


def _head_rows(hd, n):
    return pl.ds(hd, n, stride=MEM_HEADS)


def _mem_kv_body(x_ref, g_ref, w_ref, kg_ref, k_ref, v_ref):
    tm = x_ref.shape[0]
    h = _rms(x_ref[...], g_ref[...]).astype(BF16)
    kv = _dot(h, w_ref[...].astype(BF16))
    for hd in range(MEM_HEADS):
        sl = slice(hd * HEAD_DIM, (hd + 1) * HEAD_DIM)
        k_ref[_head_rows(hd, tm), :] = _rms(kv[:, sl], kg_ref[...])
        v_ref[_head_rows(hd, tm), :] = kv[:, D_MEM + hd * HEAD_DIM:D_MEM + (hd + 1) * HEAD_DIM]


def _mem_kv(mem, g_mem, w_kv, k_gain):
    r, d = mem.shape
    depth = w_kv.shape[0]
    tm = min(r, 2 * ROW_TILE)
    out = jax.ShapeDtypeStruct((depth, r * MEM_HEADS, HEAD_DIM), F32)
    ospec = pl.BlockSpec((None, tm * MEM_HEADS, HEAD_DIM), lambda l, i: (l, i, 0))
    return pl.pallas_call(
        _mem_kv_body,
        grid=(depth, r // tm),
        in_specs=[
            pl.BlockSpec((tm, d), lambda l, i: (i, 0)),
            pl.BlockSpec((None, 1, d), lambda l, i: (l, 0, 0)),
            pl.BlockSpec((None, d, 2 * D_MEM), lambda l, i: (l, 0, 0)),
            pl.BlockSpec((None, 1, HEAD_DIM), lambda l, i: (l, 0, 0)),
        ],
        out_specs=[ospec, ospec],
        out_shape=[out, out],
        compiler_params=_params("parallel", "parallel"),
        name="mem_kv",
    )(mem, g_mem.reshape(depth, 1, d), w_kv, k_gain.reshape(depth, 1, HEAD_DIM))


def _pool_body(u_ref, halo_ref, prev_ref, w_ref, s_ref, o_ref, ext_ref, a_ref, b_ref, *, tm, past):
    i = pl.program_id(1)
    base = 2 * POOL_HALO
    rows = tm + POOL_HALO
    ext_ref[0:POOL_HALO, :] = jnp.zeros((POOL_HALO, D_TOK), F32)
    a_ref[0:POOL_HALO, :] = jnp.zeros((POOL_HALO, POOL_GROUP), F32)
    b_ref[0:POOL_HALO, :] = jnp.zeros((POOL_HALO, POOL_GROUP), F32)

    @pl.when(i == 0)
    def _():
        ext_ref[POOL_HALO:base, :] = prev_ref[...]

    @pl.when(i > 0)
    def _():
        ext_ref[POOL_HALO:base, :] = halo_ref[...]

    ext_ref[base:, :] = u_ref[...]
    pos = (past + i * tm + lax.broadcasted_iota(jnp.int32, (tm, 1), 0)).astype(F32)
    for g, w in enumerate(POOL_WINDOWS):
        sl = slice(g * POOL_GROUP, (g + 1) * POOL_GROUP)
        src, cols, nxt = ext_ref, sl, (a_ref, b_ref)
        span = 1
        while span < w:
            part = src[POOL_HALO:POOL_HALO + rows, cols] + src[POOL_HALO - span:POOL_HALO - span + rows, cols]
            span *= 2
            if span < w:
                nxt[0][POOL_HALO:POOL_HALO + rows, :] = part
                src, cols, nxt = nxt[0], slice(None), nxt[::-1]
        win = part[POOL_HALO:, :]
        cur = ext_ref[base:, sl]
        cnt = jnp.minimum(pos + 1.0, float(w))
        dlt = (win / cnt - cur).astype(BF16)
        o_ref[:, sl] = (_dot(dlt, w_ref[g]) * s_ref[:, sl]).astype(BF16)


def _pool_mix(proj, prev, w_grp, scale, past):
    b, t, _ = proj.shape
    tm = min(t, 2 * ROW_TILE)
    hb = tm // POOL_HALO
    return pl.pallas_call(
        functools.partial(_pool_body, tm=tm, past=past),
        grid=(b, t // tm),
        in_specs=[
            pl.BlockSpec((None, tm, D_TOK), lambda bi, i: (bi, i, 0)),
            pl.BlockSpec((None, POOL_HALO, D_TOK), lambda bi, i: (bi, jnp.maximum(i * hb - 1, 0), 0)),
            pl.BlockSpec((None, POOL_HALO, D_TOK), lambda bi, i: (bi, 0, 0)),
            pl.BlockSpec((len(POOL_WINDOWS), POOL_GROUP, POOL_GROUP), lambda bi, i: (0, 0, 0)),
            pl.BlockSpec((1, D_TOK), lambda bi, i: (0, 0)),
        ],
        out_specs=pl.BlockSpec((None, tm, D_TOK), lambda bi, i: (bi, i, 0)),
        out_shape=jax.ShapeDtypeStruct((b, t, D_TOK), BF16),
        scratch_shapes=[pltpu.VMEM((tm + 2 * POOL_HALO, D_TOK), F32),
                        pltpu.VMEM((tm + 2 * POOL_HALO, POOL_GROUP), F32),
                        pltpu.VMEM((tm + 2 * POOL_HALO, POOL_GROUP), F32)],
        compiler_params=_params("parallel", "arbitrary"),
        name="pool_mix",
    )(proj, proj, prev, w_grp, scale.reshape(1, D_TOK))


def _scan_matrix():
    shape = (2 * SB_BLOCK, SB_BLOCK + HEAD_DIM)
    r = lax.broadcasted_iota(jnp.int32, shape, 0) % SB_BLOCK
    c = lax.broadcasted_iota(jnp.int32, shape, 1)
    return jnp.where((c >= SB_BLOCK) | (r > c), 1.0, 0.0).astype(BF16)


def _sb_steps(qs, ks, vs, runs, mask, scan):
    tq = qs[0].shape[0]
    zs = [_dot_nt(q, k) for q, k in zip(qs, ks)]
    fails, parts = [], []
    for z in zs:
        log_fail = -(jnp.maximum(z, 0.0) + jnp.log(1.0 + jnp.exp(-jnp.abs(z))))
        if mask is not None:
            log_fail = jnp.where(mask, log_fail, 0.0)
        hi = log_fail.astype(BF16)
        lo = (log_fail - hi.astype(F32)).astype(BF16)
        fails.append(log_fail)
        parts.append(jnp.concatenate([hi, lo], axis=1))
    sums = _dot(jnp.concatenate(parts, axis=0), scan)
    weights, new_runs = [], []
    for g, (z, log_fail, run) in enumerate(zip(zs, fails, runs)):
        s = sums[g * tq:(g + 1) * tq]
        a = jnp.exp(z + log_fail + s[:, :SB_BLOCK] + run)
        if mask is not None:
            a = jnp.where(mask, a, 0.0)
        weights.append(a.astype(BF16))
        new_runs.append(run + s[:, SB_BLOCK:])
    return new_runs, [_dot(a, v) for a, v in zip(weights, vs)]


def _largest(values):
    out = values[0]
    for v in values[1:]:
        out = jnp.maximum(out, v)
    return jnp.max(out)


def _sb_prompt_body(q_ref, k_ref, v_ref, o_ref, run_ref, acc_ref, scan_ref, *, t, group):
    scan_ref[...] = _scan_matrix()
    below_diag = (lax.broadcasted_iota(jnp.int32, (SB_BLOCK, SB_BLOCK), 1) <
                  lax.broadcasted_iota(jnp.int32, (SB_BLOCK, SB_BLOCK), 0))
    zero = jnp.zeros((SB_BLOCK, HEAD_DIM), F32)
    chain = [slice(g * SB_BLOCK, (g + 1) * SB_BLOCK) for g in range(group)]

    def block(i):
        return pl.ds(pl.multiple_of(i * SB_BLOCK, SB_BLOCK), SB_BLOCK)

    def group_scan(s, _):
        first = s * group
        tiles = [block(first + g) for g in range(group)]
        runs, pvs = _sb_steps([q_ref[r, :] for r in tiles], [k_ref[r, :] for r in tiles],
                              [v_ref[r, :] for r in tiles], [zero] * group, below_diag, scan_ref[...])
        for sl, run, pv in zip(chain, runs, pvs):
            run_ref[sl, :] = run
            acc_ref[sl, :] = pv

        def split(runs):
            return (_largest([r[:SB_LATE_ROWS] for r in runs]), _largest([r[SB_LATE_ROWS:] for r in runs]))

        def visit(it, n):
            keys = [block(jnp.maximum(first + g - it, 0)) for g in range(group)]
            rows = [slice(g * SB_BLOCK, g * SB_BLOCK + n) for g in range(group)]
            runs = [jnp.where(first + g - it >= 0, run_ref[r, :], RUN_FINISHED) for g, r in enumerate(rows)]
            queries = [q_ref[pl.ds(pl.multiple_of((first + g) * SB_BLOCK, SB_BLOCK), n), :] for g in range(group)]
            runs, pvs = _sb_steps(queries, [k_ref[r, :] for r in keys], [v_ref[r, :] for r in keys], runs, None,
                                  scan_ref[...])
            for r, run, pv in zip(rows, runs, pvs):
                run_ref[r, :] = run
                acc_ref[r, :] += pv
            return runs

        def more(c):
            it, late, rest = c
            return jnp.logical_and(it < first + group, jnp.maximum(late, rest) > EXP_ZERO_BELOW)

        def step(c):
            it, _, rest = c
            late, rest = lax.cond(rest > EXP_ZERO_BELOW,
                                  lambda: split(visit(it, SB_BLOCK)),
                                  lambda: (_largest(visit(it, SB_LATE_ROWS)), rest))
            return it + 1, late, rest

        lax.while_loop(more, step, (jnp.int32(1),) + split(runs))
        out_rows = pl.ds(pl.multiple_of(first * SB_BLOCK, SB_BLOCK), group * SB_BLOCK)
        o_ref[out_rows, :] = acc_ref[...].astype(o_ref.dtype)
        return 0

    lax.fori_loop(0, t // (group * SB_BLOCK), group_scan, 0)


def _sb_attn_prompt(q, k, v):
    b, _, t, _ = q.shape
    group = min(SB_GROUP, t // SB_BLOCK)
    assert t % (group * SB_BLOCK) == 0, t
    spec = pl.BlockSpec((None, None, t, HEAD_DIM), lambda bi, h: (bi, h, 0, 0))
    return pl.pallas_call(
        functools.partial(_sb_prompt_body, t=t, group=group),
        grid=(b, SB_HEADS),
        in_specs=[spec, spec, spec],
        out_specs=spec,
        out_shape=jax.ShapeDtypeStruct((b, SB_HEADS, t, HEAD_DIM), BF16),
        scratch_shapes=[pltpu.VMEM((group * SB_BLOCK, HEAD_DIM), F32),
                        pltpu.VMEM((group * SB_BLOCK, HEAD_DIM), F32),
                        pltpu.VMEM((2 * SB_BLOCK, SB_BLOCK + HEAD_DIM), BF16)],
        compiler_params=_params("parallel", "parallel"),
        name="sb_attn_prompt",
    )(q, k, v)


def _sb_cached_body(q_ref, kn_ref, vn_ref, kr_ref, vr_ref, k_hbm, v_hbm, o_ref,
                    run_ref, acc_ref, scan_ref, kbuf_ref, vbuf_ref, sem, *, t, past, recent):
    scan_ref[...] = _scan_matrix()
    earlier = (lax.broadcasted_iota(jnp.int32, (t, SB_BLOCK), 1) <
               lax.broadcasted_iota(jnp.int32, (t, SB_BLOCK), 0))
    zero = jnp.zeros((t, HEAD_DIM), F32)
    pad = jnp.zeros((SB_BLOCK - t, HEAD_DIM), BF16)
    heads = range(SB_HEADS)
    chain = [slice(h * t, (h + 1) * t) for h in heads]

    runs, pvs = _sb_steps([q_ref[h] for h in heads],
                          [jnp.concatenate([kn_ref[h], pad], axis=0) for h in heads],
                          [jnp.concatenate([vn_ref[h], pad], axis=0) for h in heads],
                          [zero] * SB_HEADS, earlier, scan_ref[...])
    for sl, run, pv in zip(chain, runs, pvs):
        run_ref[sl, :] = run
        acc_ref[sl, :] = pv

    def visit(ks, vs):
        runs, pvs = _sb_steps([q_ref[h] for h in heads], ks, vs, [run_ref[sl, :] for sl in chain], None,
                              scan_ref[...])
        for sl, run, pv in zip(chain, runs, pvs):
            run_ref[sl, :] = run
            acc_ref[sl, :] += pv
        return _largest(runs)

    def more(blocks):
        return lambda c: jnp.logical_and(c[0] < blocks, c[1] > EXP_ZERO_BELOW)

    def recent_step(c):
        it, _ = c
        keys = pl.ds(pl.multiple_of(recent - (it + 1) * SB_BLOCK, SB_BLOCK), SB_BLOCK)
        return it + 1, visit([kr_ref[h, keys, :].astype(BF16) for h in heads],
                             [vr_ref[h, keys, :].astype(BF16) for h in heads])

    def older_step(c):
        it, _ = c
        bi = pl.program_id(0)
        keys = pl.ds(pl.multiple_of(past - (it + 1) * SB_BLOCK, SB_BLOCK), SB_BLOCK)
        copies = [pltpu.make_async_copy(src.at[bi, :, keys, :], dst, sem.at[n])
                  for n, (src, dst) in enumerate(((k_hbm, kbuf_ref), (v_hbm, vbuf_ref)))]
        for cp in copies:
            cp.start()
        for cp in copies:
            cp.wait()
        return it + 1, visit([kbuf_ref[h].astype(BF16) for h in heads], [vbuf_ref[h].astype(BF16) for h in heads])

    state = lax.while_loop(more(recent // SB_BLOCK), recent_step, (jnp.int32(0), _largest(runs)))
    lax.while_loop(more(past // SB_BLOCK), older_step, state)
    for h, sl in zip(heads, chain):
        o_ref[h] = acc_ref[sl, :].astype(o_ref.dtype)


def _sb_attn_cached(q, k_new, v_new, k_past, v_past):
    b, _, t, _ = q.shape
    past = k_past.shape[2]
    assert past % SB_BLOCK == 0 and t <= SB_BLOCK and t % 16 == 0, (past, t)
    recent = min(past, 2 * SB_BLOCK)
    assert past % recent == 0, past
    new = pl.BlockSpec((None, SB_HEADS, t, HEAD_DIM), lambda bi: (bi, 0, 0, 0))
    last = pl.BlockSpec((None, SB_HEADS, recent, HEAD_DIM), lambda bi: (bi, 0, past // recent - 1, 0))
    whole = pl.BlockSpec(memory_space=pl.ANY)
    return pl.pallas_call(
        functools.partial(_sb_cached_body, t=t, past=past, recent=recent),
        grid=(b,),
        in_specs=[new, new, new, last, last, whole, whole],
        out_specs=new,
        out_shape=jax.ShapeDtypeStruct((b, SB_HEADS, t, HEAD_DIM), BF16),
        scratch_shapes=[pltpu.VMEM((SB_HEADS * t, HEAD_DIM), F32),
                        pltpu.VMEM((SB_HEADS * t, HEAD_DIM), F32),
                        pltpu.VMEM((2 * SB_BLOCK, SB_BLOCK + HEAD_DIM), BF16),
                        pltpu.VMEM((SB_HEADS, SB_BLOCK, HEAD_DIM), F32),
                        pltpu.VMEM((SB_HEADS, SB_BLOCK, HEAD_DIM), F32),
                        pltpu.SemaphoreType.DMA((2,))],
        compiler_params=_params("parallel"),
        name="sb_attn_cached",
    )(q, k_new, v_new, k_past, v_past, k_past, v_past)


def _mix_out_body(x_ref, mix_ref, qm_ref, mk_ref, mv_ref, qg_ref, w_ref, o_ref):
    bb = mk_ref.shape[0]
    n_mem = mk_ref.shape[1] // MEM_HEADS
    tt = x_ref.shape[0] // bb
    probs = []
    for bi in range(bb):
        rows = slice(bi * tt, (bi + 1) * tt)
        for hd in range(MEM_HEADS):
            sl = slice(hd * HEAD_DIM, (hd + 1) * HEAD_DIM)
            qh = _rms(qm_ref[rows, sl], qg_ref[...]).astype(BF16)
            s = _dot_nt(qh, mk_ref[bi, _head_rows(hd, n_mem), :].astype(BF16)) * INV_SQRT_HD
            p = jnp.exp(s - jnp.max(s, axis=-1, keepdims=True))
            probs.append((p / jnp.sum(p, axis=-1, keepdims=True)).astype(BF16))
    if len(mix_ref.shape) == 4:
        mix = jnp.concatenate([jnp.concatenate([mix_ref[bi, hd] for hd in range(SB_HEADS)], axis=1)
                               for bi in range(mix_ref.shape[0])], axis=0)
    else:
        mix = mix_ref[...]
    y = x_ref[...] + _dot(mix, w_ref[:D_TOK, :])
    mo = jnp.concatenate(
        [jnp.concatenate([_dot(probs[bi * MEM_HEADS + hd], mv_ref[bi, _head_rows(hd, n_mem), :].astype(BF16))
                          for hd in range(MEM_HEADS)], axis=1) for bi in range(bb)], axis=0)
    o_ref[...] = y + _dot(mo.astype(BF16), w_ref[D_TOK:, :])


def _mix_out(x, mix, qm_src, qm_col, mem_k, mem_v, q_gain, w_out, layer, t):
    n, d = x.shape
    mem_rows = mem_k.shape[2]
    tm = min(n, ROW_TILE)
    row = lambda c, col=0: pl.BlockSpec((tm, c), lambda i: (i, col))
    if t >= tm:
        per_seq = t // tm
        mem = pl.BlockSpec((None, 1, mem_rows, HEAD_DIM), lambda i: (layer, i // per_seq, 0, 0))
        heads = pl.BlockSpec((1, SB_HEADS, tm, HEAD_DIM), lambda i: (i // per_seq, 0, i % per_seq, 0))
    else:
        mem = pl.BlockSpec((None, tm // t, mem_rows, HEAD_DIM), lambda i: (layer, i, 0, 0),
                           pipeline_mode=pl.Buffered(1))
        heads = pl.BlockSpec((tm // t, SB_HEADS, t, HEAD_DIM), lambda i: (i, 0, 0, 0))
    return pl.pallas_call(
        _mix_out_body,
        grid=(n // tm,),
        in_specs=[row(d), heads if mix.ndim == 4 else row(D_TOK), row(D_MEM, qm_col), mem, mem,
                  pl.BlockSpec((1, HEAD_DIM), lambda i: (0, 0)), _resident((d, d), layer)],
        out_specs=row(d),
        out_shape=jax.ShapeDtypeStruct((n, d), F32),
        compiler_params=_params("parallel"),
        name="mix_out",
    )(x, mix, qm_src, mem_k, mem_v, q_gain.reshape(1, HEAD_DIM), w_out)


def _ffn_body(x_ref, g_ref, wu_ref, wd_ref, o_ref, *rest):
    h_ref = rest[-1]

    @pl.when(pl.program_id(1) == 0)
    def _():
        x = x_ref[...]
        h_ref[...] = _rms(x, g_ref[...]).astype(BF16)
        o_ref[...] = x

    wu = wu_ref[...].astype(BF16)
    wd = wd_ref[...].astype(BF16)
    if len(rest) == 3:
        rest[0][...] = wu
        rest[1][...] = wd
    pieces = max(1, wu.shape[1] // FFN_PIECE)
    half = wu.shape[1] // pieces
    for c in range(pieces):
        cols = slice(c * half, (c + 1) * half)
        up = jnp.maximum(_dot(h_ref[...], wu[:, cols]), 0.0)
        o_ref[...] += _dot((up * up).astype(BF16), wd[cols, :])


def _ffn(x, g, w_up, w_down):
    n, d = x.shape
    f = w_up.shape[1]
    tm = min(n, ROW_TILE)
    tk = FFN_BLOCK
    return pl.pallas_call(
        _ffn_body,
        grid=(n // tm, f // tk),
        in_specs=[
            pl.BlockSpec((tm, d), lambda i, k: (i, 0)),
            pl.BlockSpec((1, d), lambda i, k: (0, 0)),
            pl.BlockSpec((d, tk), lambda i, k: (0, k)),
            pl.BlockSpec((tk, d), lambda i, k: (k, 0)),
        ],
        out_specs=pl.BlockSpec((tm, d), lambda i, k: (i, 0)),
        out_shape=jax.ShapeDtypeStruct((n, d), F32),
        scratch_shapes=[pltpu.VMEM((tm, d), BF16)],
        compiler_params=_params("parallel", "arbitrary", vmem=BIG_VMEM_LIMIT),
        name="ffn",
    )(x, g.reshape(1, d), w_up, w_down)


def _ffn_casting(x, g, w_up, w_down, layer):
    n, d = x.shape
    f = w_up.shape[2]
    tk = FFN_CAST_BLOCK
    assert n <= ROW_TILE, n
    return pl.pallas_call(
        _ffn_body,
        grid=(1, f // tk),
        in_specs=[
            pl.BlockSpec((n, d), lambda i, k: (0, 0)),
            pl.BlockSpec((1, d), lambda i, k: (0, 0)),
            pl.BlockSpec((None, d, tk), lambda i, k: (layer, 0, k)),
            pl.BlockSpec((None, tk, d), lambda i, k: (layer, k, 0)),
        ],
        out_specs=[pl.BlockSpec((n, d), lambda i, k: (0, 0)),
                   pl.BlockSpec((d, tk), lambda i, k: (0, k)),
                   pl.BlockSpec((tk, d), lambda i, k: (k, 0))],
        out_shape=[jax.ShapeDtypeStruct((n, d), F32), jax.ShapeDtypeStruct((d, f), BF16),
                   jax.ShapeDtypeStruct((f, d), BF16)],
        scratch_shapes=[pltpu.VMEM((n, d), BF16)],
        compiler_params=_params("arbitrary", "arbitrary"),
        name="ffn_casting",
    )(x, g.reshape(1, d), w_up, w_down)


def _run_trunk(x, past, pool_prev, sb_past_k, sb_past_v, mem_k, mem_v, p, bf16):
    b, t, d = x.shape
    n = b * t
    depth = p["w_out"].shape[0]
    pool_new, sbk_new, sbv_new, made = [], [], [], dict(ffn=[], in_sb=[])
    x2 = x.reshape(n, d)
    for i in range(depth):
        j = i // 2
        if i % 2 == 0:
            proj = _norm_matmul(x2, p["norm_mix"][i], p["w_in_pool"], j)
            proj3 = proj.reshape(b, t, d)
            mix = _pool_mix(proj3, pool_prev[j], p["pool_w"][j], p["pool_scale"][j], past).reshape(n, D_TOK)
            pool_new.append(proj3[:, t - POOL_STATE:, :D_TOK])
            qm_src, qm_col = proj, D_TOK // D_MEM
        else:
            if bf16 is None:
                q, k, v, kb, vb, qm, w4 = _inproj_sb(x2, p["norm_mix"][i], p["w_in_sb"], b, t, layer=j)
                made["in_sb"].append(w4)
            else:
                q, k, v, kb, vb, qm = _inproj_sb(x2, p["norm_mix"][i], bf16["in_sb"][j], b, t)
            if sb_past_k is None:
                mix = _sb_attn_prompt(q, kb, vb)
            else:
                mix = _sb_attn_cached(q, kb, vb, sb_past_k[j], sb_past_v[j])
            sbk_new.append(k)
            sbv_new.append(v)
            qm_src, qm_col = qm, 0
        x2 = _mix_out(x2, mix, qm_src, qm_col, mem_k, mem_v, p["q_norm"][i], p["w_out"], i, t)
        if bf16 is None:
            x2, wu, wd = _ffn_casting(x2, p["norm_ffn"][i], p["w_up"], p["w_down"], i)
            made["ffn"].append((wu, wd))
        else:
            x2 = _ffn(x2, p["norm_ffn"][i], *bf16["ffn"][i])
    return x2.reshape(b, t, d), jnp.stack(pool_new), jnp.stack(sbk_new), jnp.stack(sbv_new), made


def kernel(x_prompt, x_sample, mem_prompt, cache_pool, cache_sb_k, cache_sb_v, cache_mem_k, cache_mem_v,
           norm_mix, w_in_pool, w_in_sb, pool_w, pool_scale, norm_mem, w_mem_kv, q_norm, k_norm, w_out,
           norm_ffn, w_up, w_down):
    b, t, d = x_prompt.shape
    depth = w_out.shape[0]
    n_mem = mem_prompt.shape[1]
    p = dict(norm_mix=norm_mix, pool_scale=pool_scale, q_norm=q_norm, norm_ffn=norm_ffn,
             w_in_pool=w_in_pool.astype(BF16), w_in_sb=w_in_sb, pool_w=pool_w.astype(BF16),
             w_out=w_out.astype(BF16), w_up=w_up, w_down=w_down)

    bs = x_sample.shape[0]
    past = cache_sb_k.shape[2]
    pool_prev = jnp.pad(cache_pool, ((0, 0), (0, 0), (POOL_HALO - POOL_STATE, 0), (0, 0)))
    head_major = lambda a: jnp.transpose(a, (0, 1, 3, 2, 4))
    mem_rows = lambda a, nb: a.reshape(depth, nb, n_mem * MEM_HEADS, HEAD_DIM)
    y_sample, pool_sample, sb_k_sample, sb_v_sample, bf16 = _run_trunk(
        x_sample, past, pool_prev, head_major(cache_sb_k), head_major(cache_sb_v),
        mem_rows(cache_mem_k, bs), mem_rows(cache_mem_v, bs), p, None)

    mk, mv = _mem_kv(mem_prompt.reshape(b * n_mem, d), norm_mem, w_mem_kv, k_norm)
    mk = mem_rows(mk, b)
    mv = mem_rows(mv, b)
    pool_zero = jnp.zeros((cache_pool.shape[0], b, POOL_HALO, D_TOK), F32)
    y_prompt, pool_prompt, sb_k_prompt, sb_v_prompt, _ = _run_trunk(
        x_prompt, 0, pool_zero, None, None, mk, mv, p, bf16)

    heads = lambda a: a.reshape(depth, b, n_mem, MEM_HEADS, HEAD_DIM)
    return (y_prompt, y_sample, pool_prompt, pool_sample, head_major(sb_k_prompt), head_major(sb_v_prompt),
            head_major(sb_k_sample), head_major(sb_v_sample), heads(mk), heads(mv))
```

```python
import functools
import math

import jax
import jax.numpy as jnp
from jax import lax
from jax.experimental import pallas as pl
from jax.experimental.pallas import tpu as pltpu

D_MODEL = 2048
HEAD_DIM = 128
D_TOK = 3 * D_MODEL // 4
D_MEM = D_MODEL // 4
SB_HEADS = D_TOK // HEAD_DIM
MEM_HEADS = D_MEM // HEAD_DIM
POOL_WINDOWS = (2, 4, 8, 16)
POOL_GROUP = D_TOK // len(POOL_WINDOWS)
POOL_STATE = max(POOL_WINDOWS) - 1
POOL_HALO = POOL_STATE + 1
D_FF = 4 * D_MODEL
EPS = 1e-6
INV_SQRT_HD = 1.0 / math.sqrt(HEAD_DIM)

SB_BLOCK = 128
SB_LATE_ROWS = 64
SB_GROUP = 32
EXP_ZERO_BELOW = -104.0
RUN_FINISHED = -1e30
ROW_TILE = 512
FFN_BLOCK = 2048
FFN_PIECE = 1024
FFN_CAST_BLOCK = 512
VMEM_LIMIT = 48 * 1024 * 1024
BIG_VMEM_LIMIT = 60 * 1024 * 1024

BF16 = jnp.bfloat16
F32 = jnp.float32


def _params(*sem, vmem=VMEM_LIMIT):
    return pltpu.CompilerParams(dimension_semantics=sem, vmem_limit_bytes=vmem)


def _rms(xf, g):
    return xf * lax.rsqrt(jnp.mean(xf * xf, axis=-1, keepdims=True) + EPS) * g


def _dot(a, b):
    return jnp.dot(a, b, preferred_element_type=F32)


def _dot_nt(a, b):
    return lax.dot_general(a, b, (((1,), (1,)), ((), ())), preferred_element_type=F32)


def _resident(shape, layer):
    return pl.BlockSpec((None,) + shape, lambda *_: (layer,) + (0,) * len(shape), pipeline_mode=pl.Buffered(1))


def _norm_matmul_body(x_ref, g_ref, w_ref, o_ref):
    o_ref[...] = _dot(_rms(x_ref[...], g_ref[...]).astype(BF16), w_ref[...])


def _norm_matmul(x, g, w, layer):
    n, d = x.shape
    m = w.shape[2]
    tm = min(n, 2 * ROW_TILE)
    return pl.pallas_call(
        _norm_matmul_body,
        grid=(n // tm,),
        in_specs=[
            pl.BlockSpec((tm, d), lambda i: (i, 0)),
            pl.BlockSpec((1, d), lambda i: (0, 0)),
            _resident((d, m), layer),
        ],
        out_specs=pl.BlockSpec((tm, m), lambda i: (i, 0)),
        out_shape=jax.ShapeDtypeStruct((n, m), F32),
        compiler_params=_params("parallel", vmem=BIG_VMEM_LIMIT),
        name="norm_matmul",
    )(x, g.reshape(1, d), w)


def _inproj_sb_body(x_ref, g_ref, wq_ref, wk_ref, wv_ref, wm_ref,
                    q_ref, k_ref, v_ref, kb_ref, vb_ref, qm_ref, *rest):
    h_ref = rest[-1]

    def weight(w_ref, n):
        w = w_ref[...].astype(BF16)
        if len(rest) == 5:
            rest[n][...] = w
        return w

    @pl.when(pl.program_id(1) == 0)
    def _():
        h_ref[...] = _rms(x_ref[...], g_ref[...]).astype(BF16)
        qm_ref[...] = _dot(h_ref[...], weight(wm_ref, 3))

    def put(refs, val):
        bb, heads, tt, _ = refs[0].shape
        for bi in range(bb):
            for hd in range(heads):
                piece = val[bi * tt:(bi + 1) * tt, hd * HEAD_DIM:(hd + 1) * HEAD_DIM]
                for ref in refs:
                    ref[bi, hd] = piece.astype(ref.dtype)

    h = h_ref[...]
    put([q_ref], _dot(h, weight(wq_ref, 0)) * INV_SQRT_HD)
    put([k_ref, kb_ref], _dot(h, weight(wk_ref, 1)))
    put([v_ref, vb_ref], _dot(h, weight(wv_ref, 2)))


def _inproj_sb(x, g, w, b, t, layer=None):
    n, d = x.shape
    casting = layer is not None
    tm = min(n, 2 * ROW_TILE)
    tn = 2 * HEAD_DIM if casting else D_MEM
    nj = D_TOK // tn
    hj = tn // HEAD_DIM
    if t >= tm:
        per_seq = t // tm
        oblock, omap = (1, hj, tm, HEAD_DIM), lambda i, j: (i // per_seq, j, i % per_seq, 0)
    else:
        oblock, omap = (tm // t, hj, t, HEAD_DIM), lambda i, j: (i, j, 0, 0)
    tok = lambda dt: jax.ShapeDtypeStruct((b, SB_HEADS, t, HEAD_DIM), dt)
    ospec = pl.BlockSpec(oblock, omap)
    col_block = pl.BlockSpec((d, tn), lambda i, j: (0, j))
    whole_wm = lambda **kw: pl.BlockSpec((d, D_MEM), lambda i, j: (0, 0), **kw)
    out_specs = [ospec, ospec, ospec, ospec, ospec, pl.BlockSpec((tm, D_MEM), lambda i, j: (i, 0))]
    out_shape = [tok(BF16), tok(F32), tok(F32), tok(BF16), tok(BF16), jax.ShapeDtypeStruct((n, D_MEM), F32)]
    if casting:
        assert n == tm, (n, tm)
        wspec = lambda off: pl.BlockSpec((None, d, tn), lambda i, j: (layer, 0, off + j))
        w_specs = [wspec(0), wspec(nj), wspec(2 * nj),
                   pl.BlockSpec((None, d, D_MEM), lambda i, j: (layer, 0, 3 * D_TOK // D_MEM),
                                pipeline_mode=pl.Buffered(1))]
        w_args = (w, w, w, w)
        out_specs += [col_block, col_block, col_block, whole_wm()]
        out_shape += [jax.ShapeDtypeStruct((d, D_TOK), BF16)] * 3 + [jax.ShapeDtypeStruct((d, D_MEM), BF16)]
    else:
        w_specs = [col_block, col_block, col_block, whole_wm(pipeline_mode=pl.Buffered(1))]
        w_args = tuple(w)
    out = pl.pallas_call(
        _inproj_sb_body,
        grid=(n // tm, nj),
        in_specs=[pl.BlockSpec((tm, d), lambda i, j: (i, 0)), pl.BlockSpec((1, d), lambda i, j: (0, 0))] + w_specs,
        out_specs=out_specs,
        out_shape=out_shape,
        scratch_shapes=[pltpu.VMEM((tm, d), BF16)],
        compiler_params=_params("arbitrary" if casting else "parallel", "arbitrary", vmem=BIG_VMEM_LIMIT),
        name="inproj_sb",
    )(x, g.reshape(1, d), *w_args)
    return (*out[:6], tuple(out[6:])) if casting else out


def _head_rows(hd, n):
    return pl.ds(hd, n, stride=MEM_HEADS)


def _mem_kv_body(x_ref, g_ref, w_ref, kg_ref, k_ref, v_ref):
    tm = x_ref.shape[0]
    h = _rms(x_ref[...], g_ref[...]).astype(BF16)
    kv = _dot(h, w_ref[...].astype(BF16))
    for hd in range(MEM_HEADS):
        sl = slice(hd * HEAD_DIM, (hd + 1) * HEAD_DIM)
        k_ref[_head_rows(hd, tm), :] = _rms(kv[:, sl], kg_ref[...])
        v_ref[_head_rows(hd, tm), :] = kv[:, D_MEM + hd * HEAD_DIM:D_MEM + (hd + 1) * HEAD_DIM]


def _mem_kv(mem, g_mem, w_kv, k_gain):
    r, d = mem.shape
    depth = w_kv.shape[0]
    tm = min(r, 2 * ROW_TILE)
    out = jax.ShapeDtypeStruct((depth, r * MEM_HEADS, HEAD_DIM), F32)
    ospec = pl.BlockSpec((None, tm * MEM_HEADS, HEAD_DIM), lambda l, i: (l, i, 0))
    return pl.pallas_call(
        _mem_kv_body,
        grid=(depth, r // tm),
        in_specs=[
            pl.BlockSpec((tm, d), lambda l, i: (i, 0)),
            pl.BlockSpec((None, 1, d), lambda l, i: (l, 0, 0)),
            pl.BlockSpec((None, d, 2 * D_MEM), lambda l, i: (l, 0, 0)),
            pl.BlockSpec((None, 1, HEAD_DIM), lambda l, i: (l, 0, 0)),
        ],
        out_specs=[ospec, ospec],
        out_shape=[out, out],
        compiler_params=_params("parallel", "parallel"),
        name="mem_kv",
    )(mem, g_mem.reshape(depth, 1, d), w_kv, k_gain.reshape(depth, 1, HEAD_DIM))


def _pool_body(u_ref, halo_ref, prev_ref, w_ref, s_ref, o_ref, ext_ref, a_ref, b_ref, *, tm, past):
    i = pl.program_id(1)
    base = 2 * POOL_HALO
    rows = tm + POOL_HALO
    ext_ref[0:POOL_HALO, :] = jnp.zeros((POOL_HALO, D_TOK), F32)
    a_ref[0:POOL_HALO, :] = jnp.zeros((POOL_HALO, POOL_GROUP), F32)
    b_ref[0:POOL_HALO, :] = jnp.zeros((POOL_HALO, POOL_GROUP), F32)

    @pl.when(i == 0)
    def _():
        ext_ref[POOL_HALO:base, :] = prev_ref[...]

    @pl.when(i > 0)
    def _():
        ext_ref[POOL_HALO:base, :] = halo_ref[...]

    ext_ref[base:, :] = u_ref[...]
    pos = (past + i * tm + lax.broadcasted_iota(jnp.int32, (tm, 1), 0)).astype(F32)
    for g, w in enumerate(POOL_WINDOWS):
        sl = slice(g * POOL_GROUP, (g + 1) * POOL_GROUP)
        src, cols, nxt = ext_ref, sl, (a_ref, b_ref)
        span = 1
        while span < w:
            part = src[POOL_HALO:POOL_HALO + rows, cols] + src[POOL_HALO - span:POOL_HALO - span + rows, cols]
            span *= 2
            if span < w:
                nxt[0][POOL_HALO:POOL_HALO + rows, :] = part
                src, cols, nxt = nxt[0], slice(None), nxt[::-1]
        win = part[POOL_HALO:, :]
        cur = ext_ref[base:, sl]
        cnt = jnp.minimum(pos + 1.0, float(w))
        dlt = (win / cnt - cur).astype(BF16)
        o_ref[:, sl] = (_dot(dlt, w_ref[g]) * s_ref[:, sl]).astype(BF16)


def _pool_mix(proj, prev, w_grp, scale, past):
    b, t, _ = proj.shape
    tm = min(t, 2 * ROW_TILE)
    hb = tm // POOL_HALO
    return pl.pallas_call(
        functools.partial(_pool_body, tm=tm, past=past),
        grid=(b, t // tm),
        in_specs=[
            pl.BlockSpec((None, tm, D_TOK), lambda bi, i: (bi, i, 0)),
            pl.BlockSpec((None, POOL_HALO, D_TOK), lambda bi, i: (bi, jnp.maximum(i * hb - 1, 0), 0)),
            pl.BlockSpec((None, POOL_HALO, D_TOK), lambda bi, i: (bi, 0, 0)),
            pl.BlockSpec((len(POOL_WINDOWS), POOL_GROUP, POOL_GROUP), lambda bi, i: (0, 0, 0)),
            pl.BlockSpec((1, D_TOK), lambda bi, i: (0, 0)),
        ],
        out_specs=pl.BlockSpec((None, tm, D_TOK), lambda bi, i: (bi, i, 0)),
        out_shape=jax.ShapeDtypeStruct((b, t, D_TOK), BF16),
        scratch_shapes=[pltpu.VMEM((tm + 2 * POOL_HALO, D_TOK), F32),
                        pltpu.VMEM((tm + 2 * POOL_HALO, POOL_GROUP), F32),
                        pltpu.VMEM((tm + 2 * POOL_HALO, POOL_GROUP), F32)],
        compiler_params=_params("parallel", "arbitrary"),
        name="pool_mix",
    )(proj, proj, prev, w_grp, scale.reshape(1, D_TOK))


def _scan_matrix():
    shape = (2 * SB_BLOCK, SB_BLOCK + HEAD_DIM)
    r = lax.broadcasted_iota(jnp.int32, shape, 0) % SB_BLOCK
    c = lax.broadcasted_iota(jnp.int32, shape, 1)
    return jnp.where((c >= SB_BLOCK) | (r > c), 1.0, 0.0).astype(BF16)


def _sb_steps(qs, ks, vs, runs, mask, scan):
    tq = qs[0].shape[0]
    zs = [_dot_nt(q, k) for q, k in zip(qs, ks)]
    fails, parts = [], []
    for z in zs:
        log_fail = -(jnp.maximum(z, 0.0) + jnp.log(1.0 + jnp.exp(-jnp.abs(z))))
        if mask is not None:
            log_fail = jnp.where(mask, log_fail, 0.0)
        hi = log_fail.astype(BF16)
        lo = (log_fail - hi.astype(F32)).astype(BF16)
        fails.append(log_fail)
        parts.append(jnp.concatenate([hi, lo], axis=1))
    sums = _dot(jnp.concatenate(parts, axis=0), scan)
    weights, new_runs = [], []
    for g, (z, log_fail, run) in enumerate(zip(zs, fails, runs)):
        s = sums[g * tq:(g + 1) * tq]
        a = jnp.exp(z + log_fail + s[:, :SB_BLOCK] + run)
        if mask is not None:
            a = jnp.where(mask, a, 0.0)
        weights.append(a.astype(BF16))
        new_runs.append(run + s[:, SB_BLOCK:])
    return new_runs, [_dot(a, v) for a, v in zip(weights, vs)]


def _largest(values):
    out = values[0]
    for v in values[1:]:
        out = jnp.maximum(out, v)
    return jnp.max(out)


def _sb_prompt_body(q_ref, k_ref, v_ref, o_ref, run_ref, acc_ref, scan_ref, *, t, group):
    scan_ref[...] = _scan_matrix()
    below_diag = (lax.broadcasted_iota(jnp.int32, (SB_BLOCK, SB_BLOCK), 1) <
                  lax.broadcasted_iota(jnp.int32, (SB_BLOCK, SB_BLOCK), 0))
    zero = jnp.zeros((SB_BLOCK, HEAD_DIM), F32)
    chain = [slice(g * SB_BLOCK, (g + 1) * SB_BLOCK) for g in range(group)]

    def block(i):
        return pl.ds(pl.multiple_of(i * SB_BLOCK, SB_BLOCK), SB_BLOCK)

    def group_scan(s, _):
        first = s * group
        tiles = [block(first + g) for g in range(group)]
        runs, pvs = _sb_steps([q_ref[r, :] for r in tiles], [k_ref[r, :] for r in tiles],
                              [v_ref[r, :] for r in tiles], [zero] * group, below_diag, scan_ref[...])
        for sl, run, pv in zip(chain, runs, pvs):
            run_ref[sl, :] = run
            acc_ref[sl, :] = pv

        def split(runs):
            return (_largest([r[:SB_LATE_ROWS] for r in runs]), _largest([r[SB_LATE_ROWS:] for r in runs]))

        def visit(it, n):
            keys = [block(jnp.maximum(first + g - it, 0)) for g in range(group)]
            rows = [slice(g * SB_BLOCK, g * SB_BLOCK + n) for g in range(group)]
            runs = [jnp.where(first + g - it >= 0, run_ref[r, :], RUN_FINISHED) for g, r in enumerate(rows)]
            queries = [q_ref[pl.ds(pl.multiple_of((first + g) * SB_BLOCK, SB_BLOCK), n), :] for g in range(group)]
            runs, pvs = _sb_steps(queries, [k_ref[r, :] for r in keys], [v_ref[r, :] for r in keys], runs, None,
                                  scan_ref[...])
            for r, run, pv in zip(rows, runs, pvs):
                run_ref[r, :] = run
                acc_ref[r, :] += pv
            return runs

        def more(c):
            it, late, rest = c
            return jnp.logical_and(it < first + group, jnp.maximum(late, rest) > EXP_ZERO_BELOW)

        def step(c):
            it, _, rest = c
            late, rest = lax.cond(rest > EXP_ZERO_BELOW,
                                  lambda: split(visit(it, SB_BLOCK)),
                                  lambda: (_largest(visit(it, SB_LATE_ROWS)), rest))
            return it + 1, late, rest

        lax.while_loop(more, step, (jnp.int32(1),) + split(runs))
        out_rows = pl.ds(pl.multiple_of(first * SB_BLOCK, SB_BLOCK), group * SB_BLOCK)
        o_ref[out_rows, :] = acc_ref[...].astype(o_ref.dtype)
        return 0

    lax.fori_loop(0, t // (group * SB_BLOCK), group_scan, 0)


def _sb_attn_prompt(q, k, v):
    b, _, t, _ = q.shape
    group = min(SB_GROUP, t // SB_BLOCK)
    assert t % (group * SB_BLOCK) == 0, t
    spec = pl.BlockSpec((None, None, t, HEAD_DIM), lambda bi, h: (bi, h, 0, 0))
    return pl.pallas_call(
        functools.partial(_sb_prompt_body, t=t, group=group),
        grid=(b, SB_HEADS),
        in_specs=[spec, spec, spec],
        out_specs=spec,
        out_shape=jax.ShapeDtypeStruct((b, SB_HEADS, t, HEAD_DIM), BF16),
        scratch_shapes=[pltpu.VMEM((group * SB_BLOCK, HEAD_DIM), F32),
                        pltpu.VMEM((group * SB_BLOCK, HEAD_DIM), F32),
                        pltpu.VMEM((2 * SB_BLOCK, SB_BLOCK + HEAD_DIM), BF16)],
        compiler_params=_params("parallel", "parallel"),
        name="sb_attn_prompt",
    )(q, k, v)


def _sb_cached_body(q_ref, kn_ref, vn_ref, kr_ref, vr_ref, k_hbm, v_hbm, o_ref,
                    run_ref, acc_ref, scan_ref, kbuf_ref, vbuf_ref, sem, *, t, past, recent):
    scan_ref[...] = _scan_matrix()
    earlier = (lax.broadcasted_iota(jnp.int32, (t, SB_BLOCK), 1) <
               lax.broadcasted_iota(jnp.int32, (t, SB_BLOCK), 0))
    zero = jnp.zeros((t, HEAD_DIM), F32)
    pad = jnp.zeros((SB_BLOCK - t, HEAD_DIM), BF16)
    heads = range(SB_HEADS)
    chain = [slice(h * t, (h + 1) * t) for h in heads]

    runs, pvs = _sb_steps([q_ref[h] for h in heads],
                          [jnp.concatenate([kn_ref[h], pad], axis=0) for h in heads],
                          [jnp.concatenate([vn_ref[h], pad], axis=0) for h in heads],
                          [zero] * SB_HEADS, earlier, scan_ref[...])
    for sl, run, pv in zip(chain, runs, pvs):
        run_ref[sl, :] = run
        acc_ref[sl, :] = pv

    def visit(ks, vs):
        runs, pvs = _sb_steps([q_ref[h] for h in heads], ks, vs, [run_ref[sl, :] for sl in chain], None,
                              scan_ref[...])
        for sl, run, pv in zip(chain, runs, pvs):
            run_ref[sl, :] = run
            acc_ref[sl, :] += pv
        return _largest(runs)

    def more(blocks):
        return lambda c: jnp.logical_and(c[0] < blocks, c[1] > EXP_ZERO_BELOW)

    def recent_step(c):
        it, _ = c
        keys = pl.ds(pl.multiple_of(recent - (it + 1) * SB_BLOCK, SB_BLOCK), SB_BLOCK)
        return it + 1, visit([kr_ref[h, keys, :].astype(BF16) for h in heads],
                             [vr_ref[h, keys, :].astype(BF16) for h in heads])

    def older_step(c):
        it, _ = c
        bi = pl.program_id(0)
        keys = pl.ds(pl.multiple_of(past - (it + 1) * SB_BLOCK, SB_BLOCK), SB_BLOCK)
        copies = [pltpu.make_async_copy(src.at[bi, :, keys, :], dst, sem.at[n])
                  for n, (src, dst) in enumerate(((k_hbm, kbuf_ref), (v_hbm, vbuf_ref)))]
        for cp in copies:
            cp.start()
        for cp in copies:
            cp.wait()
        return it + 1, visit([kbuf_ref[h].astype(BF16) for h in heads], [vbuf_ref[h].astype(BF16) for h in heads])

    state = lax.while_loop(more(recent // SB_BLOCK), recent_step, (jnp.int32(0), _largest(runs)))
    lax.while_loop(more(past // SB_BLOCK), older_step, state)
    for h, sl in zip(heads, chain):
        o_ref[h] = acc_ref[sl, :].astype(o_ref.dtype)


def _sb_attn_cached(q, k_new, v_new, k_past, v_past):
    b, _, t, _ = q.shape
    past = k_past.shape[2]
    assert past % SB_BLOCK == 0 and t <= SB_BLOCK and t % 16 == 0, (past, t)
    recent = min(past, 2 * SB_BLOCK)
    assert past % recent == 0, past
    new = pl.BlockSpec((None, SB_HEADS, t, HEAD_DIM), lambda bi: (bi, 0, 0, 0))
    last = pl.BlockSpec((None, SB_HEADS, recent, HEAD_DIM), lambda bi: (bi, 0, past // recent - 1, 0))
    whole = pl.BlockSpec(memory_space=pl.ANY)
    return pl.pallas_call(
        functools.partial(_sb_cached_body, t=t, past=past, recent=recent),
        grid=(b,),
        in_specs=[new, new, new, last, last, whole, whole],
        out_specs=new,
        out_shape=jax.ShapeDtypeStruct((b, SB_HEADS, t, HEAD_DIM), BF16),
        scratch_shapes=[pltpu.VMEM((SB_HEADS * t, HEAD_DIM), F32),
                        pltpu.VMEM((SB_HEADS * t, HEAD_DIM), F32),
                        pltpu.VMEM((2 * SB_BLOCK, SB_BLOCK + HEAD_DIM), BF16),
                        pltpu.VMEM((SB_HEADS, SB_BLOCK, HEAD_DIM), F32),
                        pltpu.VMEM((SB_HEADS, SB_BLOCK, HEAD_DIM), F32),
                        pltpu.SemaphoreType.DMA((2,))],
        compiler_params=_params("parallel"),
        name="sb_attn_cached",
    )(q, k_new, v_new, k_past, v_past, k_past, v_past)


def _mix_out_body(x_ref, mix_ref, qm_ref, mk_ref, mv_ref, qg_ref, wf_ref, o_ref, w_ref):
    @pl.when(pl.program_id(0) == 0)
    def _():
        w_ref[...] = wf_ref[...].astype(BF16)

    _mix_out_tile(x_ref, mix_ref, qm_ref, mk_ref, mv_ref, qg_ref, w_ref, o_ref)


def _mix_out_tile(x_ref, mix_ref, qm_ref, mk_ref, mv_ref, qg_ref, w_ref, o_ref):
    bb = mk_ref.shape[0]
    n_mem = mk_ref.shape[1] // MEM_HEADS
    tt = x_ref.shape[0] // bb
    probs = []
    for bi in range(bb):
        rows = slice(bi * tt, (bi + 1) * tt)
        for hd in range(MEM_HEADS):
            sl = slice(hd * HEAD_DIM, (hd + 1) * HEAD_DIM)
            qh = _rms(qm_ref[rows, sl], qg_ref[...]).astype(BF16)
            s = _dot_nt(qh, mk_ref[bi, _head_rows(hd, n_mem), :].astype(BF16)) * INV_SQRT_HD
            p = jnp.exp(s - jnp.max(s, axis=-1, keepdims=True))
            probs.append((p / jnp.sum(p, axis=-1, keepdims=True)).astype(BF16))
    if len(mix_ref.shape) == 4:
        mix = jnp.concatenate([jnp.concatenate([mix_ref[bi, hd] for hd in range(SB_HEADS)], axis=1)
                               for bi in range(mix_ref.shape[0])], axis=0)
    else:
        mix = mix_ref[...]
    y = x_ref[...] + _dot(mix, w_ref[:D_TOK, :])
    mo = jnp.concatenate(
        [jnp.concatenate([_dot(probs[bi * MEM_HEADS + hd], mv_ref[bi, _head_rows(hd, n_mem), :].astype(BF16))
                          for hd in range(MEM_HEADS)], axis=1) for bi in range(bb)], axis=0)
    o_ref[...] = y + _dot(mo.astype(BF16), w_ref[D_TOK:, :])


def _mix_out(x, mix, qm_src, qm_col, mem_k, mem_v, q_gain, w_out, layer, t):
    n, d = x.shape
    mem_rows = mem_k.shape[2]
    tm = min(n, ROW_TILE)
    if t < tm:
        tm = max(t, tm // 2)
    row = lambda c, col=0: pl.BlockSpec((tm, c), lambda i: (i, col))
    if t >= tm:
        per_seq = t // tm
        mem = pl.BlockSpec((None, 1, mem_rows, HEAD_DIM), lambda i: (layer, i // per_seq, 0, 0))
        heads = pl.BlockSpec((1, SB_HEADS, tm, HEAD_DIM), lambda i: (i // per_seq, 0, i % per_seq, 0))
    else:
        mem = pl.BlockSpec((None, tm // t, mem_rows, HEAD_DIM), lambda i: (layer, i, 0, 0))
        heads = pl.BlockSpec((tm // t, SB_HEADS, t, HEAD_DIM), lambda i: (i, 0, 0, 0))
    return pl.pallas_call(
        _mix_out_body,
        grid=(n // tm,),
        in_specs=[row(d), heads if mix.ndim == 4 else row(D_TOK), row(D_MEM, qm_col), mem, mem,
                  pl.BlockSpec((1, HEAD_DIM), lambda i: (0, 0)), _resident((d, d), layer)],
        out_specs=row(d),
        out_shape=jax.ShapeDtypeStruct((n, d), F32),
        scratch_shapes=[pltpu.VMEM((d, d), BF16)],
        compiler_params=_params("arbitrary", vmem=BIG_VMEM_LIMIT),
        name="mix_out",
    )(x, mix, qm_src, mem_k, mem_v, q_gain.reshape(1, HEAD_DIM), w_out)


def _ffn_body(x_ref, g_ref, wu_ref, wd_ref, o_ref, *rest):
    h_ref = rest[-1]

    @pl.when(pl.program_id(1) == 0)
    def _():
        x = x_ref[...]
        h_ref[...] = _rms(x, g_ref[...]).astype(BF16)
        o_ref[...] = x

    wu = wu_ref[...].astype(BF16)
    wd = wd_ref[...].astype(BF16)
    if len(rest) == 3:
        rest[0][...] = wu
        rest[1][...] = wd
    pieces = max(1, wu.shape[1] // FFN_PIECE)
    half = wu.shape[1] // pieces
    for c in range(pieces):
        cols = slice(c * half, (c + 1) * half)
        up = jnp.maximum(_dot(h_ref[...], wu[:, cols]), 0.0)
        o_ref[...] += _dot((up * up).astype(BF16), wd[cols, :])


def _ffn(x, g, w_up, w_down):
    n, d = x.shape
    f = w_up.shape[1]
    tm = min(n, ROW_TILE)
    tk = FFN_BLOCK
    return pl.pallas_call(
        _ffn_body,
        grid=(n // tm, f // tk),
        in_specs=[
            pl.BlockSpec((tm, d), lambda i, k: (i, 0)),
            pl.BlockSpec((1, d), lambda i, k: (0, 0)),
            pl.BlockSpec((d, tk), lambda i, k: (0, k)),
            pl.BlockSpec((tk, d), lambda i, k: (k, 0)),
        ],
        out_specs=pl.BlockSpec((tm, d), lambda i, k: (i, 0)),
        out_shape=jax.ShapeDtypeStruct((n, d), F32),
        scratch_shapes=[pltpu.VMEM((tm, d), BF16)],
        compiler_params=_params("parallel", "arbitrary", vmem=BIG_VMEM_LIMIT),
        name="ffn",
    )(x, g.reshape(1, d), w_up, w_down)


def _ffn_casting(x, g, w_up, w_down, layer):
    n, d = x.shape
    f = w_up.shape[2]
    tk = FFN_CAST_BLOCK
    assert n <= ROW_TILE, n
    return pl.pallas_call(
        _ffn_body,
        grid=(1, f // tk),
        in_specs=[
            pl.BlockSpec((n, d), lambda i, k: (0, 0)),
            pl.BlockSpec((1, d), lambda i, k: (0, 0)),
            pl.BlockSpec((None, d, tk), lambda i, k: (layer, 0, k)),
            pl.BlockSpec((None, tk, d), lambda i, k: (layer, k, 0)),
        ],
        out_specs=[pl.BlockSpec((n, d), lambda i, k: (0, 0)),
                   pl.BlockSpec((d, tk), lambda i, k: (0, k)),
                   pl.BlockSpec((tk, d), lambda i, k: (k, 0))],
        out_shape=[jax.ShapeDtypeStruct((n, d), F32), jax.ShapeDtypeStruct((d, f), BF16),
                   jax.ShapeDtypeStruct((f, d), BF16)],
        scratch_shapes=[pltpu.VMEM((n, d), BF16)],
        compiler_params=_params("arbitrary", "arbitrary"),
        name="ffn_casting",
    )(x, g.reshape(1, d), w_up, w_down)


def _run_trunk(x, past, pool_prev, sb_past_k, sb_past_v, mem_k, mem_v, p, bf16):
    b, t, d = x.shape
    n = b * t
    depth = p["w_out"].shape[0]
    pool_new, sbk_new, sbv_new, made = [], [], [], dict(ffn=[], in_sb=[])
    x2 = x.reshape(n, d)
    for i in range(depth):
        j = i // 2
        if i % 2 == 0:
            proj = _norm_matmul(x2, p["norm_mix"][i], p["w_in_pool"], j)
            proj3 = proj.reshape(b, t, d)
            mix = _pool_mix(proj3, pool_prev[j], p["pool_w"][j], p["pool_scale"][j], past).reshape(n, D_TOK)
            pool_new.append(proj3[:, t - POOL_STATE:, :D_TOK])
            qm_src, qm_col = proj, D_TOK // D_MEM
        else:
            if bf16 is None:
                q, k, v, kb, vb, qm, w4 = _inproj_sb(x2, p["norm_mix"][i], p["w_in_sb"], b, t, layer=j)
                made["in_sb"].append(w4)
            else:
                q, k, v, kb, vb, qm = _inproj_sb(x2, p["norm_mix"][i], bf16["in_sb"][j], b, t)
            if sb_past_k is None:
                mix = _sb_attn_prompt(q, kb, vb)
            else:
                mix = _sb_attn_cached(q, kb, vb, sb_past_k[j], sb_past_v[j])
            sbk_new.append(k)
            sbv_new.append(v)
            qm_src, qm_col = qm, 0
        x2 = _mix_out(x2, mix, qm_src, qm_col, mem_k, mem_v, p["q_norm"][i], p["w_out"], i, t)
        if bf16 is None:
            x2, wu, wd = _ffn_casting(x2, p["norm_ffn"][i], p["w_up"], p["w_down"], i)
            made["ffn"].append((wu, wd))
        else:
            x2 = _ffn(x2, p["norm_ffn"][i], *bf16["ffn"][i])
    return x2.reshape(b, t, d), jnp.stack(pool_new), jnp.stack(sbk_new), jnp.stack(sbv_new), made


def kernel(x_prompt, x_sample, mem_prompt, cache_pool, cache_sb_k, cache_sb_v, cache_mem_k, cache_mem_v,
           norm_mix, w_in_pool, w_in_sb, pool_w, pool_scale, norm_mem, w_mem_kv, q_norm, k_norm, w_out,
           norm_ffn, w_up, w_down):
    b, t, d = x_prompt.shape
    depth = w_out.shape[0]
    n_mem = mem_prompt.shape[1]
    p = dict(norm_mix=norm_mix, pool_scale=pool_scale, q_norm=q_norm, norm_ffn=norm_ffn,
             w_in_pool=w_in_pool.astype(BF16), w_in_sb=w_in_sb, pool_w=pool_w.astype(BF16),
             w_out=w_out, w_up=w_up, w_down=w_down)

    bs = x_sample.shape[0]
    past = cache_sb_k.shape[2]
    pool_prev = jnp.pad(cache_pool, ((0, 0), (0, 0), (POOL_HALO - POOL_STATE, 0), (0, 0)))
    head_major = lambda a: jnp.transpose(a, (0, 1, 3, 2, 4))
    mem_rows = lambda a, nb: a.reshape(depth, nb, n_mem * MEM_HEADS, HEAD_DIM)
    y_sample, pool_sample, sb_k_sample, sb_v_sample, bf16 = _run_trunk(
        x_sample, past, pool_prev, head_major(cache_sb_k), head_major(cache_sb_v),
        mem_rows(cache_mem_k, bs), mem_rows(cache_mem_v, bs), p, None)

    mk, mv = _mem_kv(mem_prompt.reshape(b * n_mem, d), norm_mem, w_mem_kv, k_norm)
    mk = mem_rows(mk, b)
    mv = mem_rows(mv, b)
    pool_zero = jnp.zeros((cache_pool.shape[0], b, POOL_HALO, D_TOK), F32)
    y_prompt, pool_prompt, sb_k_prompt, sb_v_prompt, _ = _run_trunk(
        x_prompt, 0, pool_zero, None, None, mk, mv, p, bf16)

    heads = lambda a: a.reshape(depth, b, n_mem, MEM_HEADS, HEAD_DIM)
    return (y_prompt, y_sample, pool_prompt, pool_sample, head_major(sb_k_prompt), head_major(sb_v_prompt),
            head_major(sb_k_sample), head_major(sb_v_sample), heads(mk), heads(mv))
```
